```python
import math
import jax, jax.numpy as jnp
from jax import lax
import numpy as np

D_MODEL = 2048
BATCH = 4
SEQ = 4096
DEPTH = 2

HEAD_DIM = 64
D_MIX = D_MODEL
FOX_HEADS = 8
FOX_W = FOX_HEADS * HEAD_DIM
FORGET_BIAS_INIT = 3.0
SWA_HEADS = 8
SWA_KV_HEADS = 2
SWA_GROUP = SWA_HEADS // SWA_KV_HEADS
SWA_W = SWA_HEADS * HEAD_DIM
SWA_KV_W = SWA_KV_HEADS * HEAD_DIM
SWA_WINDOW = 128
NSA_HEADS = 16
NSA_KV_HEADS = 2
NSA_GROUP = NSA_HEADS // NSA_KV_HEADS
NSA_W = NSA_HEADS * HEAD_DIM
NSA_KV_W = NSA_KV_HEADS * HEAD_DIM
NSA_CMP_LEN = 32
NSA_CMP_STRIDE = 16
NSA_CMP_HIDDEN = 256
NSA_SEL_LEN = 64
NSA_SEL_TOPK = 16
NSA_WINDOW = 512
NSA_N_BRANCH = 3
FORCE_SCORE = 1e9
REL_BUCKETS = 32
REL_MAX_DIST = 128
BIAS_HEADS = SWA_HEADS + NSA_HEADS
Q_BLOCK = 128
D_FF = 5632
N_EXPERTS = 8
TOP_K = 2
RMS_EPS = 1e-6
NEG_INF = -1e30
IN_SIZES = (FOX_W, FOX_W, FOX_W, FOX_HEADS,
            SWA_W, SWA_KV_W, SWA_KV_W,
            NSA_W, NSA_KV_W, NSA_KV_W, NSA_KV_W, NSA_KV_W, NSA_KV_W, NSA_KV_W,
            NSA_HEADS * NSA_N_BRANCH)
D_IN = 3 * FOX_W + FOX_HEADS + SWA_W + 2 * SWA_KV_W + NSA_W + 6 * NSA_KV_W + NSA_HEADS * NSA_N_BRANCH

kernel_name = 'hybrid_fox_swa_nsa_moe_trunk'


def _in_splits():
    return [int(s) for s in np.cumsum(IN_SIZES)[:-1]]


def rmsnorm(x, g):
    xf = x.astype(jnp.float32)
    y = xf * lax.rsqrt(jnp.mean(xf * xf, axis=-1, keepdims=True) + RMS_EPS)
    return (y * g.astype(jnp.float32)).astype(x.dtype)


def t5_bucket(dist):
    n = jnp.maximum(dist, 0)
    max_exact = REL_BUCKETS // 2
    nf = jnp.maximum(n, 1).astype(jnp.float32)
    large = max_exact + (jnp.log(nf / max_exact) / math.log(REL_MAX_DIST / max_exact)
                         * (REL_BUCKETS - max_exact)).astype(jnp.int32)
    large = jnp.minimum(large, REL_BUCKETS - 1)
    return jnp.where(n < max_exact, n, large)


def fox_attention(q, k, v, log_f):
    B, T, H, D = q.shape
    nb = T // Q_BLOCK
    scale = D ** -0.5
    c = jnp.cumsum(log_f, axis=1).transpose(0, 2, 1)
    key_pos = jnp.arange(T)
    q_blocks = q.reshape(B, nb, Q_BLOCK, H, D).transpose(1, 0, 2, 3, 4)
    c_blocks = c.reshape(B, H, nb, Q_BLOCK).transpose(2, 0, 1, 3)

    def one_block(args):
        i, qi, ci = args
        s = jnp.einsum('bqhd,bkhd->bhqk', qi, k).astype(jnp.float32) * scale
        s = s + ci[..., :, None] - c[..., None, :]
        q_pos = i * Q_BLOCK + jnp.arange(Q_BLOCK)
        s = jnp.where(key_pos[None, :] <= q_pos[:, None], s, NEG_INF)
        p = jax.nn.softmax(s, axis=-1)
        return jnp.einsum('bhqk,bkhd->bqhd', p.astype(v.dtype), v)

    o = lax.map(one_block, (jnp.arange(nb), q_blocks, c_blocks))
    return o.transpose(1, 0, 2, 3, 4).reshape(B, T, H * D)


def banded_gqa(q, k, v, window, bias_tab, sinks):
    B, T, G, R, D = q.shape
    nb = T // Q_BLOCK
    scale = D ** -0.5
    n_prev = (window - 1) // Q_BLOCK + 1
    pad = n_prev * Q_BLOCK
    L = (n_prev + 1) * Q_BLOCK
    kb = jnp.pad(k, ((0, 0), (pad, 0), (0, 0), (0, 0))).reshape(B, nb + n_prev, Q_BLOCK, G, D)
    vb = jnp.pad(v, ((0, 0), (pad, 0), (0, 0), (0, 0))).reshape(B, nb + n_prev, Q_BLOCK, G, D)
    k_band = jnp.concatenate([kb[:, j:j + nb] for j in range(n_prev + 1)], axis=2)
    v_band = jnp.concatenate([vb[:, j:j + nb] for j in range(n_prev + 1)], axis=2)
    qb = q.reshape(B, nb, Q_BLOCK, G, R, D)
    s = jnp.einsum('bnqgrd,bnkgd->bngrqk', qb, k_band).astype(jnp.float32) * scale
    qi = jnp.arange(Q_BLOCK)[:, None]
    kj = jnp.arange(L)[None, :]
    dist = qi + pad - kj
    key_abs = jnp.arange(nb)[:, None, None] * Q_BLOCK - pad + kj[None]
    mask = (dist >= 0)[None] & (dist < window)[None] & (key_abs >= 0)
    bias = bias_tab[t5_bucket(dist)].transpose(2, 0, 1).reshape(G, R, Q_BLOCK, L)
    s = jnp.where(mask[None, :, None, None], s + bias[None, None].astype(jnp.float32), NEG_INF)
    if sinks is not None:
        sink = jnp.broadcast_to(sinks.astype(jnp.float32).reshape(1, 1, G, R, 1, 1), s.shape[:-1] + (1,))
        p = jax.nn.softmax(jnp.concatenate([s, sink], axis=-1), axis=-1)[..., :-1]
    else:
        p = jax.nn.softmax(s, axis=-1)
    o = jnp.einsum('bngrqk,bnkgd->bnqgrd', p.astype(v.dtype), v_band)
    return o.reshape(B, T, G, R, D)


def compress_blocks(x_kv, pos_emb, w1, b1, w2, b2):
    B, T, G, D = x_kv.shape
    ratio = NSA_CMP_LEN // NSA_CMP_STRIDE
    chunks = x_kv.reshape(B, T // NSA_CMP_STRIDE, NSA_CMP_STRIDE, G, D)
    nc = T // NSA_CMP_STRIDE - ratio + 1
    blocks = jnp.concatenate([chunks[:, j:j + nc] for j in range(ratio)], axis=2)
    blocks = blocks + pos_emb[None, None, :, None, :]
    flat = blocks.transpose(0, 1, 3, 2, 4).reshape(B, nc, G, NSA_CMP_LEN * D)
    h = jax.nn.gelu(flat @ w1 + b1)
    return h @ w2 + b2


def selected_attention(q, k, v, sel_idx, bias_tab):
    B, T, G, R, D = q.shape
    nb = T // Q_BLOCK
    ns = T // NSA_SEL_LEN
    n_sel = sel_idx.shape[-1]
    scale = D ** -0.5
    k_blocks = k.reshape(B, ns, NSA_SEL_LEN, G, D).transpose(0, 3, 1, 2, 4)
    v_blocks = v.reshape(B, ns, NSA_SEL_LEN, G, D).transpose(0, 3, 1, 2, 4)
    tab_g = bias_tab.reshape(REL_BUCKETS, G, R).transpose(1, 0, 2)
    q_blocks = q.reshape(B, nb, Q_BLOCK, G, R, D).transpose(1, 0, 3, 4, 2, 5)
    idx_blocks = sel_idx.reshape(B, G, nb, Q_BLOCK, n_sel).transpose(2, 0, 1, 3, 4)
    bi = jnp.arange(B)[:, None, None, None]
    gi = jnp.arange(G)[None, :, None, None]

    def one_block(args):
        i, qi, ii = args
        kg = k_blocks[bi, gi, ii]
        vg = v_blocks[bi, gi, ii]
        s = jnp.einsum('bgrqd,bgqnld->bgrqnl', qi, kg).astype(jnp.float32) * scale
        q_pos = i * Q_BLOCK + jnp.arange(Q_BLOCK)
        k_pos = ii[..., None] * NSA_SEL_LEN + jnp.arange(NSA_SEL_LEN)
        dist = q_pos[None, None, :, None, None] - k_pos
        bias = tab_g[gi[..., None], t5_bucket(dist)].transpose(0, 1, 5, 2, 3, 4)
        s = jnp.where((dist >= 0)[:, :, None], s + bias.astype(jnp.float32), NEG_INF)
        p = jax.nn.softmax(s.reshape(B, G, R, Q_BLOCK, n_sel * NSA_SEL_LEN), axis=-1)
        p = p.reshape(B, G, R, Q_BLOCK, n_sel, NSA_SEL_LEN)
        return jnp.einsum('bgrqnl,bgqnld->bqgrd', p.astype(vg.dtype), vg)

    o = lax.map(one_block, (jnp.arange(nb), q_blocks, idx_blocks))
    return o.transpose(1, 0, 2, 3, 4, 5).reshape(B, T, G, R, D)


def nsa_attention(q, kc_raw, vc_raw, ks, vs, kw, vw, gate_logits,
                  cmp_pos, cmp_w1, cmp_b1, cmp_w2, cmp_b2, bias_tab):
    B, T, G, R, D = q.shape
    scale = D ** -0.5
    kc = compress_blocks(kc_raw, cmp_pos[0], cmp_w1[0], cmp_b1[0], cmp_w2[0], cmp_b2[0])
    vc = compress_blocks(vc_raw, cmp_pos[1], cmp_w1[1], cmp_b1[1], cmp_w2[1], cmp_b2[1])
    nc = kc.shape[1]
    s = jnp.einsum('btgrd,bngd->bgrtn', q, kc).astype(jnp.float32) * scale
    t = jnp.arange(T)
    dist = t[:, None] - (jnp.arange(nc) * NSA_CMP_STRIDE + NSA_CMP_LEN - 1)[None, :]
    valid = dist >= 0
    bias = bias_tab[t5_bucket(dist)].transpose(2, 0, 1).reshape(G, R, T, nc)
    s = jnp.where(valid, s + bias.astype(jnp.float32), NEG_INF)
    p_cmp = jnp.where(valid, jax.nn.softmax(s, axis=-1), 0.0)
    o_cmp = jnp.einsum('bgrtn,bngd->btgrd', p_cmp.astype(vc.dtype), vc)
    ratio = NSA_CMP_LEN // NSA_CMP_STRIDE
    imp = jnp.sum(p_cmp, axis=2)
    imp_chunk = jnp.pad(imp, ((0, 0), (0, 0), (0, 0), (0, ratio - 1)))
    for j in range(1, ratio):
        imp_chunk = imp_chunk + jnp.pad(imp, ((0, 0), (0, 0), (0, 0), (j, ratio - 1 - j)))
    ns = T // NSA_SEL_LEN
    imp_sel = imp_chunk.reshape(B, G, T, ns, NSA_SEL_LEN // NSA_CMP_STRIDE).sum(axis=-1)
    sb = jnp.arange(ns)[None, :]
    tb = (t // NSA_SEL_LEN)[:, None]
    forced = (sb == 0) | (sb == tb) | (sb == tb - 1)
    score = jnp.where(forced, FORCE_SCORE, jnp.where(sb <= tb, imp_sel, NEG_INF))
    n_sel = min(NSA_SEL_TOPK, ns)
    _, sel_idx = lax.top_k(score, n_sel)
    o_slc = selected_attention(q, ks, vs, sel_idx, bias_tab)
    o_win = banded_gqa(q, kw, vw, NSA_WINDOW, bias_tab, None)
    g = jax.nn.sigmoid(gate_logits)
    return g[..., 0:1] * o_cmp + g[..., 1:2] * o_slc + g[..., 2:3] * o_win


def hybrid_mixer(hn, w_in_l, forget_bias, sinks, cmp_pos, cmp_w1, cmp_b1, cmp_w2, cmp_b2, w_out_l, rel_tab):
    B, T, _ = hn.shape
    D = HEAD_DIM
    proj = hn @ w_in_l
    (fq, fk, fv, ff, sq, sk, sv, nq, nkc, nvc, nks, nvs, nkw, nvw, ng) = jnp.split(proj, _in_splits(), axis=-1)
    log_f = jax.nn.log_sigmoid(ff.astype(jnp.float32) + forget_bias.astype(jnp.float32))
    o_fox = fox_attention(fq.reshape(B, T, FOX_HEADS, D), fk.reshape(B, T, FOX_HEADS, D),
                          fv.reshape(B, T, FOX_HEADS, D), log_f)
    swa_tab = rel_tab[:, :SWA_HEADS]
    nsa_tab = rel_tab[:, SWA_HEADS:]
    o_swa = banded_gqa(sq.reshape(B, T, SWA_KV_HEADS, SWA_GROUP, D), sk.reshape(B, T, SWA_KV_HEADS, D),
                       sv.reshape(B, T, SWA_KV_HEADS, D), SWA_WINDOW, swa_tab, sinks)
    kv_shape = (B, T, NSA_KV_HEADS, D)
    o_nsa = nsa_attention(nq.reshape(B, T, NSA_KV_HEADS, NSA_GROUP, D),
                          nkc.reshape(kv_shape), nvc.reshape(kv_shape), nks.reshape(kv_shape),
                          nvs.reshape(kv_shape), nkw.reshape(kv_shape), nvw.reshape(kv_shape),
                          ng.reshape(B, T, NSA_KV_HEADS, NSA_GROUP, NSA_N_BRANCH),
                          cmp_pos, cmp_w1, cmp_b1, cmp_w2, cmp_b2, nsa_tab)
    mixed = jnp.concatenate([o_fox, o_swa.reshape(B, T, SWA_W), o_nsa.reshape(B, T, NSA_W)], axis=-1)
    return mixed @ w_out_l


def swiglu(h, w_gate, w_up, w_down):
    return (jax.nn.silu(h @ w_gate) * (h @ w_up)) @ w_down


def moe_swiglu(h, router, w_gate, w_up, w_down):
    logits = (h @ router).astype(jnp.float32)
    top_val, top_idx = lax.top_k(logits, TOP_K)
    top_p = jax.nn.softmax(top_val, axis=-1)
    gates = jnp.sum(jax.nn.one_hot(top_idx, N_EXPERTS, dtype=jnp.float32) * top_p[..., None], axis=-2)
    out = jnp.zeros_like(h)
    for e in range(N_EXPERTS):
        out = out + gates[..., e:e + 1].astype(h.dtype) * swiglu(h, w_gate[e], w_up[e], w_down[e])
    return out


def setup_inputs(seed: int = 0) -> dict:
    key = jax.random.key(seed)
    ks = jax.random.split(key, 24)
    f32 = jnp.float32
    n_dense = (DEPTH + 1) // 2
    n_moe = DEPTH // 2

    def nrm(k, shape, scale):
        return scale * jax.random.normal(k, shape, f32)

    cmp_in = NSA_CMP_LEN * HEAD_DIM
    return {
        'x': nrm(ks[0], (BATCH, SEQ, D_MODEL), 1.0),
        'attn_norm': 1.0 + nrm(ks[1], (DEPTH, D_MODEL), 0.05),
        'w_in': nrm(ks[2], (DEPTH, D_MODEL, D_IN), D_MODEL ** -0.5),
        'fox_forget_bias': FORGET_BIAS_INIT + nrm(ks[3], (DEPTH, FOX_HEADS), 0.5),
        'swa_sinks': nrm(ks[4], (DEPTH, SWA_HEADS), 0.5),
        'nsa_cmp_pos': nrm(ks[5], (DEPTH, 2, NSA_CMP_LEN, HEAD_DIM), 0.1),
        'nsa_cmp_w1': nrm(ks[6], (DEPTH, 2, cmp_in, NSA_CMP_HIDDEN), cmp_in ** -0.5),
        'nsa_cmp_b1': nrm(ks[7], (DEPTH, 2, NSA_CMP_HIDDEN), 0.01),
        'nsa_cmp_w2': nrm(ks[8], (DEPTH, 2, NSA_CMP_HIDDEN, HEAD_DIM), NSA_CMP_HIDDEN ** -0.5),
        'nsa_cmp_b2': nrm(ks[9], (DEPTH, 2, HEAD_DIM), 0.01),
        'w_out': nrm(ks[10], (DEPTH, D_MIX, D_MODEL), D_MIX ** -0.5),
        'rel_bias_table': nrm(ks[11], (REL_BUCKETS, BIAS_HEADS), 0.5),
        'ffn_norm': 1.0 + nrm(ks[12], (DEPTH, D_MODEL), 0.05),
        'dense_w_gate': nrm(ks[13], (n_dense, D_MODEL, D_FF), D_MODEL ** -0.5),
        'dense_w_up': nrm(ks[14], (n_dense, D_MODEL, D_FF), D_MODEL ** -0.5),
        'dense_w_down': nrm(ks[15], (n_dense, D_FF, D_MODEL), D_FF ** -0.5),
        'moe_router': nrm(ks[16], (n_moe, D_MODEL, N_EXPERTS), D_MODEL ** -0.5),
        'moe_w_gate': nrm(ks[17], (n_moe, N_EXPERTS, D_MODEL, D_FF), D_MODEL ** -0.5),
        'moe_w_up': nrm(ks[18], (n_moe, N_EXPERTS, D_MODEL, D_FF), D_MODEL ** -0.5),
        'moe_w_down': nrm(ks[19], (n_moe, N_EXPERTS, D_FF, D_MODEL), D_FF ** -0.5),
        'final_norm': 1.0 + nrm(ks[20], (D_MODEL,), 0.05),
    }


def reference(x, attn_norm, w_in, fox_forget_bias, swa_sinks, nsa_cmp_pos, nsa_cmp_w1, nsa_cmp_b1,
              nsa_cmp_w2, nsa_cmp_b2, w_out, rel_bias_table, ffn_norm, dense_w_gate, dense_w_up,
              dense_w_down, moe_router, moe_w_gate, moe_w_up, moe_w_down, final_norm):
    h = x
    for layer in range(DEPTH):
        hn = rmsnorm(h, attn_norm[layer])
        h = h + hybrid_mixer(hn, w_in[layer], fox_forget_bias[layer], swa_sinks[layer],
                             nsa_cmp_pos[layer], nsa_cmp_w1[layer], nsa_cmp_b1[layer],
                             nsa_cmp_w2[layer], nsa_cmp_b2[layer], w_out[layer], rel_bias_table)
        hn = rmsnorm(h, ffn_norm[layer])
        if layer % 2 == 0:
            i = layer // 2
            h = h + swiglu(hn, dense_w_gate[i], dense_w_up[i], dense_w_down[i])
        else:
            i = layer // 2
            h = h + moe_swiglu(hn, moe_router[i], moe_w_gate[i], moe_w_up[i], moe_w_down[i])
    return rmsnorm(h, final_norm)
```

```python
import functools
import math

import numpy as np
import jax
import jax.numpy as jnp
from jax import lax
from jax.experimental import pallas as pl
from jax.experimental.pallas import tpu as pltpu

F32 = jnp.float32
BF16 = jnp.bfloat16

D_MODEL = 2048
HEAD_DIM = 64
FOX_HEADS = 8
FOX_W = FOX_HEADS * HEAD_DIM
SWA_HEADS = 8
SWA_KV_HEADS = 2
SWA_GROUP = SWA_HEADS // SWA_KV_HEADS
SWA_W = SWA_HEADS * HEAD_DIM
SWA_KV_W = SWA_KV_HEADS * HEAD_DIM
SWA_WINDOW = 128
NSA_HEADS = 16
NSA_KV_HEADS = 2
NSA_GROUP = NSA_HEADS // NSA_KV_HEADS
NSA_W = NSA_HEADS * HEAD_DIM
NSA_KV_W = NSA_KV_HEADS * HEAD_DIM
NSA_CMP_LEN = 32
NSA_CMP_STRIDE = 16
NSA_CMP_HIDDEN = 256
NSA_SEL_LEN = 64
NSA_SEL_TOPK = 16
NSA_WINDOW = 512
NSA_N_BRANCH = 3
FORCE_SCORE = 1e9
REL_BUCKETS = 32
REL_MAX_DIST = 128
D_FF = 5632
N_EXPERTS = 8
TOP_K = 2
RMS_EPS = 1e-6
NEG_INF = -1e30
M_INIT = -1e29
REMOVED = -3e38

LANES = 128
QB = 128
VMEM_LIMIT = 56 * 1024 * 1024

C_FQ, C_FK, C_FV = 0, 512, 1024
C_SQ = 1536
C_NQ = 2048
C_SK, C_SV = 3072, 3328
C_NKS, C_NVS = 3584, 3840
C_NKW, C_NVW = 4096, 4352
C_NKC, C_NVC = 4608, 4736
D_PROJ = 4864
G_NG = FOX_HEADS
D_GATE = 128


def _cparams(sem, vmem=VMEM_LIMIT):
    return pltpu.CompilerParams(dimension_semantics=sem, vmem_limit_bytes=vmem)


def _dot(a, b):
    return jnp.dot(a, b, preferred_element_type=F32)


def _dot_nt(a, b):
    return lax.dot_general(a, b, (((1,), (1,)), ((), ())), preferred_element_type=F32)


def _split3(x):
    hi = x.astype(BF16)
    r1 = x - hi.astype(F32)
    mid = r1.astype(BF16)
    lo = (r1 - mid.astype(F32)).astype(BF16)
    return hi, mid, lo


def _rmsnorm_kernel(x_ref, g_ref, o_ref):
    x = x_ref[...]
    y = x * lax.rsqrt(jnp.mean(x * x, axis=-1, keepdims=True) + RMS_EPS)
    o_ref[...] = (y * g_ref[...]).astype(o_ref.dtype)


def _rmsnorm(h, g, out_dtype, tm=512):
    n, d = h.shape
    return pl.pallas_call(
        _rmsnorm_kernel,
        grid=(n // tm,),
        in_specs=[pl.BlockSpec((tm, d), lambda i: (i, 0)),
                  pl.BlockSpec((1, d), lambda i: (0, 0))],
        out_specs=pl.BlockSpec((tm, d), lambda i: (i, 0)),
        out_shape=jax.ShapeDtypeStruct((n, d), out_dtype),
        compiler_params=_cparams(("parallel",)),
        name="rmsnorm",
    )(h, g.reshape(1, d).astype(F32))


def _mm_kernel(*refs, n_in, has_res):
    o_ref = refs[-1]
    acc = None
    for a in range(n_in):
        d = _dot(refs[a][...], refs[n_in + a][...])
        acc = d if acc is None else acc + d
    if has_res:
        acc = acc + refs[2 * n_in][...]
    o_ref[...] = acc.astype(o_ref.dtype)


def _matmul(xs, ws, out_dtype, residual=None, tm=1024, tn=256, name="matmul"):
    n = xs[0].shape[0]
    m = ws[0].shape[1]
    tm = min(tm, n)
    tn = min(tn, m)
    in_specs = [pl.BlockSpec((tm, x.shape[1]), lambda i, j: (i, 0)) for x in xs]
    in_specs += [pl.BlockSpec((w.shape[0], tn), lambda i, j: (0, j)) for w in ws]
    args = list(xs) + list(ws)
    if residual is not None:
        in_specs.append(pl.BlockSpec((tm, tn), lambda i, j: (i, j)))
        args.append(residual)
    return pl.pallas_call(
        functools.partial(_mm_kernel, n_in=len(xs), has_res=residual is not None),
        grid=(n // tm, m // tn),
        in_specs=in_specs,
        out_specs=pl.BlockSpec((tm, tn), lambda i, j: (i, j)),
        out_shape=jax.ShapeDtypeStruct((n, m), out_dtype),
        compiler_params=_cparams(("parallel", "arbitrary")),
        name=name,
    )(*args)


def _cumsum_kernel(g_ref, b_ref, o_ref, carry_ref, *, tc):
    @pl.when(pl.program_id(1) == 0)
    def _():
        carry_ref[...] = jnp.zeros_like(carry_ref)

    z = g_ref[...] + b_ref[...]
    log_f = jnp.minimum(z, 0.0) - jnp.log1p(jnp.exp(-jnp.abs(z)))
    row = lax.broadcasted_iota(jnp.int32, (tc, tc), 0)
    col = lax.broadcasted_iota(jnp.int32, (tc, tc), 1)
    tri = jnp.where(col <= row, 1.0, 0.0).astype(BF16)
    hi, mid, lo = _split3(log_f)
    c = _dot(tri, hi) + _dot(tri, mid) + _dot(tri, lo) + carry_ref[...]
    o_ref[...] = c
    carry_ref[...] = c[tc - 1:tc, :]


def _fox_cumsum(gates, forget_bias, b, t, tc=256):
    nt = t // tc
    bias = jnp.zeros((1, D_GATE), F32).at[0, :FOX_HEADS].set(forget_bias.astype(F32))
    return pl.pallas_call(
        functools.partial(_cumsum_kernel, tc=tc),
        grid=(b, nt),
        in_specs=[pl.BlockSpec((tc, D_GATE), lambda bi, ti: (bi * nt + ti, 0)),
                  pl.BlockSpec((1, D_GATE), lambda bi, ti: (0, 0))],
        out_specs=pl.BlockSpec((tc, D_GATE), lambda bi, ti: (bi * nt + ti, 0)),
        out_shape=jax.ShapeDtypeStruct((b * t, D_GATE), F32),
        scratch_shapes=[pltpu.VMEM((1, D_GATE), F32)],
        compiler_params=_cparams(("parallel", "arbitrary")),
        name="fox_cumsum",
    )(gates, bias)


def _fox_kernel(q_ref, k_ref, v_ref, cc_ref, cr_ref, o_ref, m_sc, l_sc, acc_sc, *, tq):
    i = pl.program_id(1)
    j = pl.program_id(2)
    lane = lax.broadcasted_iota(jnp.int32, (1, LANES), 1)
    upper = lane >= HEAD_DIM

    @pl.when(j == 0)
    def _():
        m_sc[...] = jnp.full_like(m_sc, M_INIT)
        l_sc[...] = jnp.zeros_like(l_sc)
        acc_sc[...] = jnp.zeros_like(acc_sc)

    def step(diagonal):
        if diagonal:
            row = lax.broadcasted_iota(jnp.int32, (tq, tq), 0)
            col = lax.broadcasted_iota(jnp.int32, (tq, tq), 1)
            causal = col <= row
        for p in range(FOX_HEADS // 2):
            q2 = q_ref[:, p * LANES:(p + 1) * LANES]
            k2 = k_ref[:, p * LANES:(p + 1) * LANES]
            v2 = v_ref[:, p * LANES:(p + 1) * LANES]
            for a in range(2):
                h = 2 * p + a
                qa = jnp.where(upper, q2, jnp.zeros_like(q2)) if a else jnp.where(upper, jnp.zeros_like(q2), q2)
                s = _dot_nt(qa, k2)
                s = s + (cc_ref[:, h:h + 1] - cr_ref[0, h:h + 1, :])
                if diagonal:
                    s = jnp.where(causal, s, NEG_INF)
                m_prev = m_sc[h]
                m_new = jnp.maximum(m_prev, jnp.max(s, axis=1, keepdims=True))
                alpha = jnp.exp(m_prev - m_new)
                pr = jnp.exp(s - m_new)
                l_sc[h] = alpha * l_sc[h] + jnp.sum(pr, axis=1, keepdims=True)
                acc_sc[h] = alpha * acc_sc[h] + _dot(pr.astype(BF16), v2)
                m_sc[h] = m_new

    @pl.when(j < i)
    def _():
        step(False)

    @pl.when(j == i)
    def _():
        step(True)
        for p in range(FOX_HEADS // 2):
            o0 = acc_sc[2 * p] / l_sc[2 * p]
            o1 = acc_sc[2 * p + 1] / l_sc[2 * p + 1]
            o_ref[:, p * LANES:(p + 1) * LANES] = jnp.where(upper, o1, o0).astype(o_ref.dtype)


def _fox_attention(proj, c, c_t, b, t, tq=512):
    nt = t // tq
    blk = FOX_W // FOX_W
    del blk
    return pl.pallas_call(
        functools.partial(_fox_kernel, tq=tq),
        grid=(b, nt, nt),
        in_specs=[
            pl.BlockSpec((tq, FOX_W), lambda bi, i, j: (bi * nt + i, C_FQ // FOX_W)),
            pl.BlockSpec((tq, FOX_W), lambda bi, i, j: (bi * nt + jnp.minimum(j, i), C_FK // FOX_W)),
            pl.BlockSpec((tq, FOX_W), lambda bi, i, j: (bi * nt + jnp.minimum(j, i), C_FV // FOX_W)),
            pl.BlockSpec((tq, D_GATE), lambda bi, i, j: (bi * nt + i, 0)),
            pl.BlockSpec((1, FOX_HEADS, tq), lambda bi, i, j: (bi, 0, jnp.minimum(j, i))),
        ],
        out_specs=pl.BlockSpec((tq, FOX_W), lambda bi, i, j: (bi * nt + i, 0)),
        out_shape=jax.ShapeDtypeStruct((b * t, FOX_W), BF16),
        scratch_shapes=[pltpu.VMEM((FOX_HEADS, tq, 1), F32),
                        pltpu.VMEM((FOX_HEADS, tq, 1), F32),
                        pltpu.VMEM((FOX_HEADS, tq, LANES), F32)],
        compiler_params=_cparams(("parallel", "parallel", "arbitrary")),
        name="fox_attention",
    )(proj, proj, proj, c, c_t)


def _stack_heads(q_ref, hg):
    lane = lax.broadcasted_iota(jnp.int32, (1, LANES), 1)
    upper = lane >= HEAD_DIM
    qs = []
    for p in range(hg // 2):
        q2 = q_ref[:, p * LANES:(p + 1) * LANES]
        qs.append(jnp.where(upper, jnp.zeros_like(q2), q2))
        qs.append(jnp.where(upper, q2, jnp.zeros_like(q2)))
    return jnp.concatenate(qs, axis=0)


def _store_heads(o_ref, o3, hg):
    lane = lax.broadcasted_iota(jnp.int32, (1, LANES), 1)
    upper = lane >= HEAD_DIM
    for p in range(hg // 2):
        o_ref[:, p * LANES:(p + 1) * LANES] = jnp.where(upper, o3[2 * p + 1], o3[2 * p]).astype(o_ref.dtype)


def _band_kernel(*refs, mode, hg, chunk):
    if mode == "swa":
        q_ref, k_ref, v_ref, band_ref, sink_ref, o_ref = refs
    elif mode == "sel":
        q_ref, k_ref, v_ref, band_ref, sel_ref, e_ref, o_ref = refs
    else:
        q_ref, k_ref, v_ref, band_ref, o_ref = refs
    g = pl.program_id(1)
    i = pl.program_id(2)
    rows = hg * QB
    qstack = _stack_heads(q_ref, hg)

    def kv_block(ref, blk):
        return ref[pl.ds(pl.multiple_of(blk * QB, QB), QB), :]

    ip = jnp.maximum(i - 1, 0)
    k_near = jnp.concatenate([kv_block(k_ref, ip), kv_block(k_ref, i)], axis=0)
    v_near = jnp.concatenate([kv_block(v_ref, ip), kv_block(v_ref, i)], axis=0)
    s = _dot_nt(qstack, k_near).reshape(hg, QB, 2 * QB) + band_ref[...]
    col = lax.broadcasted_iota(jnp.int32, (1, 1, 2 * QB), 2)
    s = jnp.where(jnp.logical_or(col >= QB, i > 0), s, NEG_INF)

    if mode == "swa":
        hidx = lax.broadcasted_iota(jnp.int32, (hg, 1, 1), 0)
        sink = jnp.zeros((hg, 1, 1), F32)
        for h in range(hg):
            sink = jnp.where(hidx == h, sink_ref[g * hg + h], sink)
        m = jnp.maximum(jnp.max(s, axis=2, keepdims=True), sink)
        e = jnp.exp(s - m)
        l = jnp.sum(e, axis=2, keepdims=True) + jnp.exp(sink - m)
        o = _dot(e.astype(BF16).reshape(rows, 2 * QB), v_near).reshape(hg, QB, LANES) / l

    elif mode == "win":
        n_far = NSA_WINDOW // QB - 1
        backs = range(n_far + 1, 1, -1)
        k_far = jnp.concatenate([kv_block(k_ref, jnp.maximum(i - bk, 0)) for bk in backs], axis=0)
        v_far = jnp.concatenate([kv_block(v_ref, jnp.maximum(i - bk, 0)) for bk in backs], axis=0)
        s_far = _dot_nt(qstack, k_far).reshape(hg, QB, n_far * QB)
        kk = lax.broadcasted_iota(jnp.int32, (1, QB, n_far * QB), 2)
        qq = lax.broadcasted_iota(jnp.int32, (1, QB, n_far * QB), 1)
        ok = jnp.logical_and(jnp.logical_or(kk >= QB, kk > qq), kk >= (n_far + 1 - i) * QB)
        s_far = jnp.where(ok, s_far, NEG_INF)
        m = jnp.maximum(jnp.max(s, axis=2, keepdims=True), jnp.max(s_far, axis=2, keepdims=True))
        e_n = jnp.exp(s - m)
        e_f = jnp.exp(s_far - m)
        l = jnp.sum(e_n, axis=2, keepdims=True) + jnp.sum(e_f, axis=2, keepdims=True)
        o = (_dot(e_n.astype(BF16).reshape(rows, 2 * QB), v_near)
             + _dot(e_f.astype(BF16).reshape(rows, n_far * QB), v_far)).reshape(hg, QB, LANES) / l

    else:
        sel = sel_ref[0, 0]
        ns = sel.shape[1]
        e_prev = e_ref[:, pl.ds(pl.multiple_of(ip * QB, QB), QB)]
        m_prev = _dot(sel, e_prev)
        near_mask = jnp.concatenate([m_prev, jnp.ones((QB, QB), F32)], axis=1)
        s = jnp.where(near_mask[None] > 0.5, s, NEG_INF)
        sb = lax.broadcasted_iota(jnp.int32, (1, ns), 1)
        sel_far = jnp.where(sb < 2 * (i - 1), sel, jnp.zeros_like(sel))
        per = chunk // QB
        n_chunks = (jnp.maximum(i - 1, 0) + per - 1) // per

        def body(c, carry):
            m, l, acc = carry
            off = pl.multiple_of(c * chunk, chunk)
            kc = k_ref[pl.ds(off, chunk), :]
            vc = v_ref[pl.ds(off, chunk), :]
            mk = _dot(sel_far, e_ref[:, pl.ds(off, chunk)])
            sf = _dot_nt(qstack, kc).reshape(hg, QB, chunk)
            sf = jnp.where(mk[None] > 0.5, sf, NEG_INF)
            m_new = jnp.maximum(m, jnp.max(sf, axis=2, keepdims=True))
            alpha = jnp.exp(m - m_new)
            pf = jnp.exp(sf - m_new)
            l = alpha * l + jnp.sum(pf, axis=2, keepdims=True)
            acc = alpha * acc + _dot(pf.astype(BF16).reshape(rows, chunk), vc).reshape(hg, QB, LANES)
            return m_new, l, acc

        m0 = jnp.full((hg, QB, 1), M_INIT, F32)
        l0 = jnp.zeros((hg, QB, 1), F32)
        a0 = jnp.zeros((hg, QB, LANES), F32)
        m, l, acc = lax.fori_loop(0, n_chunks, body, (m0, l0, a0))
        m_new = jnp.maximum(m, jnp.max(s, axis=2, keepdims=True))
        alpha = jnp.exp(m - m_new)
        e = jnp.exp(s - m_new)
        l = alpha * l + jnp.sum(e, axis=2, keepdims=True)
        o = (alpha * acc + _dot(e.astype(BF16).reshape(rows, 2 * QB), v_near).reshape(hg, QB, LANES)) / l

    _store_heads(o_ref, o, hg)


def _band_attention(proj, band, b, t, *, mode, hg, c_q, c_k, c_v, sinks=None, sel=None, emat=None):
    nb = t // QB
    n_groups = 2
    qw = hg * HEAD_DIM
    chunk = min(512, t)
    in_specs = [
        pl.BlockSpec((QB, qw), lambda bi, g, i: (bi * nb + i, c_q // qw + g)),
        pl.BlockSpec((t, LANES), lambda bi, g, i: (bi, c_k // LANES + g)),
        pl.BlockSpec((t, LANES), lambda bi, g, i: (bi, c_v // LANES + g)),
        pl.BlockSpec((hg, QB, 2 * QB), lambda bi, g, i: (g, 0, 0)),
    ]
    args = [proj, proj, proj, band]
    if mode == "swa":
        in_specs.append(pl.BlockSpec(memory_space=pltpu.SMEM))
        args.append(sinks.astype(F32))
    if mode == "sel":
        ns = t // NSA_SEL_LEN
        in_specs.append(pl.BlockSpec((1, 1, QB, ns), lambda bi, g, i: (bi, g, i, 0)))
        in_specs.append(pl.BlockSpec((ns, t), lambda bi, g, i: (0, 0)))
        args += [sel, emat]
    return pl.pallas_call(
        functools.partial(_band_kernel, mode=mode, hg=hg, chunk=chunk),
        grid=(b, n_groups, nb),
        in_specs=in_specs,
        out_specs=pl.BlockSpec((QB, qw), lambda bi, g, i: (bi * nb + i, g)),
        out_shape=jax.ShapeDtypeStruct((b * t, n_groups * qw), BF16),
        compiler_params=_cparams(("parallel", "parallel", "arbitrary")),
        name="band_" + mode,
    )(*args)


def _compress_kernel(x_ref, pos_ref, w1_ref, b1_ref, w2_ref, b2_ref, o_ref):
    x = (x_ref[0].astype(F32) + pos_ref[0]).astype(BF16)
    hid = jax.nn.gelu(_dot(x, w1_ref[0]) + b1_ref[0])
    o_ref[0] = (_dot(hid.astype(BF16), w2_ref[0]) + b2_ref[0]).astype(o_ref.dtype)


def _compress(flat, pos, w1, b1, w2d, b2d, tr=256):
    _, r, cin = flat.shape
    return pl.pallas_call(
        _compress_kernel,
        grid=(2, r // tr),
        in_specs=[pl.BlockSpec((1, tr, cin), lambda s, i: (s, i, 0)),
                  pl.BlockSpec((1, 1, cin), lambda s, i: (s, 0, 0)),
                  pl.BlockSpec((1, cin, NSA_CMP_HIDDEN), lambda s, i: (s, 0, 0)),
                  pl.BlockSpec((1, 1, NSA_CMP_HIDDEN), lambda s, i: (s, 0, 0)),
                  pl.BlockSpec((1, NSA_CMP_HIDDEN, LANES), lambda s, i: (s, 0, 0)),
                  pl.BlockSpec((1, 1, LANES), lambda s, i: (s, 0, 0))],
        out_specs=pl.BlockSpec((1, tr, LANES), lambda s, i: (s, i, 0)),
        out_shape=jax.ShapeDtypeStruct((2, r, LANES), BF16),
        compiler_params=_cparams(("parallel", "parallel")),
        name="nsa_compress",
    )(flat, pos, w1, b1, w2d, b2d)


def _cmp_kernel(q_ref, kc_ref, vc_ref, dbase_ref, mmat_ref, o_ref, sel_ref, *, hg, ncp, ns, n_sel):
    i = pl.program_id(2)
    rows = hg * QB
    qstack = _stack_heads(q_ref, hg)
    s = _dot_nt(qstack, kc_ref[0, 0]).reshape(hg, QB, ncp)
    shift = lax.rem(8 * i - 9 + ncp, ncp)
    delta = jnp.stack([pltpu.roll(dbase_ref[h], shift, 1) for h in range(hg)], axis=0)
    qq = lax.broadcasted_iota(jnp.int32, (1, QB, ncp), 1)
    nn = lax.broadcasted_iota(jnp.int32, (1, QB, ncp), 2)
    dist = i * QB + qq - NSA_CMP_STRIDE * nn - (NSA_CMP_LEN - 1)
    valid = dist >= 0
    s = jnp.where(valid, s + delta, NEG_INF)
    m = jnp.max(s, axis=2, keepdims=True)
    e = jnp.exp(s - m)
    p = jnp.where(valid, e / jnp.sum(e, axis=2, keepdims=True), 0.0)
    o = _dot(p.astype(BF16).reshape(rows, ncp), vc_ref[0, 0]).reshape(hg, QB, LANES)
    _store_heads(o_ref, o, hg)

    imp = jnp.sum(p, axis=0)
    hi, mid, lo = _split3(imp)
    mm = mmat_ref[...]
    imp_sel = _dot(hi, mm) + _dot(mid, mm) + _dot(lo, mm)
    sb = lax.broadcasted_iota(jnp.int32, (QB, ns), 1)
    tq = lax.broadcasted_iota(jnp.int32, (QB, ns), 0)
    tb = (i * QB + tq) // NSA_SEL_LEN
    forced = jnp.logical_or(jnp.logical_or(sb == 0, sb == tb), sb == tb - 1)
    score = jnp.where(forced, FORCE_SCORE, jnp.where(sb <= tb, imp_sel, NEG_INF))
    sbf = sb.astype(F32)
    chosen = jnp.zeros((QB, ns), F32)
    for _ in range(n_sel):
        mx = jnp.max(score, axis=1, keepdims=True)
        first = jnp.min(jnp.where(score == mx, sbf, float(ns)), axis=1, keepdims=True)
        pick = sbf == first
        chosen = jnp.where(pick, 1.0, chosen)
        score = jnp.where(pick, REMOVED, score)
    sel_ref[0, 0] = chosen.astype(sel_ref.dtype)


def _cmp_attention(proj, kc, vc, dbase, mmat, b, t):
    nb = t // QB
    hg = NSA_GROUP
    qw = hg * HEAD_DIM
    ncp = t // NSA_CMP_STRIDE
    ns = t // NSA_SEL_LEN
    n_sel = min(NSA_SEL_TOPK, ns)
    return pl.pallas_call(
        functools.partial(_cmp_kernel, hg=hg, ncp=ncp, ns=ns, n_sel=n_sel),
        grid=(b, NSA_KV_HEADS, nb),
        in_specs=[
            pl.BlockSpec((QB, qw), lambda bi, g, i: (bi * nb + i, C_NQ // qw + g)),
            pl.BlockSpec((1, 1, ncp, LANES), lambda bi, g, i: (bi, g, 0, 0)),
            pl.BlockSpec((1, 1, ncp, LANES), lambda bi, g, i: (bi, g, 0, 0)),
            pl.BlockSpec((hg, QB, ncp), lambda bi, g, i: (g, 0, 0)),
            pl.BlockSpec((ncp, ns), lambda bi, g, i: (0, 0)),
        ],
        out_specs=[pl.BlockSpec((QB, qw), lambda bi, g, i: (bi * nb + i, g)),
                   pl.BlockSpec((1, 1, QB, ns), lambda bi, g, i: (bi, g, i, 0))],
        out_shape=[jax.ShapeDtypeStruct((b * t, NSA_W), BF16),
                   jax.ShapeDtypeStruct((b, NSA_KV_HEADS, t, ns), BF16)],
        compiler_params=_cparams(("parallel", "parallel", "arbitrary")),
        name="nsa_cmp_select",
    )(proj, kc, vc, dbase, mmat)


def _combine_kernel(oc_ref, os_ref, ow_ref, g_ref, o_ref):
    sg = jax.nn.sigmoid(g_ref[...])
    lane = lax.broadcasted_iota(jnp.int32, (1, LANES), 1)
    upper = lane >= HEAD_DIM
    for p in range(NSA_HEADS // 2):
        acc = None
        for br, ref in enumerate((oc_ref, os_ref, ow_ref)):
            c0 = G_NG + NSA_N_BRANCH * (2 * p) + br
            c1 = G_NG + NSA_N_BRANCH * (2 * p + 1) + br
            gate = jnp.where(upper, sg[:, c1:c1 + 1], sg[:, c0:c0 + 1])
            term = gate * ref[:, p * LANES:(p + 1) * LANES].astype(F32)
            acc = term if acc is None else acc + term
        o_ref[:, p * LANES:(p + 1) * LANES] = acc.astype(o_ref.dtype)


def _nsa_combine(o_cmp, o_slc, o_win, gates, tm=512):
    n = o_cmp.shape[0]
    spec = pl.BlockSpec((tm, NSA_W), lambda i: (i, 0))
    return pl.pallas_call(
        _combine_kernel,
        grid=(n // tm,),
        in_specs=[spec, spec, spec, pl.BlockSpec((tm, D_GATE), lambda i: (i, 0))],
        out_specs=spec,
        out_shape=jax.ShapeDtypeStruct((n, NSA_W), BF16),
        compiler_params=_cparams(("parallel",)),
        name="nsa_combine",
    )(o_cmp, o_slc, o_win, gates)


def _ffn_kernel(te_ref, nu_ref, *refs, has_res):
    if has_res:
        x_ref, wg_ref, wu_ref, wd_ref, res_ref, o_ref, acc_ref = refs
    else:
        x_ref, wg_ref, wu_ref, wd_ref, o_ref, acc_ref = refs
    del te_ref
    i = pl.program_id(0)
    k = pl.program_id(1)
    last = pl.num_programs(1) - 1
    used = i < nu_ref[0]

    @pl.when(jnp.logical_and(used, k == 0))
    def _():
        acc_ref[...] = jnp.zeros_like(acc_ref)

    @pl.when(used)
    def _():
        x = x_ref[...]
        gate = _dot(x, wg_ref[0])
        up = _dot(x, wu_ref[0])
        hid = (jax.nn.silu(gate) * up).astype(BF16)
        acc_ref[...] += _dot(hid, wd_ref[0])

    @pl.when(jnp.logical_and(used, k == last))
    def _():
        y = acc_ref[...]
        if has_res:
            y = y + res_ref[...]
        o_ref[...] = y.astype(o_ref.dtype)

    @pl.when(jnp.logical_and(jnp.logical_not(used), k == last))
    def _():
        o_ref[...] = jnp.zeros_like(o_ref)


def _ffn(x, w_gate, w_up, w_down, tile_expert, n_used, out_dtype, residual=None, tm=512, tf=512):
    r, d = x.shape
    nk = D_FF // tf
    n_tiles = r // tm

    def tile(i, nu):
        return jnp.minimum(i, nu[0] - 1)

    def kk(i, k, nu):
        return jnp.where(i < nu[0], k, nk - 1)

    in_specs = [
        pl.BlockSpec((tm, d), lambda i, k, te, nu: (tile(i, nu), 0)),
        pl.BlockSpec((1, d, tf), lambda i, k, te, nu: (te[tile(i, nu)], 0, kk(i, k, nu))),
        pl.BlockSpec((1, d, tf), lambda i, k, te, nu: (te[tile(i, nu)], 0, kk(i, k, nu))),
        pl.BlockSpec((1, tf, d), lambda i, k, te, nu: (te[tile(i, nu)], kk(i, k, nu), 0)),
    ]
    args = [x, w_gate, w_up, w_down]
    if residual is not None:
        in_specs.append(pl.BlockSpec((tm, d), lambda i, k, te, nu: (i, 0)))
        args.append(residual)
    return pl.pallas_call(
        functools.partial(_ffn_kernel, has_res=residual is not None),
        grid_spec=pltpu.PrefetchScalarGridSpec(
            num_scalar_prefetch=2,
            grid=(n_tiles, nk),
            in_specs=in_specs,
            out_specs=pl.BlockSpec((tm, d), lambda i, k, te, nu: (i, 0)),
            scratch_shapes=[pltpu.VMEM((tm, d), F32)],
        ),
        out_shape=jax.ShapeDtypeStruct((r, d), out_dtype),
        compiler_params=_cparams(("arbitrary", "arbitrary")),
        name="swiglu_ffn",
    )(tile_expert, n_used, *args)


def _router_kernel(l_ref, o_ref):
    lane = lax.broadcasted_iota(jnp.int32, l_ref.shape, 1)
    lf = lane.astype(F32)
    lg = jnp.where(lane < N_EXPERTS, l_ref[...], REMOVED)
    v1 = jnp.max(lg, axis=1, keepdims=True)
    i1 = jnp.min(jnp.where(lg == v1, lf, float(LANES)), axis=1, keepdims=True)
    lg2 = jnp.where(lf == i1, REMOVED, lg)
    v2 = jnp.max(lg2, axis=1, keepdims=True)
    i2 = jnp.min(jnp.where(lg2 == v2, lf, float(LANES)), axis=1, keepdims=True)
    e2 = jnp.exp(v2 - v1)
    den = 1.0 + e2
    p1 = 1.0 / den
    p2 = e2 / den
    out = jnp.where(lane == N_EXPERTS, i1, 0.0)
    out = jnp.where(lane == N_EXPERTS + 1, i2, out)
    out = jnp.where(lane == N_EXPERTS + 2, p1, out)
    out = jnp.where(lane == N_EXPERTS + 3, p2, out)
    o_ref[...] = out


def _router_top2(logits, tm=512):
    n = logits.shape[0]
    spec = pl.BlockSpec((tm, LANES), lambda i: (i, 0))
    return pl.pallas_call(
        _router_kernel,
        grid=(n // tm,),
        in_specs=[spec],
        out_specs=spec,
        out_shape=jax.ShapeDtypeStruct((n, LANES), F32),
        compiler_params=_cparams(("parallel",)),
        name="moe_router_top2",
    )(logits)


def _t5_bucket_np(dist):
    n = np.maximum(dist, 0)
    max_exact = REL_BUCKETS // 2
    nf = np.maximum(n, 1).astype(np.float32)
    large = max_exact + (np.log(nf / np.float32(max_exact)) / np.float32(math.log(REL_MAX_DIST / max_exact))
                         * np.float32(REL_BUCKETS - max_exact)).astype(np.int32)
    large = np.minimum(large, REL_BUCKETS - 1)
    return np.where(n < max_exact, n, large).astype(np.int32)


def _band_tables(rel_tab, t):
    q = np.arange(QB)[:, None]
    k = np.arange(QB)[None, :]
    toep = jnp.take(rel_tab.astype(F32), jnp.asarray(_t5_bucket_np((q - k) % QB)), axis=0)
    toep = toep.transpose(2, 0, 1)
    before = jnp.asarray(k > q)
    swa = toep[:SWA_HEADS]
    band_swa = jnp.concatenate([jnp.where(before, swa, NEG_INF), jnp.where(before, NEG_INF, swa)], axis=-1)
    far = rel_tab[REL_BUCKETS - 1, SWA_HEADS:].astype(F32)
    nsa = toep[SWA_HEADS:] - far[:, None, None]
    band_nsa = jnp.concatenate([jnp.where(before, nsa, 0.0), jnp.where(before, NEG_INF, nsa)], axis=-1)
    ncp = t // NSA_CMP_STRIDE
    c = np.arange(ncp)[None, :]
    d = q - NSA_CMP_STRIDE * (c - 9) - (NSA_CMP_LEN - 1)
    inband = (d >= 0) & (d < REL_MAX_DIST) & (c < 16)
    vals = jnp.take(rel_tab[:, SWA_HEADS:].astype(F32), jnp.asarray(_t5_bucket_np(np.clip(d, 0, None))), axis=0)
    vals = vals.transpose(2, 0, 1) - far[:, None, None]
    dbase = jnp.where(jnp.asarray(inband), vals, 0.0)
    return band_swa, band_nsa, dbase


def _selection_matrices(t):
    ncp = t // NSA_CMP_STRIDE
    ns = t // NSA_SEL_LEN
    per = NSA_SEL_LEN // NSA_CMP_STRIDE
    ratio = NSA_CMP_LEN // NSA_CMP_STRIDE
    mmat = np.zeros((ncp, ns), np.float32)
    for n in range(ncp - 1):
        for j in range(ratio):
            mmat[n, (n + j) // per] += 1.0
    emat = (np.arange(t)[None, :] // NSA_SEL_LEN == np.arange(ns)[:, None]).astype(np.float32)
    return jnp.asarray(mmat, BF16), jnp.asarray(emat, BF16)


def _dup(w):
    d = w.shape[0]
    w = w.reshape(d, -1, 1, HEAD_DIM)
    return jnp.broadcast_to(w, (d, w.shape[1], 2, HEAD_DIM)).reshape(d, -1)


def _prep_in_weights(w_in_l):
    sizes = (FOX_W, FOX_W, FOX_W, FOX_HEADS, SWA_W, SWA_KV_W, SWA_KV_W,
             NSA_W, NSA_KV_W, NSA_KV_W, NSA_KV_W, NSA_KV_W, NSA_KV_W, NSA_KV_W, NSA_HEADS * NSA_N_BRANCH)
    splits = [int(s) for s in np.cumsum(sizes)[:-1]]
    (fq, fk, fv, ff, sq, sk, sv, nq, nkc, nvc, nks, nvs, nkw, nvw, ng) = jnp.split(w_in_l, splits, axis=-1)
    scale = HEAD_DIM ** -0.5
    w_proj = jnp.concatenate([fq * scale, fk, fv, sq * scale, nq * scale, _dup(sk), _dup(sv),
                              _dup(nks), _dup(nvs), _dup(nkw), _dup(nvw), nkc, nvc], axis=-1).astype(BF16)
    pad = jnp.zeros((w_in_l.shape[0], D_GATE - FOX_HEADS - NSA_HEADS * NSA_N_BRANCH), w_in_l.dtype)
    w_gate = jnp.concatenate([ff, ng, pad], axis=-1).astype(BF16)
    return w_proj, w_gate


def _compress_inputs(proj, b, t):
    g = NSA_KV_HEADS
    flats = []
    for c0 in (C_NKC, C_NVC):
        x = proj[:, c0:c0 + NSA_KV_W].reshape(b, t, g, HEAD_DIM).transpose(0, 2, 1, 3)
        width = NSA_CMP_LEN * HEAD_DIM
        even = x.reshape(b, g, t // NSA_CMP_LEN, width)
        odd = x[:, :, NSA_CMP_STRIDE:t - NSA_CMP_STRIDE].reshape(b, g, t // NSA_CMP_LEN - 1, width)
        odd = jnp.pad(odd, ((0, 0), (0, 0), (0, 1), (0, 0)))
        flats.append(jnp.stack([even, odd], axis=3).reshape(b * g * (t // NSA_CMP_STRIDE), width))
    return jnp.stack(flats, axis=0)


def _mixer(hn, b, t, w_in_l, forget_bias, sinks, cmp_pos, cmp_w1, cmp_b1, cmp_w2, cmp_b2, tables):
    band_swa, band_nsa, dbase, mmat, emat = tables
    w_proj, w_gate = _prep_in_weights(w_in_l)
    proj = _matmul([hn], [w_proj], BF16, name="in_proj")
    gates = _matmul([hn], [w_gate], F32, tn=D_GATE, name="gate_proj")

    c = _fox_cumsum(gates, forget_bias, b, t)
    c_t = c.reshape(b, t, D_GATE)[:, :, :FOX_HEADS].transpose(0, 2, 1)
    o_fox = _fox_attention(proj, c, c_t, b, t)

    o_swa = _band_attention(proj, band_swa, b, t, mode="swa", hg=SWA_GROUP,
                            c_q=C_SQ, c_k=C_SK, c_v=C_SV, sinks=sinks)

    flat = _compress_inputs(proj, b, t)
    pos = cmp_pos.reshape(2, 1, NSA_CMP_LEN * HEAD_DIM).astype(F32)
    w2d = jnp.concatenate([cmp_w2, cmp_w2], axis=-1).astype(BF16)
    b2d = jnp.concatenate([cmp_b2, cmp_b2], axis=-1).reshape(2, 1, LANES).astype(F32)
    kvc = _compress(flat, pos, cmp_w1.astype(BF16), cmp_b1.reshape(2, 1, NSA_CMP_HIDDEN).astype(F32), w2d, b2d)
    ncp = t // NSA_CMP_STRIDE
    kvc = kvc.reshape(2, b, NSA_KV_HEADS, ncp, LANES)
    o_cmp, sel = _cmp_attention(proj, kvc[0], kvc[1], dbase, mmat, b, t)
    o_slc = _band_attention(proj, band_nsa, b, t, mode="sel", hg=NSA_GROUP,
                            c_q=C_NQ, c_k=C_NKS, c_v=C_NVS, sel=sel, emat=emat)
    o_win = _band_attention(proj, band_nsa, b, t, mode="win", hg=NSA_GROUP,
                            c_q=C_NQ, c_k=C_NKW, c_v=C_NVW)
    o_nsa = _nsa_combine(o_cmp, o_slc, o_win, gates)
    return o_fox, o_swa, o_nsa


def _moe(hn, h, router, w_gate, w_up, w_down, tm=512):
    n, d = hn.shape
    w_r = jnp.zeros((d, LANES), BF16).at[:, :N_EXPERTS].set(router.astype(BF16))
    logits = _matmul([hn], [w_r], F32, tn=LANES, name="router_logits")
    top = _router_top2(logits)
    e_idx = top[:, N_EXPERTS:N_EXPERTS + TOP_K].astype(jnp.int32)
    probs = top[:, N_EXPERTS + TOP_K:N_EXPERTS + 2 * TOP_K]
    e_flat = e_idx.reshape(-1)
    onehot = (e_flat[:, None] == jnp.arange(N_EXPERTS)[None, :]).astype(jnp.int32)
    csum = jnp.cumsum(onehot, axis=0)
    counts = csum[-1]
    rank = jnp.take_along_axis(csum, e_flat[:, None], axis=1)[:, 0] - 1
    padded = ((counts + tm - 1) // tm) * tm
    ends = jnp.cumsum(padded)
    starts = ends - padded
    dest = starts[e_flat] + rank
    r_pad = n * TOP_K + N_EXPERTS * tm
    src_tok = jnp.zeros((r_pad,), jnp.int32).at[dest].set(jnp.arange(n * TOP_K, dtype=jnp.int32) // TOP_K)
    tile_start = jnp.arange(r_pad // tm, dtype=jnp.int32) * tm
    tile_expert = jnp.minimum(jnp.sum(tile_start[:, None] >= ends[None, :], axis=1), N_EXPERTS - 1).astype(jnp.int32)
    n_used = (ends[-1:] // tm).astype(jnp.int32)
    xs = jnp.take(hn, src_tok, axis=0)
    y = _ffn(xs, w_gate, w_up, w_down, tile_expert, n_used, BF16, tm=tm)
    picked = jnp.take(y, dest, axis=0).reshape(n, TOP_K, d).astype(F32)
    return h + probs[:, 0:1] * picked[:, 0] + probs[:, 1:2] * picked[:, 1]


def kernel(x, attn_norm, w_in, fox_forget_bias, swa_sinks, nsa_cmp_pos, nsa_cmp_w1, nsa_cmp_b1, nsa_cmp_w2,
           nsa_cmp_b2, w_out, rel_bias_table, ffn_norm, dense_w_gate, dense_w_up, dense_w_down, moe_router,
           moe_w_gate, moe_w_up, moe_w_down, final_norm):
    b, t, d = x.shape
    n = b * t
    depth = w_in.shape[0]
    tables = _band_tables(rel_bias_table, t) + _selection_matrices(t)
    h = x.reshape(n, d)
    for layer in range(depth):
        hn = _rmsnorm(h, attn_norm[layer], BF16)
        o_fox, o_swa, o_nsa = _mixer(hn, b, t, w_in[layer], fox_forget_bias[layer], swa_sinks[layer],
                                     nsa_cmp_pos[layer], nsa_cmp_w1[layer], nsa_cmp_b1[layer],
                                     nsa_cmp_w2[layer], nsa_cmp_b2[layer], tables)
        wo = w_out[layer].astype(BF16)
        h = _matmul([o_fox, o_swa, o_nsa], [wo[:FOX_W], wo[FOX_W:FOX_W + SWA_W], wo[FOX_W + SWA_W:]],
                    F32, residual=h, name="out_proj")
        hn = _rmsnorm(h, ffn_norm[layer], BF16)
        i = layer // 2
        if layer % 2 == 0:
            tm = 512
            zeros = jnp.zeros((n // tm,), jnp.int32)
            h = _ffn(hn, dense_w_gate[i][None].astype(BF16), dense_w_up[i][None].astype(BF16),
                     dense_w_down[i][None].astype(BF16), zeros, jnp.full((1,), n // tm, jnp.int32),
                     F32, residual=h, tm=tm)
        else:
            h = _moe(hn, h, moe_router[i], moe_w_gate[i].astype(BF16), moe_w_up[i].astype(BF16),
                     moe_w_down[i].astype(BF16))
    return _rmsnorm(h, final_norm, F32).reshape(b, t, d)
```

```python
import functools
import math

import numpy as np
import jax
import jax.numpy as jnp
from jax import lax
from jax.experimental import pallas as pl
from jax.experimental.pallas import tpu as pltpu

F32 = jnp.float32
BF16 = jnp.bfloat16

D_MODEL = 2048
HEAD_DIM = 64
FOX_HEADS = 8
FOX_W = FOX_HEADS * HEAD_DIM
SWA_HEADS = 8
SWA_KV_HEADS = 2
SWA_GROUP = SWA_HEADS // SWA_KV_HEADS
SWA_W = SWA_HEADS * HEAD_DIM
SWA_KV_W = SWA_KV_HEADS * HEAD_DIM
SWA_WINDOW = 128
NSA_HEADS = 16
NSA_KV_HEADS = 2
NSA_GROUP = NSA_HEADS // NSA_KV_HEADS
NSA_W = NSA_HEADS * HEAD_DIM
NSA_KV_W = NSA_KV_HEADS * HEAD_DIM
NSA_CMP_LEN = 32
NSA_CMP_STRIDE = 16
NSA_CMP_HIDDEN = 256
NSA_SEL_LEN = 64
NSA_SEL_TOPK = 16
NSA_WINDOW = 512
NSA_N_BRANCH = 3
FORCE_SCORE = 1e9
REL_BUCKETS = 32
REL_MAX_DIST = 128
D_FF = 5632
N_EXPERTS = 8
TOP_K = 2
RMS_EPS = 1e-6
NEG_INF = -1e30
M_INIT = -1e29
REMOVED = -3e38

LANES = 128
QB = 128
VMEM_LIMIT = 56 * 1024 * 1024

C_FQ, C_FK, C_FV = 0, 512, 1024
C_SQ = 1536
C_NQ = 2048
C_SK, C_NKS, C_NKW = 3072, 3328, 3584
C_V = 3840
C_NKC, C_NVC = 4224, 4352
D_PROJ = 4480
PROJ_TN = 640
G_NG = FOX_HEADS
D_GATE = 128


def _cparams(sem, vmem=VMEM_LIMIT):
    return pltpu.CompilerParams(dimension_semantics=sem, vmem_limit_bytes=vmem)


def _dot(a, b):
    return jnp.dot(a, b, preferred_element_type=F32)


def _dot_nt(a, b):
    return lax.dot_general(a, b, (((1,), (1,)), ((), ())), preferred_element_type=F32)


def _split3(x):
    hi = x.astype(BF16)
    r1 = x - hi.astype(F32)
    mid = r1.astype(BF16)
    lo = (r1 - mid.astype(F32)).astype(BF16)
    return hi, mid, lo


def _rmsnorm_kernel(x_ref, g_ref, o_ref):
    x = x_ref[...]
    y = x * lax.rsqrt(jnp.mean(x * x, axis=-1, keepdims=True) + RMS_EPS)
    o_ref[...] = (y * g_ref[...]).astype(o_ref.dtype)


def _rmsnorm(h, g, out_dtype, tm=512):
    n, d = h.shape
    return pl.pallas_call(
        _rmsnorm_kernel,
        grid=(n // tm,),
        in_specs=[pl.BlockSpec((tm, d), lambda i: (i, 0)),
                  pl.BlockSpec((1, d), lambda i: (0, 0))],
        out_specs=pl.BlockSpec((tm, d), lambda i: (i, 0)),
        out_shape=jax.ShapeDtypeStruct((n, d), out_dtype),
        compiler_params=_cparams(("parallel",)),
        name="rmsnorm",
    )(h, g.reshape(1, d).astype(F32))


def _mm_kernel(*refs, n_in, has_res):
    o_ref = refs[-1]
    acc = None
    for a in range(n_in):
        d = _dot(refs[a][...], refs[n_in + a][...])
        acc = d if acc is None else acc + d
    if has_res:
        acc = acc + refs[2 * n_in][...]
    o_ref[...] = acc.astype(o_ref.dtype)


def _matmul(xs, ws, out_dtype, residual=None, tm=1024, tn=256, name="matmul"):
    n = xs[0].shape[0]
    m = ws[0].shape[1]
    tm = min(tm, n)
    tn = min(tn, m)
    in_specs = [pl.BlockSpec((tm, x.shape[1]), lambda i, j: (i, 0)) for x in xs]
    in_specs += [pl.BlockSpec((w.shape[0], tn), lambda i, j: (0, j)) for w in ws]
    args = list(xs) + list(ws)
    if residual is not None:
        in_specs.append(pl.BlockSpec((tm, tn), lambda i, j: (i, j)))
        args.append(residual)
    return pl.pallas_call(
        functools.partial(_mm_kernel, n_in=len(xs), has_res=residual is not None),
        grid=(n // tm, m // tn),
        in_specs=in_specs,
        out_specs=pl.BlockSpec((tm, tn), lambda i, j: (i, j)),
        out_shape=jax.ShapeDtypeStruct((n, m), out_dtype),
        compiler_params=_cparams(("parallel", "arbitrary")),
        name=name,
    )(*args)


def _cumsum_kernel(g_ref, b_ref, o_ref, carry_ref, *, tc):
    @pl.when(pl.program_id(1) == 0)
    def _():
        carry_ref[...] = jnp.zeros_like(carry_ref)

    z = g_ref[...] + b_ref[...]
    log_f = jnp.minimum(z, 0.0) - jnp.log1p(jnp.exp(-jnp.abs(z)))
    row = lax.broadcasted_iota(jnp.int32, (tc, tc), 0)
    col = lax.broadcasted_iota(jnp.int32, (tc, tc), 1)
    tri = jnp.where(col <= row, 1.0, 0.0).astype(BF16)
    hi, mid, lo = _split3(log_f)
    c = _dot(tri, hi) + _dot(tri, mid) + _dot(tri, lo) + carry_ref[...]
    o_ref[...] = c
    carry_ref[...] = c[tc - 1:tc, :]


def _fox_cumsum(gates, forget_bias, b, t, tc=256):
    nt = t // tc
    bias = jnp.zeros((1, D_GATE), F32).at[0, :FOX_HEADS].set(forget_bias.astype(F32))
    return pl.pallas_call(
        functools.partial(_cumsum_kernel, tc=tc),
        grid=(b, nt),
        in_specs=[pl.BlockSpec((tc, D_GATE), lambda bi, ti: (bi * nt + ti, 0)),
                  pl.BlockSpec((1, D_GATE), lambda bi, ti: (0, 0))],
        out_specs=pl.BlockSpec((tc, D_GATE), lambda bi, ti: (bi * nt + ti, 0)),
        out_shape=jax.ShapeDtypeStruct((b * t, D_GATE), F32),
        scratch_shapes=[pltpu.VMEM((1, D_GATE), F32)],
        compiler_params=_cparams(("parallel", "arbitrary")),
        name="fox_cumsum",
    )(gates, bias)


def _fox_kernel(q_ref, k_ref, vt_ref, ck_ref, cq_ref, o_ref, m_sc, l_sc, acc_sc, *, tq):
    i = pl.program_id(1)
    j = pl.program_id(2)
    lane = lax.broadcasted_iota(jnp.int32, (1, LANES), 1)
    upper = lane >= HEAD_DIM

    @pl.when(j == 0)
    def _():
        m_sc[...] = jnp.full_like(m_sc, M_INIT)
        l_sc[...] = jnp.zeros_like(l_sc)
        acc_sc[...] = jnp.zeros_like(acc_sc)

    def step(diagonal):
        if diagonal:
            key = lax.broadcasted_iota(jnp.int32, (tq, tq), 0)
            qry = lax.broadcasted_iota(jnp.int32, (tq, tq), 1)
            causal = key <= qry
        for p in range(FOX_HEADS // 2):
            q2 = q_ref[:, p * LANES:(p + 1) * LANES]
            k2 = k_ref[:, p * LANES:(p + 1) * LANES]
            zero = jnp.zeros_like(q2)
            for a in range(2):
                h = 2 * p + a
                qa = jnp.where(upper, q2, zero) if a else jnp.where(upper, zero, q2)
                s = _dot_nt(k2, qa)
                s = s + (cq_ref[0, h:h + 1, :] - ck_ref[:, h:h + 1])
                if diagonal:
                    s = jnp.where(causal, s, NEG_INF)
                m_prev = m_sc[h]
                m_new = jnp.maximum(m_prev, jnp.max(s, axis=0, keepdims=True))
                alpha = jnp.exp(m_prev - m_new)
                pr = jnp.exp(s - m_new)
                l_sc[h] = alpha * l_sc[h] + jnp.sum(pr, axis=0, keepdims=True)
                vt = vt_ref[0, h * HEAD_DIM:(h + 1) * HEAD_DIM, :]
                acc_sc[h] = alpha * acc_sc[h] + _dot(vt, pr.astype(BF16))
                m_sc[h] = m_new

    @pl.when(j < i)
    def _():
        step(False)

    @pl.when(j == i)
    def _():
        step(True)
        for p in range(FOX_HEADS // 2):
            o0 = acc_sc[2 * p] * (1.0 / l_sc[2 * p])
            o1 = acc_sc[2 * p + 1] * (1.0 / l_sc[2 * p + 1])
            o_ref[:, p * LANES:(p + 1) * LANES] = jnp.concatenate([o0, o1], axis=0).T.astype(o_ref.dtype)


def _fox_attention(proj, v_t, c, c_t, b, t, tq=512):
    nt = t // tq
    return pl.pallas_call(
        functools.partial(_fox_kernel, tq=tq),
        grid=(b, nt, nt),
        in_specs=[
            pl.BlockSpec((tq, FOX_W), lambda bi, i, j: (bi * nt + i, C_FQ // FOX_W)),
            pl.BlockSpec((tq, FOX_W), lambda bi, i, j: (bi * nt + jnp.minimum(j, i), C_FK // FOX_W)),
            pl.BlockSpec((1, FOX_W, tq), lambda bi, i, j: (bi, 0, jnp.minimum(j, i))),
            pl.BlockSpec((tq, D_GATE), lambda bi, i, j: (bi * nt + jnp.minimum(j, i), 0)),
            pl.BlockSpec((1, FOX_HEADS, tq), lambda bi, i, j: (bi, 0, i)),
        ],
        out_specs=pl.BlockSpec((tq, FOX_W), lambda bi, i, j: (bi * nt + i, 0)),
        out_shape=jax.ShapeDtypeStruct((b * t, FOX_W), BF16),
        scratch_shapes=[pltpu.VMEM((FOX_HEADS, 1, tq), F32),
                        pltpu.VMEM((FOX_HEADS, 1, tq), F32),
                        pltpu.VMEM((FOX_HEADS, HEAD_DIM, tq), F32)],
        compiler_params=_cparams(("parallel", "parallel", "arbitrary")),
        name="fox_attention",
    )(proj, proj, v_t, c, c_t)


def _stack_heads(q_ref, hg):
    lane = lax.broadcasted_iota(jnp.int32, (1, LANES), 1)
    upper = lane >= HEAD_DIM
    qs = []
    for p in range(hg // 2):
        q2 = q_ref[:, p * LANES:(p + 1) * LANES]
        qs.append(jnp.where(upper, jnp.zeros_like(q2), q2))
        qs.append(jnp.where(upper, q2, jnp.zeros_like(q2)))
    return jnp.concatenate(qs, axis=0)


def _store_heads(o_ref, o_t, hg):
    for p in range(hg // 2):
        pair = jnp.concatenate([o_t[:, (2 * p) * QB:(2 * p + 1) * QB],
                                o_t[:, (2 * p + 1) * QB:(2 * p + 2) * QB]], axis=0)
        o_ref[:, p * LANES:(p + 1) * LANES] = pair.T.astype(o_ref.dtype)


def _band_kernel(*refs, mode, hg, chunk):
    if mode == "swa":
        q_ref, k_ref, vt_ref, band_ref, sink_ref, o_ref = refs
    elif mode == "sel":
        q_ref, k_ref, vt_ref, band_ref, sel_ref, et_ref, o_ref = refs
    else:
        q_ref, k_ref, vt_ref, band_ref, o_ref = refs
    g = pl.program_id(1)
    i = pl.program_id(2)
    cols = hg * QB
    qstack = _stack_heads(q_ref, hg)

    def k_block(blk):
        return k_ref[pl.ds(pl.multiple_of(blk * QB, QB), QB), :]

    def vt_block(blk):
        return vt_ref[0, :, pl.ds(pl.multiple_of(blk * QB, QB), QB)]

    ip = jnp.maximum(i - 1, 0)
    k_near = jnp.concatenate([k_block(ip), k_block(i)], axis=0)
    vt_near = jnp.concatenate([vt_block(ip), vt_block(i)], axis=1)
    s = _dot_nt(k_near, qstack) + band_ref[0]
    key = lax.broadcasted_iota(jnp.int32, (2 * QB, 1), 0)
    s = jnp.where(jnp.logical_or(key >= QB, i > 0), s, NEG_INF)

    if mode == "swa":
        head = lax.broadcasted_iota(jnp.int32, (1, cols), 1) // QB
        sink = jnp.zeros((1, cols), F32)
        for h in range(hg):
            sink = jnp.where(head == h, sink_ref[g * hg + h], sink)
        m = jnp.maximum(jnp.max(s, axis=0, keepdims=True), sink)
        e = jnp.exp(s - m)
        l = jnp.sum(e, axis=0, keepdims=True) + jnp.exp(sink - m)
        o_t = _dot(vt_near, e.astype(BF16)) * (1.0 / l)

    elif mode == "win":
        n_far = NSA_WINDOW // QB - 1
        backs = range(n_far + 1, 1, -1)
        k_far = jnp.concatenate([k_block(jnp.maximum(i - bk, 0)) for bk in backs], axis=0)
        vt_far = jnp.concatenate([vt_block(jnp.maximum(i - bk, 0)) for bk in backs], axis=1)
        s_far = _dot_nt(k_far, qstack)
        kk = lax.broadcasted_iota(jnp.int32, (n_far * QB, cols), 0)
        qq = lax.broadcasted_iota(jnp.int32, (n_far * QB, cols), 1) % QB
        ok = jnp.logical_and(jnp.logical_or(kk >= QB, kk > qq), kk >= (n_far + 1 - i) * QB)
        s_far = jnp.where(ok, s_far, NEG_INF)
        m = jnp.maximum(jnp.max(s, axis=0, keepdims=True), jnp.max(s_far, axis=0, keepdims=True))
        e_n = jnp.exp(s - m)
        e_f = jnp.exp(s_far - m)
        l = jnp.sum(e_n, axis=0, keepdims=True) + jnp.sum(e_f, axis=0, keepdims=True)
        o_t = (_dot(vt_near, e_n.astype(BF16)) + _dot(vt_far, e_f.astype(BF16))) * (1.0 / l)

    else:
        sel_t = sel_ref[0, 0]
        ns = sel_t.shape[0]
        m_prev = _dot(et_ref[pl.ds(pl.multiple_of(ip * QB, QB), QB), :], sel_t)
        near_mask = jnp.concatenate([m_prev, jnp.ones((QB, QB), F32)], axis=0)
        s = jnp.where(jnp.concatenate([near_mask] * hg, axis=1) > 0.5, s, NEG_INF)
        sb = lax.broadcasted_iota(jnp.int32, (ns, 1), 0)
        sel_far = jnp.where(sb < 2 * (i - 1), sel_t, jnp.zeros_like(sel_t))
        per = chunk // QB
        n_chunks = (jnp.maximum(i - 1, 0) + per - 1) // per

        def body(c, carry):
            m, l, acc = carry
            off = pl.multiple_of(c * chunk, chunk)
            mk = _dot(et_ref[pl.ds(off, chunk), :], sel_far)
            sf = _dot_nt(k_ref[pl.ds(off, chunk), :], qstack)
            sf = jnp.where(jnp.concatenate([mk] * hg, axis=1) > 0.5, sf, NEG_INF)
            m_new = jnp.maximum(m, jnp.max(sf, axis=0, keepdims=True))
            alpha = jnp.exp(m - m_new)
            pf = jnp.exp(sf - m_new)
            l = alpha * l + jnp.sum(pf, axis=0, keepdims=True)
            acc = alpha * acc + _dot(vt_ref[0, :, pl.ds(off, chunk)], pf.astype(BF16))
            return m_new, l, acc

        m0 = jnp.full((1, cols), M_INIT, F32)
        l0 = jnp.zeros((1, cols), F32)
        a0 = jnp.zeros((HEAD_DIM, cols), F32)
        m, l, acc = lax.fori_loop(0, n_chunks, body, (m0, l0, a0))
        m_new = jnp.maximum(m, jnp.max(s, axis=0, keepdims=True))
        alpha = jnp.exp(m - m_new)
        e = jnp.exp(s - m_new)
        l = alpha * l + jnp.sum(e, axis=0, keepdims=True)
        o_t = (alpha * acc + _dot(vt_near, e.astype(BF16))) * (1.0 / l)

    _store_heads(o_ref, o_t, hg)


def _band_attention(proj, v_t, band, b, t, *, mode, hg, c_q, c_k, v_blk, sinks=None, sel=None, emat_t=None):
    nb = t // QB
    n_groups = 2
    qw = hg * HEAD_DIM
    chunk = min(512, t)
    in_specs = [
        pl.BlockSpec((QB, qw), lambda bi, g, i: (bi * nb + i, c_q // qw + g)),
        pl.BlockSpec((t, LANES), lambda bi, g, i: (bi, c_k // LANES + g)),
        pl.BlockSpec((1, HEAD_DIM, t), lambda bi, g, i: (bi, v_blk + g, 0)),
        pl.BlockSpec((1, 2 * QB, hg * QB), lambda bi, g, i: (g, 0, 0)),
    ]
    args = [proj, proj, v_t, band]
    if mode == "swa":
        in_specs.append(pl.BlockSpec(memory_space=pltpu.SMEM))
        args.append(sinks.astype(F32))
    if mode == "sel":
        ns = t // NSA_SEL_LEN
        in_specs.append(pl.BlockSpec((1, 1, ns, QB), lambda bi, g, i: (bi, g, 0, i)))
        in_specs.append(pl.BlockSpec((t, ns), lambda bi, g, i: (0, 0)))
        args += [sel, emat_t]
    return pl.pallas_call(
        functools.partial(_band_kernel, mode=mode, hg=hg, chunk=chunk),
        grid=(b, n_groups, nb),
        in_specs=in_specs,
        out_specs=pl.BlockSpec((QB, qw), lambda bi, g, i: (bi * nb + i, g)),
        out_shape=jax.ShapeDtypeStruct((b * t, n_groups * qw), BF16),
        compiler_params=_cparams(("parallel", "parallel", "arbitrary")),
        name="band_" + mode,
    )(*args)


def _compress_kernel(x_ref, pos_ref, w1_ref, b1_ref, w2_ref, b2_ref, o_ref):
    x = (x_ref[0].astype(F32) + pos_ref[0]).astype(BF16)
    hid = jax.nn.gelu(_dot(x, w1_ref[0]) + b1_ref[0])
    o_ref[0] = (_dot(hid.astype(BF16), w2_ref[0]) + b2_ref[0]).astype(o_ref.dtype)


def _compress(flat, pos, w1, b1, w2d, b2d, tr=256):
    _, r, cin = flat.shape
    return pl.pallas_call(
        _compress_kernel,
        grid=(2, r // tr),
        in_specs=[pl.BlockSpec((1, tr, cin), lambda s, i: (s, i, 0)),
                  pl.BlockSpec((1, 1, cin), lambda s, i: (s, 0, 0)),
                  pl.BlockSpec((1, cin, NSA_CMP_HIDDEN), lambda s, i: (s, 0, 0)),
                  pl.BlockSpec((1, 1, NSA_CMP_HIDDEN), lambda s, i: (s, 0, 0)),
                  pl.BlockSpec((1, NSA_CMP_HIDDEN, LANES), lambda s, i: (s, 0, 0)),
                  pl.BlockSpec((1, 1, LANES), lambda s, i: (s, 0, 0))],
        out_specs=pl.BlockSpec((1, tr, LANES), lambda s, i: (s, i, 0)),
        out_shape=jax.ShapeDtypeStruct((2, r, LANES), BF16),
        compiler_params=_cparams(("parallel", "parallel")),
        name="nsa_compress",
    )(flat, pos, w1, b1, w2d, b2d)


def _cmp_kernel(q_ref, kc_ref, vct_ref, ztab_ref, mmat_ref, o_ref, sel_ref, *, hg, ncp, ns, n_sel):
    i = pl.program_id(2)
    cols = hg * QB
    qstack = _stack_heads(q_ref, hg)
    s = _dot_nt(kc_ref[0, 0], qstack)
    start = pl.multiple_of(ncp - 8 * i, 8)
    delta = jnp.concatenate([ztab_ref[h, pl.ds(start, ncp), :] for h in range(hg)], axis=1)
    nn = lax.broadcasted_iota(jnp.int32, (ncp, QB), 0)
    qq = lax.broadcasted_iota(jnp.int32, (ncp, QB), 1)
    valid1 = (i * QB + qq - NSA_CMP_STRIDE * nn - (NSA_CMP_LEN - 1)) >= 0
    valid = jnp.concatenate([valid1] * hg, axis=1)
    s = jnp.where(valid, s + delta, NEG_INF)
    m = jnp.max(s, axis=0, keepdims=True)
    e = jnp.exp(s - m)
    p = jnp.where(valid, e * (1.0 / jnp.sum(e, axis=0, keepdims=True)), 0.0)
    _store_heads(o_ref, _dot(vct_ref[0, 0], p.astype(BF16)), hg)

    imp = p[:, 0:QB]
    for h in range(1, hg):
        imp = imp + p[:, h * QB:(h + 1) * QB]
    hi, mid, lo = _split3(imp)
    mm = mmat_ref[...]
    imp_sel = _dot(mm, hi) + _dot(mm, mid) + _dot(mm, lo)
    sb = lax.broadcasted_iota(jnp.int32, (ns, QB), 0)
    tb = (i * QB + lax.broadcasted_iota(jnp.int32, (ns, QB), 1)) // NSA_SEL_LEN
    forced = jnp.logical_or(jnp.logical_or(sb == 0, sb == tb), sb == tb - 1)
    score = jnp.where(forced, FORCE_SCORE, jnp.where(sb <= tb, imp_sel, NEG_INF))
    sub = 8
    rows = [score[v * sub:(v + 1) * sub, :] for v in range(ns // sub)]
    ranks = [jnp.zeros((sub, QB), F32) for _ in rows]
    sub_idx = lax.broadcasted_iota(jnp.int32, (sub, QB), 0)
    for r in range(ns):
        other = score[r:r + 1, :]
        for v, mine in enumerate(rows):
            if v < r // sub:
                inc = jnp.where(other > mine, 1.0, 0.0)
            elif v > r // sub:
                inc = jnp.where(other >= mine, 1.0, 0.0)
            else:
                tie = jnp.where(sub_idx > r % sub, 1.0, 0.0)
                inc = jnp.where(other > mine, 1.0, jnp.where(other == mine, tie, 0.0))
            ranks[v] = ranks[v] + inc
    rank = jnp.concatenate(ranks, axis=0)
    sel_ref[0, 0] = jnp.where(rank < n_sel, 1.0, 0.0).astype(sel_ref.dtype)


def _cmp_attention(proj, kc, vc_t, ztab, mmat_t, b, t):
    nb = t // QB
    hg = NSA_GROUP
    qw = hg * HEAD_DIM
    ncp = t // NSA_CMP_STRIDE
    ns = t // NSA_SEL_LEN
    n_sel = min(NSA_SEL_TOPK, ns)
    return pl.pallas_call(
        functools.partial(_cmp_kernel, hg=hg, ncp=ncp, ns=ns, n_sel=n_sel),
        grid=(b, NSA_KV_HEADS, nb),
        in_specs=[
            pl.BlockSpec((QB, qw), lambda bi, g, i: (bi * nb + i, C_NQ // qw + g)),
            pl.BlockSpec((1, 1, ncp, LANES), lambda bi, g, i: (bi, g, 0, 0)),
            pl.BlockSpec((1, 1, HEAD_DIM, ncp), lambda bi, g, i: (bi, g, 0, 0)),
            pl.BlockSpec((hg, 2 * ncp, QB), lambda bi, g, i: (g, 0, 0)),
            pl.BlockSpec((ns, ncp), lambda bi, g, i: (0, 0)),
        ],
        out_specs=[pl.BlockSpec((QB, qw), lambda bi, g, i: (bi * nb + i, g)),
                   pl.BlockSpec((1, 1, ns, QB), lambda bi, g, i: (bi, g, 0, i))],
        out_shape=[jax.ShapeDtypeStruct((b * t, NSA_W), BF16),
                   jax.ShapeDtypeStruct((b, NSA_KV_HEADS, ns, t), BF16)],
        compiler_params=_cparams(("parallel", "parallel", "arbitrary")),
        name="nsa_cmp_select",
    )(proj, kc, vc_t, ztab, mmat_t)


def _combine_kernel(oc_ref, os_ref, ow_ref, g_ref, o_ref):
    sg = jax.nn.sigmoid(g_ref[...])
    lane = lax.broadcasted_iota(jnp.int32, (1, LANES), 1)
    upper = lane >= HEAD_DIM
    for p in range(NSA_HEADS // 2):
        acc = None
        for br, ref in enumerate((oc_ref, os_ref, ow_ref)):
            c0 = G_NG + NSA_N_BRANCH * (2 * p) + br
            c1 = G_NG + NSA_N_BRANCH * (2 * p + 1) + br
            gate = jnp.where(upper, sg[:, c1:c1 + 1], sg[:, c0:c0 + 1])
            term = gate * ref[:, p * LANES:(p + 1) * LANES].astype(F32)
            acc = term if acc is None else acc + term
        o_ref[:, p * LANES:(p + 1) * LANES] = acc.astype(o_ref.dtype)


def _nsa_combine(o_cmp, o_slc, o_win, gates, tm=512):
    n = o_cmp.shape[0]
    spec = pl.BlockSpec((tm, NSA_W), lambda i: (i, 0))
    return pl.pallas_call(
        _combine_kernel,
        grid=(n // tm,),
        in_specs=[spec, spec, spec, pl.BlockSpec((tm, D_GATE), lambda i: (i, 0))],
        out_specs=spec,
        out_shape=jax.ShapeDtypeStruct((n, NSA_W), BF16),
        compiler_params=_cparams(("parallel",)),
        name="nsa_combine",
    )(o_cmp, o_slc, o_win, gates)


def _ffn_kernel(te_ref, nu_ref, *refs, has_res):
    if has_res:
        x_ref, wg_ref, wu_ref, wd_ref, res_ref, o_ref, acc_ref = refs
    else:
        x_ref, wg_ref, wu_ref, wd_ref, o_ref, acc_ref = refs
    del te_ref
    i = pl.program_id(0)
    k = pl.program_id(1)
    last = pl.num_programs(1) - 1
    used = i < nu_ref[0]

    @pl.when(jnp.logical_and(used, k == 0))
    def _():
        acc_ref[...] = jnp.zeros_like(acc_ref)

    @pl.when(used)
    def _():
        x = x_ref[...]
        gate = _dot(x, wg_ref[0])
        up = _dot(x, wu_ref[0])
        hid = (jax.nn.silu(gate) * up).astype(BF16)
        acc_ref[...] += _dot(hid, wd_ref[0])

    @pl.when(jnp.logical_and(used, k == last))
    def _():
        y = acc_ref[...]
        if has_res:
            y = y + res_ref[...]
        o_ref[...] = y.astype(o_ref.dtype)

    @pl.when(jnp.logical_and(jnp.logical_not(used), k == last))
    def _():
        o_ref[...] = jnp.zeros_like(o_ref)


def _ffn(x, w_gate, w_up, w_down, tile_expert, n_used, out_dtype, residual=None, tm=512, tf=512):
    r, d = x.shape
    nk = D_FF // tf
    n_tiles = r // tm

    def tile(i, nu):
        return jnp.minimum(i, nu[0] - 1)

    def kk(i, k, nu):
        return jnp.where(i < nu[0], k, nk - 1)

    in_specs = [
        pl.BlockSpec((tm, d), lambda i, k, te, nu: (tile(i, nu), 0)),
        pl.BlockSpec((1, d, tf), lambda i, k, te, nu: (te[tile(i, nu)], 0, kk(i, k, nu))),
        pl.BlockSpec((1, d, tf), lambda i, k, te, nu: (te[tile(i, nu)], 0, kk(i, k, nu))),
        pl.BlockSpec((1, tf, d), lambda i, k, te, nu: (te[tile(i, nu)], kk(i, k, nu), 0)),
    ]
    args = [x, w_gate, w_up, w_down]
    if residual is not None:
        in_specs.append(pl.BlockSpec((tm, d), lambda i, k, te, nu: (i, 0)))
        args.append(residual)
    return pl.pallas_call(
        functools.partial(_ffn_kernel, has_res=residual is not None),
        grid_spec=pltpu.PrefetchScalarGridSpec(
            num_scalar_prefetch=2,
            grid=(n_tiles, nk),
            in_specs=in_specs,
            out_specs=pl.BlockSpec((tm, d), lambda i, k, te, nu: (i, 0)),
            scratch_shapes=[pltpu.VMEM((tm, d), F32)],
        ),
        out_shape=jax.ShapeDtypeStruct((r, d), out_dtype),
        compiler_params=_cparams(("arbitrary", "arbitrary")),
        name="swiglu_ffn",
    )(tile_expert, n_used, *args)


def _router_kernel(l_ref, o_ref):
    lane = lax.broadcasted_iota(jnp.int32, l_ref.shape, 1)
    lf = lane.astype(F32)
    lg = jnp.where(lane < N_EXPERTS, l_ref[...], REMOVED)
    v1 = jnp.max(lg, axis=1, keepdims=True)
    i1 = jnp.min(jnp.where(lg == v1, lf, float(LANES)), axis=1, keepdims=True)
    lg2 = jnp.where(lf == i1, REMOVED, lg)
    v2 = jnp.max(lg2, axis=1, keepdims=True)
    i2 = jnp.min(jnp.where(lg2 == v2, lf, float(LANES)), axis=1, keepdims=True)
    e2 = jnp.exp(v2 - v1)
    den = 1.0 + e2
    p1 = 1.0 / den
    p2 = e2 / den
    out = jnp.where(lane == N_EXPERTS, i1, 0.0)
    out = jnp.where(lane == N_EXPERTS + 1, i2, out)
    out = jnp.where(lane == N_EXPERTS + 2, p1, out)
    out = jnp.where(lane == N_EXPERTS + 3, p2, out)
    o_ref[...] = out


def _router_top2(logits, tm=512):
    n = logits.shape[0]
    spec = pl.BlockSpec((tm, LANES), lambda i: (i, 0))
    return pl.pallas_call(
        _router_kernel,
        grid=(n // tm,),
        in_specs=[spec],
        out_specs=spec,
        out_shape=jax.ShapeDtypeStruct((n, LANES), F32),
        compiler_params=_cparams(("parallel",)),
        name="moe_router_top2",
    )(logits)


def _t5_bucket_np(dist):
    n = np.maximum(dist, 0)
    max_exact = REL_BUCKETS // 2
    nf = np.maximum(n, 1).astype(np.float32)
    large = max_exact + (np.log(nf / np.float32(max_exact)) / np.float32(math.log(REL_MAX_DIST / max_exact))
                         * np.float32(REL_BUCKETS - max_exact)).astype(np.int32)
    large = np.minimum(large, REL_BUCKETS - 1)
    return np.where(n < max_exact, n, large).astype(np.int32)


def _band_tables(rel_tab, t):
    k = np.arange(QB)[:, None]
    q = np.arange(QB)[None, :]
    toep = jnp.take(rel_tab.astype(F32), jnp.asarray(_t5_bucket_np((q - k) % QB)), axis=0)
    toep = toep.transpose(2, 0, 1)
    before = jnp.asarray(k > q)
    far = rel_tab[REL_BUCKETS - 1, SWA_HEADS:].astype(F32)

    def tiles(tab, fill_prev):
        pair = jnp.concatenate([jnp.where(before, tab, fill_prev), jnp.where(before, NEG_INF, tab)], axis=1)
        heads, group = tab.shape[0], tab.shape[0] // 2
        return pair.reshape(2, group, 2 * QB, QB).transpose(0, 2, 1, 3).reshape(2, 2 * QB, group * QB)

    band_swa = tiles(toep[:SWA_HEADS], NEG_INF)
    band_nsa = tiles(toep[SWA_HEADS:] - far[:, None, None], 0.0)
    ncp = t // NSA_CMP_STRIDE
    m = np.arange(2 * ncp)[:, None] - ncp
    d = q - NSA_CMP_STRIDE * m - (NSA_CMP_LEN - 1)
    inband = (d >= 0) & (d < REL_MAX_DIST) & (m >= -9) & (m <= 6)
    vals = jnp.take(rel_tab[:, SWA_HEADS:].astype(F32), jnp.asarray(_t5_bucket_np(np.clip(d, 0, None))), axis=0)
    vals = vals.transpose(2, 0, 1) - far[:, None, None]
    ztab = jnp.where(jnp.asarray(inband), vals, 0.0)
    return band_swa, band_nsa, ztab


def _selection_matrices(t):
    ncp = t // NSA_CMP_STRIDE
    ns = t // NSA_SEL_LEN
    per = NSA_SEL_LEN // NSA_CMP_STRIDE
    ratio = NSA_CMP_LEN // NSA_CMP_STRIDE
    mmat_t = np.zeros((ns, ncp), np.float32)
    for n in range(ncp - 1):
        for j in range(ratio):
            mmat_t[(n + j) // per, n] += 1.0
    emat_t = (np.arange(t)[:, None] // NSA_SEL_LEN == np.arange(ns)[None, :]).astype(np.float32)
    return jnp.asarray(mmat_t, BF16), jnp.asarray(emat_t, BF16)


def _dup(w):
    d = w.shape[0]
    w = w.reshape(d, -1, 1, HEAD_DIM)
    return jnp.broadcast_to(w, (d, w.shape[1], 2, HEAD_DIM)).reshape(d, -1)


def _prep_in_weights(w_in_l):
    sizes = (FOX_W, FOX_W, FOX_W, FOX_HEADS, SWA_W, SWA_KV_W, SWA_KV_W,
             NSA_W, NSA_KV_W, NSA_KV_W, NSA_KV_W, NSA_KV_W, NSA_KV_W, NSA_KV_W, NSA_HEADS * NSA_N_BRANCH)
    splits = [int(s) for s in np.cumsum(sizes)[:-1]]
    (fq, fk, fv, ff, sq, sk, sv, nq, nkc, nvc, nks, nvs, nkw, nvw, ng) = jnp.split(w_in_l, splits, axis=-1)
    scale = HEAD_DIM ** -0.5
    w_proj = jnp.concatenate([fq * scale, fk, fv, sq * scale, nq * scale, _dup(sk), _dup(nks), _dup(nkw),
                              sv, nvs, nvw, nkc, nvc], axis=-1).astype(BF16)
    pad = jnp.zeros((w_in_l.shape[0], D_GATE - FOX_HEADS - NSA_HEADS * NSA_N_BRANCH), w_in_l.dtype)
    w_gate = jnp.concatenate([ff, ng, pad], axis=-1).astype(BF16)
    return w_proj, w_gate


def _compress_inputs(proj, b, t):
    g = NSA_KV_HEADS
    flats = []
    for c0 in (C_NKC, C_NVC):
        x = proj[:, c0:c0 + NSA_KV_W].reshape(b, t, g, HEAD_DIM).transpose(0, 2, 1, 3)
        width = NSA_CMP_LEN * HEAD_DIM
        even = x.reshape(b, g, t // NSA_CMP_LEN, width)
        odd = x[:, :, NSA_CMP_STRIDE:t - NSA_CMP_STRIDE].reshape(b, g, t // NSA_CMP_LEN - 1, width)
        odd = jnp.pad(odd, ((0, 0), (0, 0), (0, 1), (0, 0)))
        flats.append(jnp.stack([even, odd], axis=3).reshape(b * g * (t // NSA_CMP_STRIDE), width))
    return jnp.stack(flats, axis=0)


def _mixer(hn, b, t, w_in_l, forget_bias, sinks, cmp_pos, cmp_w1, cmp_b1, cmp_w2, cmp_b2, tables):
    band_swa, band_nsa, ztab, mmat_t, emat_t = tables
    w_proj, w_gate = _prep_in_weights(w_in_l)
    proj = _matmul([hn], [w_proj], BF16, tn=PROJ_TN, name="in_proj")
    gates = _matmul([hn], [w_gate], F32, tn=D_GATE, name="gate_proj")
    fv_t = proj[:, C_FV:C_FV + FOX_W].reshape(b, t, FOX_W).transpose(0, 2, 1)
    v_t = proj[:, C_V:C_V + 3 * NSA_KV_W].reshape(b, t, 3 * NSA_KV_W).transpose(0, 2, 1)

    c = _fox_cumsum(gates, forget_bias, b, t)
    c_t = c.reshape(b, t, D_GATE)[:, :, :FOX_HEADS].transpose(0, 2, 1)
    o_fox = _fox_attention(proj, fv_t, c, c_t, b, t)

    o_swa = _band_attention(proj, v_t, band_swa, b, t, mode="swa", hg=SWA_GROUP,
                            c_q=C_SQ, c_k=C_SK, v_blk=0, sinks=sinks)

    flat = _compress_inputs(proj, b, t)
    pos = cmp_pos.reshape(2, 1, NSA_CMP_LEN * HEAD_DIM).astype(F32)
    w2d = jnp.concatenate([cmp_w2, cmp_w2], axis=-1).astype(BF16)
    b2d = jnp.concatenate([cmp_b2, cmp_b2], axis=-1).reshape(2, 1, LANES).astype(F32)
    kvc = _compress(flat, pos, cmp_w1.astype(BF16), cmp_b1.reshape(2, 1, NSA_CMP_HIDDEN).astype(F32), w2d, b2d)
    ncp = t // NSA_CMP_STRIDE
    kvc = kvc.reshape(2, b, NSA_KV_HEADS, ncp, LANES)
    vc_t = kvc[1, :, :, :, :HEAD_DIM].transpose(0, 1, 3, 2)
    o_cmp, sel = _cmp_attention(proj, kvc[0], vc_t, ztab, mmat_t, b, t)
    o_slc = _band_attention(proj, v_t, band_nsa, b, t, mode="sel", hg=NSA_GROUP,
                            c_q=C_NQ, c_k=C_NKS, v_blk=2, sel=sel, emat_t=emat_t)
    o_win = _band_attention(proj, v_t, band_nsa, b, t, mode="win", hg=NSA_GROUP,
                            c_q=C_NQ, c_k=C_NKW, v_blk=4)
    o_nsa = _nsa_combine(o_cmp, o_slc, o_win, gates)
    return o_fox, o_swa, o_nsa


def _moe(hn, h, router, w_gate, w_up, w_down, tm=512):
    n, d = hn.shape
    w_r = jnp.zeros((d, LANES), BF16).at[:, :N_EXPERTS].set(router.astype(BF16))
    logits = _matmul([hn], [w_r], F32, tn=LANES, name="router_logits")
    top = _router_top2(logits)
    e_idx = top[:, N_EXPERTS:N_EXPERTS + TOP_K].astype(jnp.int32)
    probs = top[:, N_EXPERTS + TOP_K:N_EXPERTS + 2 * TOP_K]
    e_flat = e_idx.reshape(-1)
    onehot = (e_flat[:, None] == jnp.arange(N_EXPERTS)[None, :]).astype(jnp.int32)
    csum = jnp.cumsum(onehot, axis=0)
    counts = csum[-1]
    rank = jnp.take_along_axis(csum, e_flat[:, None], axis=1)[:, 0] - 1
    padded = ((counts + tm - 1) // tm) * tm
    ends = jnp.cumsum(padded)
    starts = ends - padded
    dest = starts[e_flat] + rank
    r_pad = n * TOP_K + N_EXPERTS * tm
    src_tok = jnp.zeros((r_pad,), jnp.int32).at[dest].set(jnp.arange(n * TOP_K, dtype=jnp.int32) // TOP_K)
    tile_start = jnp.arange(r_pad // tm, dtype=jnp.int32) * tm
    tile_expert = jnp.minimum(jnp.sum(tile_start[:, None] >= ends[None, :], axis=1), N_EXPERTS - 1).astype(jnp.int32)
    n_used = (ends[-1:] // tm).astype(jnp.int32)
    xs = jnp.take(hn, src_tok, axis=0)
    y = _ffn(xs, w_gate, w_up, w_down, tile_expert, n_used, BF16, tm=tm)
    picked = jnp.take(y, dest, axis=0).reshape(n, TOP_K, d).astype(F32)
    return h + probs[:, 0:1] * picked[:, 0] + probs[:, 1:2] * picked[:, 1]


def kernel(x, attn_norm, w_in, fox_forget_bias, swa_sinks, nsa_cmp_pos, nsa_cmp_w1, nsa_cmp_b1, nsa_cmp_w2,
           nsa_cmp_b2, w_out, rel_bias_table, ffn_norm, dense_w_gate, dense_w_up, dense_w_down, moe_router,
           moe_w_gate, moe_w_up, moe_w_down, final_norm):
    b, t, d = x.shape
    n = b * t
    depth = w_in.shape[0]
    tables = _band_tables(rel_bias_table, t) + _selection_matrices(t)
    h = x.reshape(n, d)
    for layer in range(depth):
        hn = _rmsnorm(h, attn_norm[layer], BF16)
        o_fox, o_swa, o_nsa = _mixer(hn, b, t, w_in[layer], fox_forget_bias[layer], swa_sinks[layer],
                                     nsa_cmp_pos[layer], nsa_cmp_w1[layer], nsa_cmp_b1[layer],
                                     nsa_cmp_w2[layer], nsa_cmp_b2[layer], tables)
        wo = w_out[layer].astype(BF16)
        h = _matmul([o_fox, o_swa, o_nsa], [wo[:FOX_W], wo[FOX_W:FOX_W + SWA_W], wo[FOX_W + SWA_W:]],
                    F32, residual=h, name="out_proj")
        hn = _rmsnorm(h, ffn_norm[layer], BF16)
        i = layer // 2
        if layer % 2 == 0:
            tm = 512
            zeros = jnp.zeros((n // tm,), jnp.int32)
            h = _ffn(hn, dense_w_gate[i][None].astype(BF16), dense_w_up[i][None].astype(BF16),
                     dense_w_down[i][None].astype(BF16), zeros, jnp.full((1,), n // tm, jnp.int32),
                     F32, residual=h, tm=tm)
        else:
            h = _moe(hn, h, moe_router[i], moe_w_gate[i].astype(BF16), moe_w_up[i].astype(BF16),
                     moe_w_down[i].astype(BF16))
    return _rmsnorm(h, final_norm, F32).reshape(b, t, d)
```

```python
import functools
import math

import numpy as np
import jax
import jax.numpy as jnp
from jax import lax
from jax.experimental import pallas as pl
from jax.experimental.pallas import tpu as pltpu

F32 = jnp.float32
BF16 = jnp.bfloat16

D_MODEL = 2048
HEAD_DIM = 64
FOX_HEADS = 8
FOX_W = FOX_HEADS * HEAD_DIM
SWA_HEADS = 8
SWA_KV_HEADS = 2
SWA_GROUP = SWA_HEADS // SWA_KV_HEADS
SWA_W = SWA_HEADS * HEAD_DIM
SWA_KV_W = SWA_KV_HEADS * HEAD_DIM
SWA_WINDOW = 128
NSA_HEADS = 16
NSA_KV_HEADS = 2
NSA_GROUP = NSA_HEADS // NSA_KV_HEADS
NSA_W = NSA_HEADS * HEAD_DIM
NSA_KV_W = NSA_KV_HEADS * HEAD_DIM
NSA_CMP_LEN = 32
NSA_CMP_STRIDE = 16
NSA_CMP_HIDDEN = 256
NSA_SEL_LEN = 64
NSA_SEL_TOPK = 16
NSA_WINDOW = 512
NSA_N_BRANCH = 3
FORCE_SCORE = 1e9
REL_BUCKETS = 32
REL_MAX_DIST = 128
D_FF = 5632
N_EXPERTS = 8
TOP_K = 2
RMS_EPS = 1e-6
NEG_INF = -1e30
M_INIT = -1e29
REMOVED = -3e38
LOG2E = 1.4426950408889634
ONES_ROWS = 16

LANES = 128
QB = 128
VMEM_LIMIT = 56 * 1024 * 1024

C_FQ, C_FK, C_FV = 0, 512, 1024
C_SQ = 1536
C_NQ = 2048
C_SK, C_NKS, C_NKW = 3072, 3328, 3584
C_V = 3840
C_NKC, C_NVC = 4224, 4352
D_PROJ = 4480
PROJ_TN = 640
G_NG = FOX_HEADS
D_GATE = 128


def _cparams(sem, vmem=VMEM_LIMIT):
    return pltpu.CompilerParams(dimension_semantics=sem, vmem_limit_bytes=vmem)


def _dot(a, b):
    return jnp.dot(a, b, preferred_element_type=F32)


def _dot_nt(a, b):
    return lax.dot_general(a, b, (((1,), (1,)), ((), ())), preferred_element_type=F32)


def _split3(x):
    hi = x.astype(BF16)
    r1 = x - hi.astype(F32)
    mid = r1.astype(BF16)
    lo = (r1 - mid.astype(F32)).astype(BF16)
    return hi, mid, lo


def _pipeline3(n, stage1, stage2, stage3):
    first, second, outs = {}, {}, []
    for step in range(n + 2):
        if step < n:
            first[step] = stage1(step)
        if 0 <= step - 1 < n:
            second[step - 1] = stage2(step - 1, first.pop(step - 1))
        if 0 <= step - 2 < n:
            outs.append(stage3(step - 2, *second.pop(step - 2)))
    return outs


def _rmsnorm_kernel(x_ref, g_ref, o_ref):
    x = x_ref[...]
    y = x * lax.rsqrt(jnp.mean(x * x, axis=-1, keepdims=True) + RMS_EPS)
    o_ref[...] = (y * g_ref[...]).astype(o_ref.dtype)


def _rmsnorm(h, g, out_dtype, tm=512):
    n, d = h.shape
    return pl.pallas_call(
        _rmsnorm_kernel,
        grid=(n // tm,),
        in_specs=[pl.BlockSpec((tm, d), lambda i: (i, 0)),
                  pl.BlockSpec((1, d), lambda i: (0, 0))],
        out_specs=pl.BlockSpec((tm, d), lambda i: (i, 0)),
        out_shape=jax.ShapeDtypeStruct((n, d), out_dtype),
        compiler_params=_cparams(("parallel",)),
        name="rmsnorm",
    )(h, g.reshape(1, d).astype(F32))


def _mm_kernel(*refs, n_in, has_res):
    o_ref = refs[-1]
    acc = None
    for a in range(n_in):
        d = _dot(refs[a][...], refs[n_in + a][...])
        acc = d if acc is None else acc + d
    if has_res:
        acc = acc + refs[2 * n_in][...]
    o_ref[...] = acc.astype(o_ref.dtype)


def _matmul(xs, ws, out_dtype, residual=None, tm=1024, tn=256, name="matmul"):
    n = xs[0].shape[0]
    m = ws[0].shape[1]
    tm = min(tm, n)
    tn = min(tn, m)
    in_specs = [pl.BlockSpec((tm, x.shape[1]), lambda i, j: (i, 0)) for x in xs]
    in_specs += [pl.BlockSpec((w.shape[0], tn), lambda i, j: (0, j)) for w in ws]
    args = list(xs) + list(ws)
    if residual is not None:
        in_specs.append(pl.BlockSpec((tm, tn), lambda i, j: (i, j)))
        args.append(residual)
    return pl.pallas_call(
        functools.partial(_mm_kernel, n_in=len(xs), has_res=residual is not None),
        grid=(n // tm, m // tn),
        in_specs=in_specs,
        out_specs=pl.BlockSpec((tm, tn), lambda i, j: (i, j)),
        out_shape=jax.ShapeDtypeStruct((n, m), out_dtype),
        compiler_params=_cparams(("parallel", "arbitrary")),
        name=name,
    )(*args)


def _cumsum_kernel(g_ref, b_ref, pq_ref, pk_ref, oq_ref, ok_ref, qx_ref, kx_ref, carry_ref, *, tc):
    @pl.when(pl.program_id(1) == 0)
    def _():
        carry_ref[...] = jnp.zeros_like(carry_ref)

    z = g_ref[...] + b_ref[...]
    log_f = jnp.minimum(z, 0.0) - jnp.log1p(jnp.exp(-jnp.abs(z)))
    row = lax.broadcasted_iota(jnp.int32, (tc, tc), 0)
    col = lax.broadcasted_iota(jnp.int32, (tc, tc), 1)
    tri = jnp.where(col <= row, 1.0, 0.0).astype(BF16)
    hi, mid, lo = _split3(log_f)
    c = _dot(tri, hi) + _dot(tri, mid) + _dot(tri, lo) + carry_ref[...]
    carry_ref[...] = c[tc - 1:tc, :]
    parts = jnp.concatenate(_split3(c * LOG2E), axis=1)
    qx_ref[...] = (_dot(parts, pq_ref[...]) + oq_ref[...]).astype(qx_ref.dtype)
    kx_ref[...] = (_dot(parts, pk_ref[...]) + ok_ref[...]).astype(kx_ref.dtype)


def _fox_decay_operands(gates, forget_bias, b, t, tc=256):
    nt = t // tc
    wide = FOX_HEADS * LANES
    bias = jnp.zeros((1, D_GATE), F32).at[0, :FOX_HEADS].set(forget_bias.astype(F32))
    pq = np.zeros((3 * D_GATE, wide), np.float32)
    pk = np.zeros((3 * D_GATE, wide), np.float32)
    oq = np.zeros((1, wide), np.float32)
    ok = np.zeros((1, wide), np.float32)
    for h in range(FOX_HEADS):
        for part in range(3):
            pq[part * D_GATE + h, h * LANES + part] = 1.0
            pk[part * D_GATE + h, h * LANES + 3 + part] = -1.0
            oq[0, h * LANES + 3 + part] = 1.0
            ok[0, h * LANES + part] = 1.0
    const = lambda shape: pl.BlockSpec(shape, lambda bi, ti: (0, 0))
    out_spec = pl.BlockSpec((tc, wide), lambda bi, ti: (bi * nt + ti, 0))
    return pl.pallas_call(
        functools.partial(_cumsum_kernel, tc=tc),
        grid=(b, nt),
        in_specs=[pl.BlockSpec((tc, D_GATE), lambda bi, ti: (bi * nt + ti, 0)),
                  const((1, D_GATE)), const((3 * D_GATE, wide)), const((3 * D_GATE, wide)),
                  const((1, wide)), const((1, wide))],
        out_specs=[out_spec, out_spec],
        out_shape=[jax.ShapeDtypeStruct((b * t, wide), BF16), jax.ShapeDtypeStruct((b * t, wide), BF16)],
        scratch_shapes=[pltpu.VMEM((1, D_GATE), F32)],
        compiler_params=_cparams(("parallel", "arbitrary")),
        name="fox_cumsum",
    )(gates, bias, jnp.asarray(pq, BF16), jnp.asarray(pk, BF16), jnp.asarray(oq), jnp.asarray(ok))


def _fox_kernel(q_ref, k_ref, vt_ref, qx_ref, kx_ref, o_ref, m_sc, acc_sc, *, tq):
    i = pl.program_id(1)
    j = pl.program_id(2)
    lane = lax.broadcasted_iota(jnp.int32, (1, LANES), 1)
    upper = lane >= HEAD_DIM

    @pl.when(j == 0)
    def _():
        m_sc[...] = jnp.full_like(m_sc, M_INIT)
        acc_sc[...] = jnp.zeros_like(acc_sc)

    def step(diagonal):
        if diagonal:
            key = lax.broadcasted_iota(jnp.int32, (tq, tq), 0)
            qry = lax.broadcasted_iota(jnp.int32, (tq, tq), 1)
            causal = key <= qry
        ones = jnp.ones((ONES_ROWS, tq), BF16)

        def qk(h):
            p, a = divmod(h, 2)
            q2 = q_ref[:, p * LANES:(p + 1) * LANES]
            k2 = k_ref[:, p * LANES:(p + 1) * LANES]
            zero = jnp.zeros_like(q2)
            qa = jnp.where(upper, q2, zero) if a else jnp.where(upper, zero, q2)
            q_aug = jnp.concatenate([qa, qx_ref[:, h * LANES:(h + 1) * LANES]], axis=1)
            k_aug = jnp.concatenate([k2, kx_ref[:, h * LANES:(h + 1) * LANES]], axis=1)
            s = _dot_nt(k_aug, q_aug)
            return jnp.where(causal, s, NEG_INF) if diagonal else s

        def softmax(h, s):
            m_prev = m_sc[h]
            m_new = jnp.maximum(m_prev, jnp.max(s, axis=0, keepdims=True))
            m_sc[h] = m_new
            return jnp.exp2(m_prev - m_new), jnp.exp2(s - m_new).astype(BF16)

        def pv(h, alpha, pr):
            vt = jnp.concatenate([vt_ref[0, h * HEAD_DIM:(h + 1) * HEAD_DIM, :], ones], axis=0)
            acc_sc[h] = alpha * acc_sc[h] + _dot(vt, pr)

        _pipeline3(FOX_HEADS, qk, softmax, pv)

    @pl.when(j < i)
    def _():
        step(False)

    @pl.when(j == i)
    def _():
        step(True)
        for p in range(FOX_HEADS // 2):
            outs = []
            for h in (2 * p, 2 * p + 1):
                acc = acc_sc[h]
                outs.append(acc[:HEAD_DIM] * (1.0 / acc[HEAD_DIM:HEAD_DIM + 1]))
            o_ref[:, p * LANES:(p + 1) * LANES] = jnp.concatenate(outs, axis=0).T.astype(o_ref.dtype)


def _fox_attention(proj, v_t, qx, kx, b, t, tq=512):
    nt = t // tq
    wide = FOX_HEADS * LANES
    return pl.pallas_call(
        functools.partial(_fox_kernel, tq=tq),
        grid=(b, nt, nt),
        in_specs=[
            pl.BlockSpec((tq, FOX_W), lambda bi, i, j: (bi * nt + i, C_FQ // FOX_W)),
            pl.BlockSpec((tq, FOX_W), lambda bi, i, j: (bi * nt + jnp.minimum(j, i), C_FK // FOX_W)),
            pl.BlockSpec((1, FOX_W, tq), lambda bi, i, j: (bi, 0, jnp.minimum(j, i))),
            pl.BlockSpec((tq, wide), lambda bi, i, j: (bi * nt + i, 0)),
            pl.BlockSpec((tq, wide), lambda bi, i, j: (bi * nt + jnp.minimum(j, i), 0)),
        ],
        out_specs=pl.BlockSpec((tq, FOX_W), lambda bi, i, j: (bi * nt + i, 0)),
        out_shape=jax.ShapeDtypeStruct((b * t, FOX_W), BF16),
        scratch_shapes=[pltpu.VMEM((FOX_HEADS, 1, tq), F32),
                        pltpu.VMEM((FOX_HEADS, HEAD_DIM + ONES_ROWS, tq), F32)],
        compiler_params=_cparams(("parallel", "parallel", "arbitrary")),
        name="fox_attention",
    )(proj, proj, v_t, qx, kx)


def _stack_heads(q_ref, hg):
    lane = lax.broadcasted_iota(jnp.int32, (1, LANES), 1)
    upper = lane >= HEAD_DIM
    qs = []
    for p in range(hg // 2):
        q2 = q_ref[:, p * LANES:(p + 1) * LANES]
        qs.append(jnp.where(upper, jnp.zeros_like(q2), q2))
        qs.append(jnp.where(upper, q2, jnp.zeros_like(q2)))
    return jnp.concatenate(qs, axis=0)


def _store_heads(o_ref, o_t, hg):
    for p in range(hg // 2):
        pair = jnp.concatenate([o_t[:, (2 * p) * QB:(2 * p + 1) * QB],
                                o_t[:, (2 * p + 1) * QB:(2 * p + 2) * QB]], axis=0)
        o_ref[:, p * LANES:(p + 1) * LANES] = pair.T.astype(o_ref.dtype)


def _band_kernel(*refs, mode, hg, chunk):
    if mode == "swa":
        q_ref, k_ref, vt_ref, tab_ref, sink_ref, o_ref = refs
    elif mode == "sel":
        q_ref, k_ref, vt_ref, tab_ref, sel_ref, e_ref, et_ref, o_ref, sa_ref, sb_ref, m_ref, acc_ref = refs
    else:
        q_ref, k_ref, vt_ref, tab_ref, o_ref = refs
    g = pl.program_id(1)
    i = pl.program_id(2)
    cols = hg * QB
    qstack = _stack_heads(q_ref, hg)
    eye = jnp.where(lax.broadcasted_iota(jnp.int32, (QB, QB), 0) == lax.broadcasted_iota(jnp.int32, (QB, QB), 1),
                    1.0, 0.0).astype(BF16)

    def k_block(blk):
        return k_ref[pl.ds(pl.multiple_of(blk * QB, QB), QB), :]

    def vt_rows(start, size):
        return jnp.concatenate([vt_ref[0, :, pl.ds(start, size)], jnp.ones((ONES_ROWS, size), BF16)], axis=0)

    def vt_block(blk):
        return vt_rows(pl.multiple_of(blk * QB, QB), QB)

    def with_table(table):
        return jnp.concatenate([qstack, table.astype(BF16)], axis=1)

    def grp(x, j):
        return x[j * GW:(j + 1) * GW]

    def lanes(x, j):
        return x[:, j * GW:(j + 1) * GW]

    GW = cols if mode == "sel" else 2 * QB
    n_grp = cols // GW
    ip = jnp.maximum(i - 1, 0)
    t_prev = tab_ref[0, 0]
    if mode == "sel":
        sel_q = sel_ref[0, 0]
        allowed = _dot(sel_q, e_ref[:, pl.ds(pl.multiple_of(ip * QB, QB), QB)])
        t_prev = jnp.where(jnp.concatenate([allowed] * hg, axis=0) > 0.5, t_prev, NEG_INF)
    t_prev = jnp.where(i > 0, t_prev, NEG_INF)
    q_prev = with_table(t_prev)
    q_cur = with_table(tab_ref[0, 1])
    k_prev = jnp.concatenate([k_block(ip), eye], axis=1)
    k_cur = jnp.concatenate([k_block(i), eye], axis=1)
    vt_near = jnp.concatenate([vt_block(ip), vt_block(i)], axis=1)

    if mode == "swa":
        head = lax.broadcasted_iota(jnp.int32, (1, cols), 1) // QB
        sink = jnp.zeros((1, cols), F32)
        for h in range(hg):
            sink = jnp.where(head == h, sink_ref[g * hg + h], sink)

    elif mode == "win":
        n_far = NSA_WINDOW // QB - 1
        backs = range(n_far + 1, 1, -1)
        onehot = jnp.concatenate([eye, jnp.zeros(((n_far - 1) * QB, QB), BF16)], axis=0)
        k_far = jnp.concatenate(
            [jnp.concatenate([k_block(jnp.maximum(i - bk, 0)) for bk in backs], axis=0), onehot], axis=1)
        vt_far = jnp.concatenate([vt_block(jnp.maximum(i - bk, 0)) for bk in backs], axis=1)
        q_tri = with_table(tab_ref[0, 2])
        in_seq = lax.broadcasted_iota(jnp.int32, (n_far * QB, 1), 0) >= (n_far + 1 - i) * QB

    else:
        blk = lax.broadcasted_iota(jnp.int32, (QB, LANES), 1)
        pick = jnp.logical_and(sel_q.astype(F32) > 0.5, blk < 2 * (i - 1))
        sel_bias = jnp.where(pick, 0.0, NEG_INF)
        q_far = with_table(jnp.concatenate([sel_bias] * hg, axis=0))
        per = chunk // QB
        n_chunks = (jnp.maximum(i - 1, 0) + per - 1) // per
        def far_scores(c):
            off = pl.multiple_of(c * chunk, chunk)
            k_rows = jnp.concatenate([k_ref[pl.ds(off, chunk), :], et_ref[pl.ds(off, chunk), :]], axis=1)
            return _dot_nt(k_rows, q_far)

        def sweep(c, cur_ref, nxt_ref):
            nxt_ref[...] = far_scores(jnp.minimum(c + 1, n_chunks - 1))
            sf = cur_ref[...]
            m_old = m_ref[...]
            m_new = jnp.maximum(m_old, jnp.max(sf, axis=0, keepdims=True))
            pf = jnp.exp2(sf - m_new).astype(BF16)
            acc_ref[...] = (jnp.exp2(m_old - m_new) * acc_ref[...]
                            + _dot(vt_rows(pl.multiple_of(c * chunk, chunk), chunk), pf))
            m_ref[...] = m_new

        def body(pair, carry):
            sweep(2 * pair, sa_ref, sb_ref)

            @pl.when(2 * pair + 1 < n_chunks)
            def _():
                sweep(2 * pair + 1, sb_ref, sa_ref)

            return carry

        m_ref[...] = jnp.full((1, cols), M_INIT, F32)
        acc_ref[...] = jnp.zeros((HEAD_DIM + ONES_ROWS, cols), F32)
        sa_ref[...] = far_scores(0)
        lax.fori_loop(0, (n_chunks + 1) // 2, body, 0)
        m_far = m_ref[...]
        acc_far = acc_ref[...]

    def near_scores(j):
        parts = [_dot_nt(k_prev, grp(q_prev, j)), _dot_nt(k_cur, grp(q_cur, j))]
        if mode == "win":
            parts.append(jnp.where(in_seq, _dot_nt(k_far, grp(q_tri, j)), NEG_INF))
        return parts

    def near_softmax(j, parts):
        m = jnp.max(parts[0], axis=0, keepdims=True)
        for s in parts[1:]:
            m = jnp.maximum(m, jnp.max(s, axis=0, keepdims=True))
        if mode == "swa":
            m = jnp.maximum(m, lanes(sink, j))
        if mode == "sel":
            m = jnp.maximum(m, lanes(m_far, j))
        return m, [jnp.exp2(s - m).astype(BF16) for s in parts]

    def near_output(j, m, probs):
        acc = _dot(vt_near, jnp.concatenate(probs[:2], axis=0))
        if mode == "win":
            acc = acc + _dot(vt_far, probs[2])
        if mode == "sel":
            acc = jnp.exp2(lanes(m_far, j) - m) * lanes(acc_far, j) + acc
        den = acc[HEAD_DIM:HEAD_DIM + 1]
        if mode == "swa":
            den = den + jnp.exp2(lanes(sink, j) - m)
        o_t = acc[:HEAD_DIM] * (1.0 / den)
        for pp in range(GW // (2 * QB)):
            pair = jnp.concatenate([o_t[:, 2 * pp * QB:(2 * pp + 1) * QB],
                                    o_t[:, (2 * pp + 1) * QB:(2 * pp + 2) * QB]], axis=0)
            c0 = (j * (GW // (2 * QB)) + pp) * LANES
            o_ref[:, c0:c0 + LANES] = pair.T.astype(o_ref.dtype)

    _pipeline3(n_grp, near_scores, near_softmax, near_output)


def _band_attention(proj, v_t, tabs, b, t, *, mode, hg, c_q, c_k, v_blk, sinks=None, sel=None, emats=None):
    nb = t // QB
    n_groups = 2
    qw = hg * HEAD_DIM
    chunk = min(512, t)
    in_specs = [
        pl.BlockSpec((QB, qw), lambda bi, g, i: (bi * nb + i, c_q // qw + g)),
        pl.BlockSpec((t, LANES), lambda bi, g, i: (bi, c_k // LANES + g)),
        pl.BlockSpec((1, HEAD_DIM, t), lambda bi, g, i: (bi, v_blk + g, 0)),
        pl.BlockSpec((1,) + tabs.shape[1:], lambda bi, g, i: (g, 0, 0, 0)),
    ]
    args = [proj, proj, v_t, tabs]
    scratch = []
    if mode == "swa":
        in_specs.append(pl.BlockSpec(memory_space=pltpu.SMEM))
        args.append(sinks.astype(F32) * LOG2E)
    if mode == "sel":
        emat, emat_t = emats
        in_specs.append(pl.BlockSpec((1, 1, QB, LANES), lambda bi, g, i: (bi, g, i, 0)))
        in_specs.append(pl.BlockSpec((LANES, t), lambda bi, g, i: (0, 0)))
        in_specs.append(pl.BlockSpec((t, LANES), lambda bi, g, i: (0, 0)))
        args += [sel, emat, emat_t]
        scratch = [pltpu.VMEM((chunk, hg * QB), F32), pltpu.VMEM((chunk, hg * QB), F32),
                   pltpu.VMEM((1, hg * QB), F32), pltpu.VMEM((HEAD_DIM + ONES_ROWS, hg * QB), F32)]
    return pl.pallas_call(
        functools.partial(_band_kernel, mode=mode, hg=hg, chunk=chunk),
        grid=(b, n_groups, nb),
        in_specs=in_specs,
        out_specs=pl.BlockSpec((QB, qw), lambda bi, g, i: (bi * nb + i, g)),
        out_shape=jax.ShapeDtypeStruct((b * t, n_groups * qw), BF16),
        scratch_shapes=scratch,
        compiler_params=_cparams(("parallel", "parallel", "arbitrary")),
        name="band_" + mode,
    )(*args)


def _compress_kernel(x_ref, pos_ref, w1_ref, b1_ref, w2_ref, b2_ref, o_ref):
    x = (x_ref[0].astype(F32) + pos_ref[0]).astype(BF16)
    hid = jax.nn.gelu(_dot(x, w1_ref[0]) + b1_ref[0])
    o_ref[0] = (_dot(hid.astype(BF16), w2_ref[0]) + b2_ref[0]).astype(o_ref.dtype)


def _compress(flat, pos, w1, b1, w2d, b2d, tr=256):
    _, r, cin = flat.shape
    return pl.pallas_call(
        _compress_kernel,
        grid=(2, r // tr),
        in_specs=[pl.BlockSpec((1, tr, cin), lambda s, i: (s, i, 0)),
                  pl.BlockSpec((1, 1, cin), lambda s, i: (s, 0, 0)),
                  pl.BlockSpec((1, cin, NSA_CMP_HIDDEN), lambda s, i: (s, 0, 0)),
                  pl.BlockSpec((1, 1, NSA_CMP_HIDDEN), lambda s, i: (s, 0, 0)),
                  pl.BlockSpec((1, NSA_CMP_HIDDEN, LANES), lambda s, i: (s, 0, 0)),
                  pl.BlockSpec((1, 1, LANES), lambda s, i: (s, 0, 0))],
        out_specs=pl.BlockSpec((1, tr, LANES), lambda s, i: (s, i, 0)),
        out_shape=jax.ShapeDtypeStruct((2, r, LANES), BF16),
        compiler_params=_cparams(("parallel", "parallel")),
        name="nsa_compress",
    )(flat, pos, w1, b1, w2d, b2d)


def _cmp_kernel(q_ref, kc_ref, vct_ref, ztab_ref, mmat_ref, o_ref, sel_ref, *, hg, ncp, ns, n_sel):
    i = pl.program_id(2)
    cols = hg * QB
    qstack = _stack_heads(q_ref, hg)
    s = _dot_nt(kc_ref[0, 0], qstack)
    start = pl.multiple_of(ncp - 8 * i, 8)
    delta = jnp.concatenate([ztab_ref[h, pl.ds(start, ncp), :] for h in range(hg)], axis=1)
    nn = lax.broadcasted_iota(jnp.int32, (ncp, QB), 0)
    qq = lax.broadcasted_iota(jnp.int32, (ncp, QB), 1)
    valid1 = (i * QB + qq - NSA_CMP_STRIDE * nn - (NSA_CMP_LEN - 1)) >= 0
    valid = jnp.concatenate([valid1] * hg, axis=1)
    s = jnp.where(valid, s + delta, NEG_INF)
    m = jnp.max(s, axis=0, keepdims=True)
    e = jnp.exp2(s - m)
    p = jnp.where(valid, e * (1.0 / jnp.sum(e, axis=0, keepdims=True)), 0.0)
    _store_heads(o_ref, _dot(vct_ref[0, 0], p.astype(BF16)), hg)

    imp = p[:, 0:QB]
    for h in range(1, hg):
        imp = imp + p[:, h * QB:(h + 1) * QB]
    hi, mid, lo = _split3(imp)
    mm = mmat_ref[...]
    imp_sel = _dot(mm, hi) + _dot(mm, mid) + _dot(mm, lo)
    sb = lax.broadcasted_iota(jnp.int32, (ns, QB), 0)
    tb = (i * QB + lax.broadcasted_iota(jnp.int32, (ns, QB), 1)) // NSA_SEL_LEN
    forced = jnp.logical_or(jnp.logical_or(sb == 0, sb == tb), sb == tb - 1)
    score = jnp.where(forced, FORCE_SCORE, jnp.where(sb <= tb, imp_sel, NEG_INF))
    sub = 8
    rows = [score[v * sub:(v + 1) * sub, :] for v in range(ns // sub)]
    ranks = [jnp.zeros((sub, QB), F32) for _ in rows]
    sub_idx = lax.broadcasted_iota(jnp.int32, (sub, QB), 0)
    for r in range(ns):
        other = score[r:r + 1, :]
        for v, mine in enumerate(rows):
            if v < r // sub:
                inc = jnp.where(other > mine, 1.0, 0.0)
            elif v > r // sub:
                inc = jnp.where(other >= mine, 1.0, 0.0)
            else:
                tie = jnp.where(sub_idx > r % sub, 1.0, 0.0)
                inc = jnp.where(other > mine, 1.0, jnp.where(other == mine, tie, 0.0))
            ranks[v] = ranks[v] + inc
    chosen = [jnp.where(rk < n_sel, 1.0, 0.0) for rk in ranks]
    chosen.append(jnp.zeros((LANES - ns, QB), F32))
    sel_ref[0, 0] = jnp.concatenate(chosen, axis=0).T.astype(sel_ref.dtype)


def _cmp_attention(proj, kc, vc_t, ztab, mmat_t, b, t):
    nb = t // QB
    hg = NSA_GROUP
    qw = hg * HEAD_DIM
    ncp = t // NSA_CMP_STRIDE
    ns = t // NSA_SEL_LEN
    n_sel = min(NSA_SEL_TOPK, ns)
    return pl.pallas_call(
        functools.partial(_cmp_kernel, hg=hg, ncp=ncp, ns=ns, n_sel=n_sel),
        grid=(b, NSA_KV_HEADS, nb),
        in_specs=[
            pl.BlockSpec((QB, qw), lambda bi, g, i: (bi * nb + i, C_NQ // qw + g)),
            pl.BlockSpec((1, 1, ncp, LANES), lambda bi, g, i: (bi, g, 0, 0)),
            pl.BlockSpec((1, 1, HEAD_DIM, ncp), lambda bi, g, i: (bi, g, 0, 0)),
            pl.BlockSpec((hg, 2 * ncp, QB), lambda bi, g, i: (g, 0, 0)),
            pl.BlockSpec((ns, ncp), lambda bi, g, i: (0, 0)),
        ],
        out_specs=[pl.BlockSpec((QB, qw), lambda bi, g, i: (bi * nb + i, g)),
                   pl.BlockSpec((1, 1, QB, LANES), lambda bi, g, i: (bi, g, i, 0))],
        out_shape=[jax.ShapeDtypeStruct((b * t, NSA_W), BF16),
                   jax.ShapeDtypeStruct((b, NSA_KV_HEADS, t, LANES), BF16)],
        compiler_params=_cparams(("parallel", "parallel", "arbitrary")),
        name="nsa_cmp_select",
    )(proj, kc, vc_t, ztab, mmat_t)


def _combine_kernel(oc_ref, os_ref, ow_ref, g_ref, o_ref):
    sg = jax.nn.sigmoid(g_ref[...])
    lane = lax.broadcasted_iota(jnp.int32, (1, LANES), 1)
    upper = lane >= HEAD_DIM
    for p in range(NSA_HEADS // 2):
        acc = None
        for br, ref in enumerate((oc_ref, os_ref, ow_ref)):
            c0 = G_NG + NSA_N_BRANCH * (2 * p) + br
            c1 = G_NG + NSA_N_BRANCH * (2 * p + 1) + br
            gate = jnp.where(upper, sg[:, c1:c1 + 1], sg[:, c0:c0 + 1])
            term = gate * ref[:, p * LANES:(p + 1) * LANES].astype(F32)
            acc = term if acc is None else acc + term
        o_ref[:, p * LANES:(p + 1) * LANES] = acc.astype(o_ref.dtype)


def _nsa_combine(o_cmp, o_slc, o_win, gates, tm=512):
    n = o_cmp.shape[0]
    spec = pl.BlockSpec((tm, NSA_W), lambda i: (i, 0))
    return pl.pallas_call(
        _combine_kernel,
        grid=(n // tm,),
        in_specs=[spec, spec, spec, pl.BlockSpec((tm, D_GATE), lambda i: (i, 0))],
        out_specs=spec,
        out_shape=jax.ShapeDtypeStruct((n, NSA_W), BF16),
        compiler_params=_cparams(("parallel",)),
        name="nsa_combine",
    )(o_cmp, o_slc, o_win, gates)


def _ffn_kernel(te_ref, nu_ref, *refs, has_res):
    if has_res:
        x_ref, wg_ref, wu_ref, wd_ref, res_ref, o_ref, acc_ref = refs
    else:
        x_ref, wg_ref, wu_ref, wd_ref, o_ref, acc_ref = refs
    del te_ref
    i = pl.program_id(0)
    k = pl.program_id(1)
    last = pl.num_programs(1) - 1
    used = i < nu_ref[0]

    @pl.when(jnp.logical_and(used, k == 0))
    def _():
        acc_ref[...] = jnp.zeros_like(acc_ref)

    @pl.when(used)
    def _():
        x = x_ref[...]
        gate = _dot(x, wg_ref[0])
        up = _dot(x, wu_ref[0])
        hid = (jax.nn.silu(gate) * up).astype(BF16)
        acc_ref[...] += _dot(hid, wd_ref[0])

    @pl.when(jnp.logical_and(used, k == last))
    def _():
        y = acc_ref[...]
        if has_res:
            y = y + res_ref[...]
        o_ref[...] = y.astype(o_ref.dtype)

    @pl.when(jnp.logical_and(jnp.logical_not(used), k == last))
    def _():
        o_ref[...] = jnp.zeros_like(o_ref)


def _ffn(x, w_gate, w_up, w_down, tile_expert, n_used, out_dtype, residual=None, tm=512, tf=512):
    r, d = x.shape
    nk = D_FF // tf
    n_tiles = r // tm

    def tile(i, nu):
        return jnp.minimum(i, nu[0] - 1)

    def kk(i, k, nu):
        return jnp.where(i < nu[0], k, nk - 1)

    in_specs = [
        pl.BlockSpec((tm, d), lambda i, k, te, nu: (tile(i, nu), 0)),
        pl.BlockSpec((1, d, tf), lambda i, k, te, nu: (te[tile(i, nu)], 0, kk(i, k, nu))),
        pl.BlockSpec((1, d, tf), lambda i, k, te, nu: (te[tile(i, nu)], 0, kk(i, k, nu))),
        pl.BlockSpec((1, tf, d), lambda i, k, te, nu: (te[tile(i, nu)], kk(i, k, nu), 0)),
    ]
    args = [x, w_gate, w_up, w_down]
    if residual is not None:
        in_specs.append(pl.BlockSpec((tm, d), lambda i, k, te, nu: (i, 0)))
        args.append(residual)
    return pl.pallas_call(
        functools.partial(_ffn_kernel, has_res=residual is not None),
        grid_spec=pltpu.PrefetchScalarGridSpec(
            num_scalar_prefetch=2,
            grid=(n_tiles, nk),
            in_specs=in_specs,
            out_specs=pl.BlockSpec((tm, d), lambda i, k, te, nu: (i, 0)),
            scratch_shapes=[pltpu.VMEM((tm, d), F32)],
        ),
        out_shape=jax.ShapeDtypeStruct((r, d), out_dtype),
        compiler_params=_cparams(("arbitrary", "arbitrary")),
        name="swiglu_ffn",
    )(tile_expert, n_used, *args)


def _router_kernel(l_ref, o_ref):
    lane = lax.broadcasted_iota(jnp.int32, l_ref.shape, 1)
    lf = lane.astype(F32)
    lg = jnp.where(lane < N_EXPERTS, l_ref[...], REMOVED)
    v1 = jnp.max(lg, axis=1, keepdims=True)
    i1 = jnp.min(jnp.where(lg == v1, lf, float(LANES)), axis=1, keepdims=True)
    lg2 = jnp.where(lf == i1, REMOVED, lg)
    v2 = jnp.max(lg2, axis=1, keepdims=True)
    i2 = jnp.min(jnp.where(lg2 == v2, lf, float(LANES)), axis=1, keepdims=True)
    e2 = jnp.exp(v2 - v1)
    den = 1.0 + e2
    p1 = 1.0 / den
    p2 = e2 / den
    out = jnp.where(lane == N_EXPERTS, i1, 0.0)
    out = jnp.where(lane == N_EXPERTS + 1, i2, out)
    out = jnp.where(lane == N_EXPERTS + 2, p1, out)
    out = jnp.where(lane == N_EXPERTS + 3, p2, out)
    o_ref[...] = out


def _router_top2(logits, tm=512):
    n = logits.shape[0]
    spec = pl.BlockSpec((tm, LANES), lambda i: (i, 0))
    return pl.pallas_call(
        _router_kernel,
        grid=(n // tm,),
        in_specs=[spec],
        out_specs=spec,
        out_shape=jax.ShapeDtypeStruct((n, LANES), F32),
        compiler_params=_cparams(("parallel",)),
        name="moe_router_top2",
    )(logits)


def _moe_combine_kernel(h_ref, a_ref, b_ref, top_ref, g_ref, o_ref, *, final_norm):
    p0 = top_ref[:, N_EXPERTS + TOP_K:N_EXPERTS + TOP_K + 1]
    p1 = top_ref[:, N_EXPERTS + TOP_K + 1:N_EXPERTS + TOP_K + 2]
    y = h_ref[...] + p0 * a_ref[...].astype(F32) + p1 * b_ref[...].astype(F32)
    if final_norm:
        y = y * lax.rsqrt(jnp.mean(y * y, axis=-1, keepdims=True) + RMS_EPS) * g_ref[...]
    o_ref[...] = y


def _moe_combine(h, ya, yb, top, gain, final_norm, tm=512):
    n, d = h.shape
    row = pl.BlockSpec((tm, d), lambda i: (i, 0))
    return pl.pallas_call(
        functools.partial(_moe_combine_kernel, final_norm=final_norm),
        grid=(n // tm,),
        in_specs=[row, row, row, pl.BlockSpec((tm, LANES), lambda i: (i, 0)),
                  pl.BlockSpec((1, d), lambda i: (0, 0))],
        out_specs=row,
        out_shape=jax.ShapeDtypeStruct((n, d), F32),
        compiler_params=_cparams(("parallel",)),
        name="moe_combine",
    )(h, ya, yb, top, gain.reshape(1, d).astype(F32))


def _t5_bucket_np(dist):
    n = np.maximum(dist, 0)
    max_exact = REL_BUCKETS // 2
    nf = np.maximum(n, 1).astype(np.float32)
    large = max_exact + (np.log(nf / np.float32(max_exact)) / np.float32(math.log(REL_MAX_DIST / max_exact))
                         * np.float32(REL_BUCKETS - max_exact)).astype(np.int32)
    large = np.minimum(large, REL_BUCKETS - 1)
    return np.where(n < max_exact, n, large).astype(np.int32)


def _band_tables(rel_tab, t):
    q = np.arange(QB)[:, None]
    k = np.arange(QB)[None, :]
    rel = rel_tab.astype(F32) * LOG2E
    toep = jnp.take(rel, jnp.asarray(_t5_bucket_np((q - k) % QB)), axis=0)
    toep = toep.transpose(2, 0, 1)
    before = jnp.asarray(k > q)
    far = rel[REL_BUCKETS - 1, SWA_HEADS:]

    def tiles(tab, fill_prev, extra=()):
        heads = tab.shape[0]
        parts = [jnp.where(before, tab, fill_prev), jnp.where(before, NEG_INF, tab)]
        parts += [jnp.broadcast_to(e, tab.shape) for e in extra]
        return jnp.stack([p.reshape(2, heads // 2 * QB, QB) for p in parts], axis=1)

    tabs_swa = tiles(toep[:SWA_HEADS], NEG_INF)
    tabs_nsa = tiles(toep[SWA_HEADS:] - far[:, None, None], 0.0, extra=[jnp.where(before, 0.0, NEG_INF)])
    ncp = t // NSA_CMP_STRIDE
    m = np.arange(-9, 7)[:, None]
    qr = np.arange(QB)[None, :]
    d = qr - NSA_CMP_STRIDE * m - (NSA_CMP_LEN - 1)
    inband = (d >= 0) & (d < REL_MAX_DIST)
    vals = jnp.take(rel[:, SWA_HEADS:], jnp.asarray(_t5_bucket_np(np.clip(d, 0, None))), axis=0)
    vals = vals.transpose(2, 0, 1) - far[:, None, None]
    band = jnp.where(jnp.asarray(inband), vals, 0.0)
    ztab = jnp.pad(band, ((0, 0), (ncp - 9, ncp - 7), (0, 0)))
    return tabs_swa, tabs_nsa, ztab


def _selection_matrices(t):
    ncp = t // NSA_CMP_STRIDE
    ns = t // NSA_SEL_LEN
    per = NSA_SEL_LEN // NSA_CMP_STRIDE
    ratio = NSA_CMP_LEN // NSA_CMP_STRIDE
    mmat_t = np.zeros((ns, ncp), np.float32)
    for n in range(ncp - 1):
        for j in range(ratio):
            mmat_t[(n + j) // per, n] += 1.0
    emat_t = (np.arange(t)[:, None] // NSA_SEL_LEN == np.arange(LANES)[None, :]).astype(np.float32)
    return jnp.asarray(mmat_t, BF16), (jnp.asarray(emat_t.T, BF16), jnp.asarray(emat_t, BF16))


def _dup(w):
    d = w.shape[0]
    w = w.reshape(d, -1, 1, HEAD_DIM)
    return jnp.broadcast_to(w, (d, w.shape[1], 2, HEAD_DIM)).reshape(d, -1)


def _prep_in_weights(w_in_l):
    sizes = (FOX_W, FOX_W, FOX_W, FOX_HEADS, SWA_W, SWA_KV_W, SWA_KV_W,
             NSA_W, NSA_KV_W, NSA_KV_W, NSA_KV_W, NSA_KV_W, NSA_KV_W, NSA_KV_W, NSA_HEADS * NSA_N_BRANCH)
    splits = [int(s) for s in np.cumsum(sizes)[:-1]]
    (fq, fk, fv, ff, sq, sk, sv, nq, nkc, nvc, nks, nvs, nkw, nvw, ng) = jnp.split(w_in_l, splits, axis=-1)
    scale = HEAD_DIM ** -0.5 * LOG2E
    w_proj = jnp.concatenate([fq * scale, fk, fv, sq * scale, nq * scale, _dup(sk), _dup(nks), _dup(nkw),
                              sv, nvs, nvw, nkc, nvc], axis=-1).astype(BF16)
    pad = jnp.zeros((w_in_l.shape[0], D_GATE - FOX_HEADS - NSA_HEADS * NSA_N_BRANCH), w_in_l.dtype)
    w_gate = jnp.concatenate([ff, ng, pad], axis=-1).astype(BF16)
    return w_proj, w_gate


def _compress_inputs(proj, b, t):
    g = NSA_KV_HEADS
    flats = []
    for c0 in (C_NKC, C_NVC):
        x = proj[:, c0:c0 + NSA_KV_W].reshape(b, t, g, HEAD_DIM).transpose(0, 2, 1, 3)
        width = NSA_CMP_LEN * HEAD_DIM
        even = x.reshape(b, g, t // NSA_CMP_LEN, width)
        odd = x[:, :, NSA_CMP_STRIDE:t - NSA_CMP_STRIDE].reshape(b, g, t // NSA_CMP_LEN - 1, width)
        odd = jnp.pad(odd, ((0, 0), (0, 0), (0, 1), (0, 0)))
        flats.append(jnp.stack([even, odd], axis=3).reshape(b * g * (t // NSA_CMP_STRIDE), width))
    return jnp.stack(flats, axis=0)


def _mixer(hn, b, t, w_in_l, forget_bias, sinks, cmp_pos, cmp_w1, cmp_b1, cmp_w2, cmp_b2, tables):
    tabs_swa, tabs_nsa, ztab, mmat_t, emats = tables
    w_proj, w_gate = _prep_in_weights(w_in_l)
    proj = _matmul([hn], [w_proj], BF16, tn=PROJ_TN, name="in_proj")
    gates = _matmul([hn], [w_gate], F32, tn=D_GATE, name="gate_proj")
    fv_t = proj[:, C_FV:C_FV + FOX_W].reshape(b, t, FOX_W).transpose(0, 2, 1)
    v_t = proj[:, C_V:C_V + 3 * NSA_KV_W].reshape(b, t, 3 * NSA_KV_W).transpose(0, 2, 1)

    qx, kx = _fox_decay_operands(gates, forget_bias, b, t)
    o_fox = _fox_attention(proj, fv_t, qx, kx, b, t)

    o_swa = _band_attention(proj, v_t, tabs_swa, b, t, mode="swa", hg=SWA_GROUP,
                            c_q=C_SQ, c_k=C_SK, v_blk=0, sinks=sinks)

    flat = _compress_inputs(proj, b, t)
    pos = cmp_pos.reshape(2, 1, NSA_CMP_LEN * HEAD_DIM).astype(F32)
    w2d = jnp.concatenate([cmp_w2, cmp_w2], axis=-1).astype(BF16)
    b2d = jnp.concatenate([cmp_b2, cmp_b2], axis=-1).reshape(2, 1, LANES).astype(F32)
    kvc = _compress(flat, pos, cmp_w1.astype(BF16), cmp_b1.reshape(2, 1, NSA_CMP_HIDDEN).astype(F32), w2d, b2d)
    ncp = t // NSA_CMP_STRIDE
    kvc = kvc.reshape(2, b, NSA_KV_HEADS, ncp, LANES)
    vc_t = kvc[1, :, :, :, :HEAD_DIM].transpose(0, 1, 3, 2)
    o_cmp, sel = _cmp_attention(proj, kvc[0], vc_t, ztab, mmat_t, b, t)
    o_slc = _band_attention(proj, v_t, tabs_nsa, b, t, mode="sel", hg=NSA_GROUP,
                            c_q=C_NQ, c_k=C_NKS, v_blk=2, sel=sel, emats=emats)
    o_win = _band_attention(proj, v_t, tabs_nsa, b, t, mode="win", hg=NSA_GROUP,
                            c_q=C_NQ, c_k=C_NKW, v_blk=4)
    o_nsa = _nsa_combine(o_cmp, o_slc, o_win, gates)
    return o_fox, o_swa, o_nsa


def _moe(hn, h, router, w_gate, w_up, w_down, gain, final_norm, tm=512):
    n, d = hn.shape
    w_r = jnp.zeros((d, LANES), BF16).at[:, :N_EXPERTS].set(router.astype(BF16))
    logits = _matmul([hn], [w_r], F32, tn=LANES, name="router_logits")
    top = _router_top2(logits)
    e_idx = top[:, N_EXPERTS:N_EXPERTS + TOP_K].astype(jnp.int32)
    e_flat = e_idx.reshape(-1)
    onehot = (e_flat[:, None] == jnp.arange(N_EXPERTS)[None, :]).astype(jnp.int32)
    csum = jnp.cumsum(onehot, axis=0)
    counts = csum[-1]
    rank = jnp.take_along_axis(csum, e_flat[:, None], axis=1)[:, 0] - 1
    padded = ((counts + tm - 1) // tm) * tm
    ends = jnp.cumsum(padded)
    starts = ends - padded
    dest = starts[e_flat] + rank
    r_pad = n * TOP_K + N_EXPERTS * tm
    src_tok = jnp.zeros((r_pad,), jnp.int32).at[dest].set(jnp.arange(n * TOP_K, dtype=jnp.int32) // TOP_K)
    tile_start = jnp.arange(r_pad // tm, dtype=jnp.int32) * tm
    tile_expert = jnp.minimum(jnp.sum(tile_start[:, None] >= ends[None, :], axis=1), N_EXPERTS - 1).astype(jnp.int32)
    n_used = (ends[-1:] // tm).astype(jnp.int32)
    xs = hn.at[src_tok].get(mode="promise_in_bounds")
    y = _ffn(xs, w_gate, w_up, w_down, tile_expert, n_used, BF16, tm=tm)
    dest = dest.reshape(n, TOP_K)
    ya = y.at[dest[:, 0]].get(mode="promise_in_bounds")
    yb = y.at[dest[:, 1]].get(mode="promise_in_bounds")
    return _moe_combine(h, ya, yb, top, gain, final_norm)


def kernel(x, attn_norm, w_in, fox_forget_bias, swa_sinks, nsa_cmp_pos, nsa_cmp_w1, nsa_cmp_b1, nsa_cmp_w2,
           nsa_cmp_b2, w_out, rel_bias_table, ffn_norm, dense_w_gate, dense_w_up, dense_w_down, moe_router,
           moe_w_gate, moe_w_up, moe_w_down, final_norm):
    b, t, d = x.shape
    n = b * t
    depth = w_in.shape[0]
    tables = _band_tables(rel_bias_table, t) + _selection_matrices(t)
    h = x.reshape(n, d)
    for layer in range(depth):
        hn = _rmsnorm(h, attn_norm[layer], BF16)
        o_fox, o_swa, o_nsa = _mixer(hn, b, t, w_in[layer], fox_forget_bias[layer], swa_sinks[layer],
                                     nsa_cmp_pos[layer], nsa_cmp_w1[layer], nsa_cmp_b1[layer],
                                     nsa_cmp_w2[layer], nsa_cmp_b2[layer], tables)
        wo = w_out[layer].astype(BF16)
        h = _matmul([o_fox, o_swa, o_nsa], [wo[:FOX_W], wo[FOX_W:FOX_W + SWA_W], wo[FOX_W + SWA_W:]],
                    F32, residual=h, tn=1024, name="out_proj")
        hn = _rmsnorm(h, ffn_norm[layer], BF16)
        i = layer // 2
        last = layer == depth - 1
        if layer % 2 == 0:
            tm = 512
            zeros = jnp.zeros((n // tm,), jnp.int32)
            h = _ffn(hn, dense_w_gate[i][None].astype(BF16), dense_w_up[i][None].astype(BF16),
                     dense_w_down[i][None].astype(BF16), zeros, jnp.full((1,), n // tm, jnp.int32),
                     F32, residual=h, tm=tm)
            if last:
                h = _rmsnorm(h, final_norm, F32)
        else:
            h = _moe(hn, h, moe_router[i], moe_w_gate[i].astype(BF16), moe_w_up[i].astype(BF16),
                     moe_w_down[i].astype(BF16), final_norm, last)
    return h.reshape(b, t, d)
```

```python
import functools
import math

import numpy as np
import jax
import jax.numpy as jnp
from jax import lax
from jax.experimental import pallas as pl
from jax.experimental.pallas import tpu as pltpu

F32 = jnp.float32
BF16 = jnp.bfloat16

D_MODEL = 2048
HEAD_DIM = 64
FOX_HEADS = 8
FOX_W = FOX_HEADS * HEAD_DIM
SWA_HEADS = 8
SWA_KV_HEADS = 2
SWA_GROUP = SWA_HEADS // SWA_KV_HEADS
SWA_W = SWA_HEADS * HEAD_DIM
SWA_KV_W = SWA_KV_HEADS * HEAD_DIM
SWA_WINDOW = 128
NSA_HEADS = 16
NSA_KV_HEADS = 2
NSA_GROUP = NSA_HEADS // NSA_KV_HEADS
NSA_W = NSA_HEADS * HEAD_DIM
NSA_KV_W = NSA_KV_HEADS * HEAD_DIM
NSA_CMP_LEN = 32
NSA_CMP_STRIDE = 16
NSA_CMP_HIDDEN = 256
NSA_SEL_LEN = 64
NSA_SEL_TOPK = 16
NSA_WINDOW = 512
NSA_N_BRANCH = 3
FORCE_SCORE = 1e9
REL_BUCKETS = 32
REL_MAX_DIST = 128
D_FF = 5632
N_EXPERTS = 8
TOP_K = 2
RMS_EPS = 1e-6
NEG_INF = -1e30
M_INIT = -1e29
REMOVED = -3e38
LOG2E = 1.4426950408889634
ONES_ROWS = 16

LANES = 128
QB = 128
VMEM_LIMIT = 56 * 1024 * 1024

C_FQ, C_FK, C_FV = 0, 512, 1024
C_SQ = 1536
C_NQ = 2048
C_SK, C_NKS, C_NKW = 3072, 3328, 3584
C_V = 3840
C_NKC, C_NVC = 4224, 4352
D_PROJ = 4480
PROJ_TN = 640
G_NG = FOX_HEADS
D_GATE = 128


def _cparams(sem, vmem=VMEM_LIMIT):
    return pltpu.CompilerParams(dimension_semantics=sem, vmem_limit_bytes=vmem)


def _dot(a, b):
    return jnp.dot(a, b, preferred_element_type=F32)


def _dot_nt(a, b):
    return lax.dot_general(a, b, (((1,), (1,)), ((), ())), preferred_element_type=F32)


def _split3(x):
    hi = x.astype(BF16)
    r1 = x - hi.astype(F32)
    mid = r1.astype(BF16)
    lo = (r1 - mid.astype(F32)).astype(BF16)
    return hi, mid, lo


def _pipeline3(n, stage1, stage2, stage3):
    first, second, outs = {}, {}, []
    for step in range(n + 2):
        if step < n:
            first[step] = stage1(step)
        if 0 <= step - 1 < n:
            second[step - 1] = stage2(step - 1, first.pop(step - 1))
        if 0 <= step - 2 < n:
            outs.append(stage3(step - 2, *second.pop(step - 2)))
    return outs


def _rmsnorm_kernel(x_ref, g_ref, o_ref):
    x = x_ref[...]
    y = x * lax.rsqrt(jnp.mean(x * x, axis=-1, keepdims=True) + RMS_EPS)
    o_ref[...] = (y * g_ref[...]).astype(o_ref.dtype)


def _rmsnorm(h, g, out_dtype, tm=512):
    n, d = h.shape
    return pl.pallas_call(
        _rmsnorm_kernel,
        grid=(n // tm,),
        in_specs=[pl.BlockSpec((tm, d), lambda i: (i, 0)),
                  pl.BlockSpec((1, d), lambda i: (0, 0))],
        out_specs=pl.BlockSpec((tm, d), lambda i: (i, 0)),
        out_shape=jax.ShapeDtypeStruct((n, d), out_dtype),
        compiler_params=_cparams(("parallel",)),
        name="rmsnorm",
    )(h, g.reshape(1, d).astype(F32))


def _mm_kernel(*refs, n_in, has_res):
    o_ref = refs[-1]
    acc = None
    for a in range(n_in):
        d = _dot(refs[a][...], refs[n_in + a][...])
        acc = d if acc is None else acc + d
    if has_res:
        acc = acc + refs[2 * n_in][...]
    o_ref[...] = acc.astype(o_ref.dtype)


def _matmul(xs, ws, out_dtype, residual=None, tm=1024, tn=256, name="matmul"):
    n = xs[0].shape[0]
    m = ws[0].shape[1]
    tm = min(tm, n)
    tn = min(tn, m)
    in_specs = [pl.BlockSpec((tm, x.shape[1]), lambda i, j: (i, 0)) for x in xs]
    in_specs += [pl.BlockSpec((w.shape[0], tn), lambda i, j: (0, j)) for w in ws]
    args = list(xs) + list(ws)
    if residual is not None:
        in_specs.append(pl.BlockSpec((tm, tn), lambda i, j: (i, j)))
        args.append(residual)
    return pl.pallas_call(
        functools.partial(_mm_kernel, n_in=len(xs), has_res=residual is not None),
        grid=(n // tm, m // tn),
        in_specs=in_specs,
        out_specs=pl.BlockSpec((tm, tn), lambda i, j: (i, j)),
        out_shape=jax.ShapeDtypeStruct((n, m), out_dtype),
        compiler_params=_cparams(("parallel", "arbitrary")),
        name=name,
    )(*args)


def _in_proj_kernel(h_ref, g_ref, w_ref, wgate_ref, proj_ref, gates_ref, xn_ref):
    @pl.when(pl.program_id(1) == 0)
    def _():
        h = h_ref[...]
        y = h * lax.rsqrt(jnp.mean(h * h, axis=-1, keepdims=True) + RMS_EPS)
        xn_ref[...] = (y * g_ref[...]).astype(BF16)
        gates_ref[...] = _dot(xn_ref[...], wgate_ref[...])

    proj_ref[...] = _dot(xn_ref[...], w_ref[...]).astype(proj_ref.dtype)


def _in_proj(h, gain, w_proj, w_gate, tm=1024, tn=PROJ_TN):
    n, d = h.shape
    return pl.pallas_call(
        _in_proj_kernel,
        grid=(n // tm, D_PROJ // tn),
        in_specs=[pl.BlockSpec((tm, d), lambda i, j: (i, 0)),
                  pl.BlockSpec((1, d), lambda i, j: (0, 0)),
                  pl.BlockSpec((d, tn), lambda i, j: (0, j)),
                  pl.BlockSpec((d, D_GATE), lambda i, j: (0, 0))],
        out_specs=[pl.BlockSpec((tm, tn), lambda i, j: (i, j)),
                   pl.BlockSpec((tm, D_GATE), lambda i, j: (i, 0))],
        out_shape=[jax.ShapeDtypeStruct((n, D_PROJ), BF16), jax.ShapeDtypeStruct((n, D_GATE), F32)],
        scratch_shapes=[pltpu.VMEM((tm, d), BF16)],
        compiler_params=_cparams(("parallel", "arbitrary")),
        name="in_proj",
    )(h, gain.reshape(1, d).astype(F32), w_proj, w_gate)


def _cumsum_kernel(g_ref, b_ref, pq_ref, pk_ref, oq_ref, ok_ref, qx_ref, kx_ref, carry_ref, *, tc):
    @pl.when(pl.program_id(1) == 0)
    def _():
        carry_ref[...] = jnp.zeros_like(carry_ref)

    z = g_ref[...] + b_ref[...]
    log_f = jnp.minimum(z, 0.0) - jnp.log1p(jnp.exp(-jnp.abs(z)))
    row = lax.broadcasted_iota(jnp.int32, (tc, tc), 0)
    col = lax.broadcasted_iota(jnp.int32, (tc, tc), 1)
    tri = jnp.where(col <= row, 1.0, 0.0).astype(BF16)
    hi, mid, lo = _split3(log_f)
    c = _dot(tri, hi) + _dot(tri, mid) + _dot(tri, lo) + carry_ref[...]
    carry_ref[...] = c[tc - 1:tc, :]
    parts = jnp.concatenate(_split3(c * LOG2E), axis=1)
    qx_ref[...] = (_dot(parts, pq_ref[...]) + oq_ref[...]).astype(qx_ref.dtype)
    kx_ref[...] = (_dot(parts, pk_ref[...]) + ok_ref[...]).astype(kx_ref.dtype)


def _fox_decay_operands(gates, forget_bias, b, t, tc=256):
    nt = t // tc
    wide = FOX_HEADS * LANES
    bias = jnp.zeros((1, D_GATE), F32).at[0, :FOX_HEADS].set(forget_bias.astype(F32))
    pq = np.zeros((3 * D_GATE, wide), np.float32)
    pk = np.zeros((3 * D_GATE, wide), np.float32)
    oq = np.zeros((1, wide), np.float32)
    ok = np.zeros((1, wide), np.float32)
    for h in range(FOX_HEADS):
        for part in range(3):
            pq[part * D_GATE + h, h * LANES + part] = 1.0
            pk[part * D_GATE + h, h * LANES + 3 + part] = -1.0
            oq[0, h * LANES + 3 + part] = 1.0
            ok[0, h * LANES + part] = 1.0
    const = lambda shape: pl.BlockSpec(shape, lambda bi, ti: (0, 0))
    out_spec = pl.BlockSpec((tc, wide), lambda bi, ti: (bi * nt + ti, 0))
    return pl.pallas_call(
        functools.partial(_cumsum_kernel, tc=tc),
        grid=(b, nt),
        in_specs=[pl.BlockSpec((tc, D_GATE), lambda bi, ti: (bi * nt + ti, 0)),
                  const((1, D_GATE)), const((3 * D_GATE, wide)), const((3 * D_GATE, wide)),
                  const((1, wide)), const((1, wide))],
        out_specs=[out_spec, out_spec],
        out_shape=[jax.ShapeDtypeStruct((b * t, wide), BF16), jax.ShapeDtypeStruct((b * t, wide), BF16)],
        scratch_shapes=[pltpu.VMEM((1, D_GATE), F32)],
        compiler_params=_cparams(("parallel", "arbitrary")),
        name="fox_cumsum",
    )(gates, bias, jnp.asarray(pq, BF16), jnp.asarray(pk, BF16), jnp.asarray(oq), jnp.asarray(ok))


def _fox_kernel(q_ref, k_ref, vt_ref, qx_ref, kx_ref, o_ref, m_sc, acc_sc, *, tq):
    i = pl.program_id(1)
    j = pl.program_id(2)
    lane = lax.broadcasted_iota(jnp.int32, (1, LANES), 1)
    upper = lane >= HEAD_DIM

    @pl.when(j == 0)
    def _():
        m_sc[...] = jnp.full_like(m_sc, M_INIT)
        acc_sc[...] = jnp.zeros_like(acc_sc)

    def step(diagonal):
        if diagonal:
            key = lax.broadcasted_iota(jnp.int32, (tq, tq), 0)
            qry = lax.broadcasted_iota(jnp.int32, (tq, tq), 1)
            causal = key <= qry
        ones = jnp.ones((ONES_ROWS, tq), BF16)

        def qk(h):
            p, a = divmod(h, 2)
            q2 = q_ref[:, p * LANES:(p + 1) * LANES]
            k2 = k_ref[:, p * LANES:(p + 1) * LANES]
            zero = jnp.zeros_like(q2)
            qa = jnp.where(upper, q2, zero) if a else jnp.where(upper, zero, q2)
            q_aug = jnp.concatenate([qa, qx_ref[:, h * LANES:(h + 1) * LANES]], axis=1)
            k_aug = jnp.concatenate([k2, kx_ref[:, h * LANES:(h + 1) * LANES]], axis=1)
            s = _dot_nt(k_aug, q_aug)
            return jnp.where(causal, s, NEG_INF) if diagonal else s

        def softmax(h, s):
            m_prev = m_sc[h]
            m_new = jnp.maximum(m_prev, jnp.max(s, axis=0, keepdims=True))
            m_sc[h] = m_new
            return jnp.exp2(m_prev - m_new), jnp.exp2(s - m_new).astype(BF16)

        def pv(h, alpha, pr):
            vt = jnp.concatenate([vt_ref[0, h * HEAD_DIM:(h + 1) * HEAD_DIM, :], ones], axis=0)
            acc_sc[h] = alpha * acc_sc[h] + _dot(vt, pr)

        _pipeline3(FOX_HEADS, qk, softmax, pv)

    @pl.when(j < i)
    def _():
        step(False)

    @pl.when(j == i)
    def _():
        step(True)
        for p in range(FOX_HEADS // 2):
            outs = []
            for h in (2 * p, 2 * p + 1):
                acc = acc_sc[h]
                outs.append(acc[:HEAD_DIM] * (1.0 / acc[HEAD_DIM:HEAD_DIM + 1]))
            o_ref[:, p * LANES:(p + 1) * LANES] = jnp.concatenate(outs, axis=0).T.astype(o_ref.dtype)


def _fox_attention(proj, v_t, qx, kx, b, t, tq=512):
    nt = t // tq
    wide = FOX_HEADS * LANES
    return pl.pallas_call(
        functools.partial(_fox_kernel, tq=tq),
        grid=(b, nt, nt),
        in_specs=[
            pl.BlockSpec((tq, FOX_W), lambda bi, i, j: (bi * nt + i, C_FQ // FOX_W)),
            pl.BlockSpec((tq, FOX_W), lambda bi, i, j: (bi * nt + jnp.minimum(j, i), C_FK // FOX_W)),
            pl.BlockSpec((1, FOX_W, tq), lambda bi, i, j: (bi, 0, jnp.minimum(j, i))),
            pl.BlockSpec((tq, wide), lambda bi, i, j: (bi * nt + i, 0)),
            pl.BlockSpec((tq, wide), lambda bi, i, j: (bi * nt + jnp.minimum(j, i), 0)),
        ],
        out_specs=pl.BlockSpec((tq, FOX_W), lambda bi, i, j: (bi * nt + i, 0)),
        out_shape=jax.ShapeDtypeStruct((b * t, FOX_W), BF16),
        scratch_shapes=[pltpu.VMEM((FOX_HEADS, 1, tq), F32),
                        pltpu.VMEM((FOX_HEADS, HEAD_DIM + ONES_ROWS, tq), F32)],
        compiler_params=_cparams(("parallel", "parallel", "arbitrary")),
        name="fox_attention",
    )(proj, proj, v_t, qx, kx)


def _stack_heads(q_ref, hg):
    lane = lax.broadcasted_iota(jnp.int32, (1, LANES), 1)
    upper = lane >= HEAD_DIM
    qs = []
    for p in range(hg // 2):
        q2 = q_ref[:, p * LANES:(p + 1) * LANES]
        qs.append(jnp.where(upper, jnp.zeros_like(q2), q2))
        qs.append(jnp.where(upper, q2, jnp.zeros_like(q2)))
    return jnp.concatenate(qs, axis=0)


def _store_heads(o_ref, o_t, hg):
    for p in range(hg // 2):
        pair = jnp.concatenate([o_t[:, (2 * p) * QB:(2 * p + 1) * QB],
                                o_t[:, (2 * p + 1) * QB:(2 * p + 2) * QB]], axis=0)
        o_ref[:, p * LANES:(p + 1) * LANES] = pair.T.astype(o_ref.dtype)


def _band_kernel(*refs, mode, hg, chunk):
    if mode == "swa":
        q_ref, k_ref, vt_ref, tab_ref, sink_ref, o_ref = refs
    elif mode == "sel":
        q_ref, k_ref, vt_ref, tab_ref, sel_ref, e_ref, et_ref, o_ref, sa_ref, sb_ref, m_ref, acc_ref = refs
    else:
        q_ref, k_ref, vt_ref, tab_ref, o_ref = refs
    g = pl.program_id(1)
    i = pl.program_id(2)
    cols = hg * QB
    qstack = _stack_heads(q_ref, hg)
    eye = jnp.where(lax.broadcasted_iota(jnp.int32, (QB, QB), 0) == lax.broadcasted_iota(jnp.int32, (QB, QB), 1),
                    1.0, 0.0).astype(BF16)

    def k_block(blk):
        return k_ref[pl.ds(pl.multiple_of(blk * QB, QB), QB), :]

    def vt_rows(start, size):
        return jnp.concatenate([vt_ref[0, :, pl.ds(start, size)], jnp.ones((ONES_ROWS, size), BF16)], axis=0)

    def vt_block(blk):
        return vt_rows(pl.multiple_of(blk * QB, QB), QB)

    def with_table(table):
        return jnp.concatenate([qstack, table.astype(BF16)], axis=1)

    def grp(x, j):
        return x[j * GW:(j + 1) * GW]

    def lanes(x, j):
        return x[:, j * GW:(j + 1) * GW]

    GW = cols if mode == "sel" else 2 * QB
    n_grp = cols // GW
    ip = jnp.maximum(i - 1, 0)
    t_prev = tab_ref[0, 0]
    if mode == "sel":
        sel_q = sel_ref[0, 0]
        allowed = _dot(sel_q, e_ref[:, pl.ds(pl.multiple_of(ip * QB, QB), QB)])
        t_prev = jnp.where(jnp.concatenate([allowed] * hg, axis=0) > 0.5, t_prev, NEG_INF)
    t_prev = jnp.where(i > 0, t_prev, NEG_INF)
    q_prev = with_table(t_prev)
    q_cur = with_table(tab_ref[0, 1])
    k_prev = jnp.concatenate([k_block(ip), eye], axis=1)
    k_cur = jnp.concatenate([k_block(i), eye], axis=1)
    vt_near = jnp.concatenate([vt_block(ip), vt_block(i)], axis=1)

    if mode == "swa":
        head = lax.broadcasted_iota(jnp.int32, (1, cols), 1) // QB
        sink = jnp.zeros((1, cols), F32)
        for h in range(hg):
            sink = jnp.where(head == h, sink_ref[g * hg + h], sink)

    elif mode == "win":
        n_far = NSA_WINDOW // QB - 1
        backs = range(n_far + 1, 1, -1)
        onehot = jnp.concatenate([eye, jnp.zeros(((n_far - 1) * QB, QB), BF16)], axis=0)
        k_far = jnp.concatenate(
            [jnp.concatenate([k_block(jnp.maximum(i - bk, 0)) for bk in backs], axis=0), onehot], axis=1)
        vt_far = jnp.concatenate([vt_block(jnp.maximum(i - bk, 0)) for bk in backs], axis=1)
        q_tri = with_table(tab_ref[0, 2])
        in_seq = lax.broadcasted_iota(jnp.int32, (n_far * QB, 1), 0) >= (n_far + 1 - i) * QB

    else:
        blk = lax.broadcasted_iota(jnp.int32, (QB, LANES), 1)
        pick = jnp.logical_and(sel_q.astype(F32) > 0.5, blk < 2 * (i - 1))
        sel_bias = jnp.where(pick, 0.0, NEG_INF)
        q_far = with_table(jnp.concatenate([sel_bias] * hg, axis=0))
        per = chunk // QB
        n_chunks = (jnp.maximum(i - 1, 0) + per - 1) // per
        def far_scores(c):
            off = pl.multiple_of(c * chunk, chunk)
            k_rows = jnp.concatenate([k_ref[pl.ds(off, chunk), :], et_ref[pl.ds(off, chunk), :]], axis=1)
            return _dot_nt(k_rows, q_far)

        def sweep(c, cur_ref, nxt_ref):
            nxt_ref[...] = far_scores(jnp.minimum(c + 1, n_chunks - 1))
            sf = cur_ref[...]
            m_old = m_ref[...]
            m_new = jnp.maximum(m_old, jnp.max(sf, axis=0, keepdims=True))
            pf = jnp.exp2(sf - m_new).astype(BF16)
            acc_ref[...] = (jnp.exp2(m_old - m_new) * acc_ref[...]
                            + _dot(vt_rows(pl.multiple_of(c * chunk, chunk), chunk), pf))
            m_ref[...] = m_new

        def body(pair, carry):
            sweep(2 * pair, sa_ref, sb_ref)

            @pl.when(2 * pair + 1 < n_chunks)
            def _():
                sweep(2 * pair + 1, sb_ref, sa_ref)

            return carry

        m_ref[...] = jnp.full((1, cols), M_INIT, F32)
        acc_ref[...] = jnp.zeros((HEAD_DIM + ONES_ROWS, cols), F32)
        sa_ref[...] = far_scores(0)
        lax.fori_loop(0, (n_chunks + 1) // 2, body, 0)
        m_far = m_ref[...]
        acc_far = acc_ref[...]

    def near_scores(j):
        parts = [_dot_nt(k_prev, grp(q_prev, j)), _dot_nt(k_cur, grp(q_cur, j))]
        if mode == "win":
            parts.append(jnp.where(in_seq, _dot_nt(k_far, grp(q_tri, j)), NEG_INF))
        return parts

    def near_softmax(j, parts):
        m = jnp.max(parts[0], axis=0, keepdims=True)
        for s in parts[1:]:
            m = jnp.maximum(m, jnp.max(s, axis=0, keepdims=True))
        if mode == "swa":
            m = jnp.maximum(m, lanes(sink, j))
        if mode == "sel":
            m = jnp.maximum(m, lanes(m_far, j))
        return m, [jnp.exp2(s - m).astype(BF16) for s in parts]

    def near_output(j, m, probs):
        acc = _dot(vt_near, jnp.concatenate(probs[:2], axis=0))
        if mode == "win":
            acc = acc + _dot(vt_far, probs[2])
        if mode == "sel":
            acc = jnp.exp2(lanes(m_far, j) - m) * lanes(acc_far, j) + acc
        den = acc[HEAD_DIM:HEAD_DIM + 1]
        if mode == "swa":
            den = den + jnp.exp2(lanes(sink, j) - m)
        o_t = acc[:HEAD_DIM] * (1.0 / den)
        for pp in range(GW // (2 * QB)):
            pair = jnp.concatenate([o_t[:, 2 * pp * QB:(2 * pp + 1) * QB],
                                    o_t[:, (2 * pp + 1) * QB:(2 * pp + 2) * QB]], axis=0)
            c0 = (j * (GW // (2 * QB)) + pp) * LANES
            o_ref[:, c0:c0 + LANES] = pair.T.astype(o_ref.dtype)

    _pipeline3(n_grp, near_scores, near_softmax, near_output)


def _band_attention(proj, v_t, tabs, b, t, *, mode, hg, c_q, c_k, v_blk, sinks=None, sel=None, emats=None):
    nb = t // QB
    n_groups = 2
    qw = hg * HEAD_DIM
    chunk = min(512, t)
    in_specs = [
        pl.BlockSpec((QB, qw), lambda bi, g, i: (bi * nb + i, c_q // qw + g)),
        pl.BlockSpec((t, LANES), lambda bi, g, i: (bi, c_k // LANES + g)),
        pl.BlockSpec((1, HEAD_DIM, t), lambda bi, g, i: (bi, v_blk + g, 0)),
        pl.BlockSpec((1,) + tabs.shape[1:], lambda bi, g, i: (g, 0, 0, 0)),
    ]
    args = [proj, proj, v_t, tabs]
    scratch = []
    if mode == "swa":
        in_specs.append(pl.BlockSpec(memory_space=pltpu.SMEM))
        args.append(sinks.astype(F32) * LOG2E)
    if mode == "sel":
        emat, emat_t = emats
        in_specs.append(pl.BlockSpec((1, 1, QB, LANES), lambda bi, g, i: (bi, g, i, 0)))
        in_specs.append(pl.BlockSpec((LANES, t), lambda bi, g, i: (0, 0)))
        in_specs.append(pl.BlockSpec((t, LANES), lambda bi, g, i: (0, 0)))
        args += [sel, emat, emat_t]
        scratch = [pltpu.VMEM((chunk, hg * QB), F32), pltpu.VMEM((chunk, hg * QB), F32),
                   pltpu.VMEM((1, hg * QB), F32), pltpu.VMEM((HEAD_DIM + ONES_ROWS, hg * QB), F32)]
    return pl.pallas_call(
        functools.partial(_band_kernel, mode=mode, hg=hg, chunk=chunk),
        grid=(b, n_groups, nb),
        in_specs=in_specs,
        out_specs=pl.BlockSpec((QB, qw), lambda bi, g, i: (bi * nb + i, g)),
        out_shape=jax.ShapeDtypeStruct((b * t, n_groups * qw), BF16),
        scratch_shapes=scratch,
        compiler_params=_cparams(("parallel", "parallel", "arbitrary")),
        name="band_" + mode,
    )(*args)


def _compress_kernel(x_ref, pos_ref, w1_ref, b1_ref, w2_ref, b2_ref, o_ref):
    x = (x_ref[0].astype(F32) + pos_ref[0]).astype(BF16)
    hid = jax.nn.gelu(_dot(x, w1_ref[0]) + b1_ref[0])
    o_ref[0] = (_dot(hid.astype(BF16), w2_ref[0]) + b2_ref[0]).astype(o_ref.dtype)


def _compress(flat, pos, w1, b1, w2d, b2d, tr=256):
    _, r, cin = flat.shape
    return pl.pallas_call(
        _compress_kernel,
        grid=(2, r // tr),
        in_specs=[pl.BlockSpec((1, tr, cin), lambda s, i: (s, i, 0)),
                  pl.BlockSpec((1, 1, cin), lambda s, i: (s, 0, 0)),
                  pl.BlockSpec((1, cin, NSA_CMP_HIDDEN), lambda s, i: (s, 0, 0)),
                  pl.BlockSpec((1, 1, NSA_CMP_HIDDEN), lambda s, i: (s, 0, 0)),
                  pl.BlockSpec((1, NSA_CMP_HIDDEN, LANES), lambda s, i: (s, 0, 0)),
                  pl.BlockSpec((1, 1, LANES), lambda s, i: (s, 0, 0))],
        out_specs=pl.BlockSpec((1, tr, LANES), lambda s, i: (s, i, 0)),
        out_shape=jax.ShapeDtypeStruct((2, r, LANES), BF16),
        compiler_params=_cparams(("parallel", "parallel")),
        name="nsa_compress",
    )(flat, pos, w1, b1, w2d, b2d)


def _cmp_kernel(q_ref, kc_ref, vct_ref, ztab_ref, mmat_ref, o_ref, sel_ref, *, hg, ncp, ns, n_sel):
    i = pl.program_id(2)
    cols = hg * QB
    qstack = _stack_heads(q_ref, hg)
    s = _dot_nt(kc_ref[0, 0], qstack)
    start = pl.multiple_of(ncp - 8 * i, 8)
    delta = jnp.concatenate([ztab_ref[h, pl.ds(start, ncp), :] for h in range(hg)], axis=1)
    nn = lax.broadcasted_iota(jnp.int32, (ncp, QB), 0)
    qq = lax.broadcasted_iota(jnp.int32, (ncp, QB), 1)
    valid1 = (i * QB + qq - NSA_CMP_STRIDE * nn - (NSA_CMP_LEN - 1)) >= 0
    valid = jnp.concatenate([valid1] * hg, axis=1)
    s = jnp.where(valid, s + delta, NEG_INF)
    m = jnp.max(s, axis=0, keepdims=True)
    e = jnp.exp2(s - m)
    p = jnp.where(valid, e * (1.0 / jnp.sum(e, axis=0, keepdims=True)), 0.0)
    _store_heads(o_ref, _dot(vct_ref[0, 0], p.astype(BF16)), hg)

    imp = p[:, 0:QB]
    for h in range(1, hg):
        imp = imp + p[:, h * QB:(h + 1) * QB]
    hi, mid, lo = _split3(imp)
    mm = mmat_ref[...]
    imp_sel = _dot(mm, hi) + _dot(mm, mid) + _dot(mm, lo)
    sb = lax.broadcasted_iota(jnp.int32, (ns, QB), 0)
    tb = (i * QB + lax.broadcasted_iota(jnp.int32, (ns, QB), 1)) // NSA_SEL_LEN
    forced = jnp.logical_or(jnp.logical_or(sb == 0, sb == tb), sb == tb - 1)
    score = jnp.where(forced, FORCE_SCORE, jnp.where(sb <= tb, imp_sel, NEG_INF))
    sub = 8
    rows = [score[v * sub:(v + 1) * sub, :] for v in range(ns // sub)]
    ranks = [jnp.zeros((sub, QB), F32) for _ in rows]
    sub_idx = lax.broadcasted_iota(jnp.int32, (sub, QB), 0)
    for r in range(ns):
        other = score[r:r + 1, :]
        for v, mine in enumerate(rows):
            if v < r // sub:
                inc = jnp.where(other > mine, 1.0, 0.0)
            elif v > r // sub:
                inc = jnp.where(other >= mine, 1.0, 0.0)
            else:
                tie = jnp.where(sub_idx > r % sub, 1.0, 0.0)
                inc = jnp.where(other > mine, 1.0, jnp.where(other == mine, tie, 0.0))
            ranks[v] = ranks[v] + inc
    chosen = [jnp.where(rk < n_sel, 1.0, 0.0) for rk in ranks]
    chosen.append(jnp.zeros((LANES - ns, QB), F32))
    sel_ref[0, 0] = jnp.concatenate(chosen, axis=0).T.astype(sel_ref.dtype)


def _cmp_attention(proj, kc, vc_t, ztab, mmat_t, b, t):
    nb = t // QB
    hg = NSA_GROUP
    qw = hg * HEAD_DIM
    ncp = t // NSA_CMP_STRIDE
    ns = t // NSA_SEL_LEN
    n_sel = min(NSA_SEL_TOPK, ns)
    return pl.pallas_call(
        functools.partial(_cmp_kernel, hg=hg, ncp=ncp, ns=ns, n_sel=n_sel),
        grid=(b, NSA_KV_HEADS, nb),
        in_specs=[
            pl.BlockSpec((QB, qw), lambda bi, g, i: (bi * nb + i, C_NQ // qw + g)),
            pl.BlockSpec((1, 1, ncp, LANES), lambda bi, g, i: (bi, g, 0, 0)),
            pl.BlockSpec((1, 1, HEAD_DIM, ncp), lambda bi, g, i: (bi, g, 0, 0)),
            pl.BlockSpec((hg, 2 * ncp, QB), lambda bi, g, i: (g, 0, 0)),
            pl.BlockSpec((ns, ncp), lambda bi, g, i: (0, 0)),
        ],
        out_specs=[pl.BlockSpec((QB, qw), lambda bi, g, i: (bi * nb + i, g)),
                   pl.BlockSpec((1, 1, QB, LANES), lambda bi, g, i: (bi, g, i, 0))],
        out_shape=[jax.ShapeDtypeStruct((b * t, NSA_W), BF16),
                   jax.ShapeDtypeStruct((b, NSA_KV_HEADS, t, LANES), BF16)],
        compiler_params=_cparams(("parallel", "parallel", "arbitrary")),
        name="nsa_cmp_select",
    )(proj, kc, vc_t, ztab, mmat_t)


def _combine_kernel(oc_ref, os_ref, ow_ref, g_ref, o_ref):
    sg = jax.nn.sigmoid(g_ref[...])
    lane = lax.broadcasted_iota(jnp.int32, (1, LANES), 1)
    upper = lane >= HEAD_DIM
    for p in range(NSA_HEADS // 2):
        acc = None
        for br, ref in enumerate((oc_ref, os_ref, ow_ref)):
            c0 = G_NG + NSA_N_BRANCH * (2 * p) + br
            c1 = G_NG + NSA_N_BRANCH * (2 * p + 1) + br
            gate = jnp.where(upper, sg[:, c1:c1 + 1], sg[:, c0:c0 + 1])
            term = gate * ref[:, p * LANES:(p + 1) * LANES].astype(F32)
            acc = term if acc is None else acc + term
        o_ref[:, p * LANES:(p + 1) * LANES] = acc.astype(o_ref.dtype)


def _nsa_combine(o_cmp, o_slc, o_win, gates, tm=512):
    n = o_cmp.shape[0]
    spec = pl.BlockSpec((tm, NSA_W), lambda i: (i, 0))
    return pl.pallas_call(
        _combine_kernel,
        grid=(n // tm,),
        in_specs=[spec, spec, spec, pl.BlockSpec((tm, D_GATE), lambda i: (i, 0))],
        out_specs=spec,
        out_shape=jax.ShapeDtypeStruct((n, NSA_W), BF16),
        compiler_params=_cparams(("parallel",)),
        name="nsa_combine",
    )(o_cmp, o_slc, o_win, gates)


def _ffn_kernel(te_ref, nu_ref, *refs, fused_norm):
    if fused_norm:
        x_ref, g_ref, wg_ref, wu_ref, wd_ref, o_ref, acc_ref, xn_ref = refs
    else:
        x_ref, wg_ref, wu_ref, wd_ref, o_ref, acc_ref = refs
    del te_ref
    i = pl.program_id(0)
    k = pl.program_id(1)
    last = pl.num_programs(1) - 1
    used = i < nu_ref[0]

    @pl.when(jnp.logical_and(used, k == 0))
    def _():
        if fused_norm:
            h = x_ref[...]
            y = h * lax.rsqrt(jnp.mean(h * h, axis=-1, keepdims=True) + RMS_EPS)
            xn_ref[...] = (y * g_ref[...]).astype(BF16)
            acc_ref[...] = h
        else:
            acc_ref[...] = jnp.zeros_like(acc_ref)

    @pl.when(used)
    def _():
        x = xn_ref[...] if fused_norm else x_ref[...]
        gate = _dot(x, wg_ref[0].astype(BF16))
        up = _dot(x, wu_ref[0].astype(BF16))
        hid = (jax.nn.silu(gate) * up).astype(BF16)
        acc_ref[...] += _dot(hid, wd_ref[0].astype(BF16))

    @pl.when(jnp.logical_and(used, k == last))
    def _():
        o_ref[...] = acc_ref[...].astype(o_ref.dtype)

    @pl.when(jnp.logical_and(jnp.logical_not(used), k == last))
    def _():
        o_ref[...] = jnp.zeros_like(o_ref)


def _ffn(x, w_gate, w_up, w_down, tile_expert, n_used, out_dtype, gain=None, tm=512, tf=512):
    r, d = x.shape
    nk = D_FF // tf
    n_tiles = r // tm
    fused_norm = gain is not None

    def tile(i, nu):
        return jnp.minimum(i, nu[0] - 1)

    def kk(i, k, nu):
        return jnp.where(i < nu[0], k, nk - 1)

    in_specs = [pl.BlockSpec((tm, d), lambda i, k, te, nu: (tile(i, nu), 0))]
    args = [x]
    scratch = [pltpu.VMEM((tm, d), F32)]
    if fused_norm:
        in_specs.append(pl.BlockSpec((1, d), lambda i, k, te, nu: (0, 0)))
        args.append(gain.reshape(1, d).astype(F32))
        scratch.append(pltpu.VMEM((tm, d), BF16))
    in_specs += [
        pl.BlockSpec((1, d, tf), lambda i, k, te, nu: (te[tile(i, nu)], 0, kk(i, k, nu))),
        pl.BlockSpec((1, d, tf), lambda i, k, te, nu: (te[tile(i, nu)], 0, kk(i, k, nu))),
        pl.BlockSpec((1, tf, d), lambda i, k, te, nu: (te[tile(i, nu)], kk(i, k, nu), 0)),
    ]
    args += [w_gate, w_up, w_down]
    return pl.pallas_call(
        functools.partial(_ffn_kernel, fused_norm=fused_norm),
        grid_spec=pltpu.PrefetchScalarGridSpec(
            num_scalar_prefetch=2,
            grid=(n_tiles, nk),
            in_specs=in_specs,
            out_specs=pl.BlockSpec((tm, d), lambda i, k, te, nu: (i, 0)),
            scratch_shapes=scratch,
        ),
        out_shape=jax.ShapeDtypeStruct((r, d), out_dtype),
        compiler_params=_cparams(("arbitrary", "arbitrary")),
        name="swiglu_ffn",
    )(tile_expert, n_used, *args)


def _router_kernel(l_ref, o_ref):
    lane = lax.broadcasted_iota(jnp.int32, l_ref.shape, 1)
    lf = lane.astype(F32)
    lg = jnp.where(lane < N_EXPERTS, l_ref[...], REMOVED)
    v1 = jnp.max(lg, axis=1, keepdims=True)
    i1 = jnp.min(jnp.where(lg == v1, lf, float(LANES)), axis=1, keepdims=True)
    lg2 = jnp.where(lf == i1, REMOVED, lg)
    v2 = jnp.max(lg2, axis=1, keepdims=True)
    i2 = jnp.min(jnp.where(lg2 == v2, lf, float(LANES)), axis=1, keepdims=True)
    e2 = jnp.exp(v2 - v1)
    den = 1.0 + e2
    p1 = 1.0 / den
    p2 = e2 / den
    out = jnp.where(lane == N_EXPERTS, i1, 0.0)
    out = jnp.where(lane == N_EXPERTS + 1, i2, out)
    out = jnp.where(lane == N_EXPERTS + 2, p1, out)
    out = jnp.where(lane == N_EXPERTS + 3, p2, out)
    o_ref[...] = out


def _router_top2(logits, tm=512):
    n = logits.shape[0]
    spec = pl.BlockSpec((tm, LANES), lambda i: (i, 0))
    return pl.pallas_call(
        _router_kernel,
        grid=(n // tm,),
        in_specs=[spec],
        out_specs=spec,
        out_shape=jax.ShapeDtypeStruct((n, LANES), F32),
        compiler_params=_cparams(("parallel",)),
        name="moe_router_top2",
    )(logits)


def _moe_combine_kernel(h_ref, a_ref, b_ref, top_ref, g_ref, o_ref, *, final_norm):
    p0 = top_ref[:, N_EXPERTS + TOP_K:N_EXPERTS + TOP_K + 1]
    p1 = top_ref[:, N_EXPERTS + TOP_K + 1:N_EXPERTS + TOP_K + 2]
    y = h_ref[...] + p0 * a_ref[...].astype(F32) + p1 * b_ref[...].astype(F32)
    if final_norm:
        y = y * lax.rsqrt(jnp.mean(y * y, axis=-1, keepdims=True) + RMS_EPS) * g_ref[...]
    o_ref[...] = y


def _moe_combine(h, ya, yb, top, gain, final_norm, tm=512):
    n, d = h.shape
    row = pl.BlockSpec((tm, d), lambda i: (i, 0))
    return pl.pallas_call(
        functools.partial(_moe_combine_kernel, final_norm=final_norm),
        grid=(n // tm,),
        in_specs=[row, row, row, pl.BlockSpec((tm, LANES), lambda i: (i, 0)),
                  pl.BlockSpec((1, d), lambda i: (0, 0))],
        out_specs=row,
        out_shape=jax.ShapeDtypeStruct((n, d), F32),
        compiler_params=_cparams(("parallel",)),
        name="moe_combine",
    )(h, ya, yb, top, gain.reshape(1, d).astype(F32))


def _t5_bucket_np(dist):
    n = np.maximum(dist, 0)
    max_exact = REL_BUCKETS // 2
    nf = np.maximum(n, 1).astype(np.float32)
    large = max_exact + (np.log(nf / np.float32(max_exact)) / np.float32(math.log(REL_MAX_DIST / max_exact))
                         * np.float32(REL_BUCKETS - max_exact)).astype(np.int32)
    large = np.minimum(large, REL_BUCKETS - 1)
    return np.where(n < max_exact, n, large).astype(np.int32)


def _band_tables(rel_tab, t):
    q = np.arange(QB)[:, None]
    k = np.arange(QB)[None, :]
    rel = rel_tab.astype(F32) * LOG2E
    toep = jnp.take(rel, jnp.asarray(_t5_bucket_np((q - k) % QB)), axis=0)
    toep = toep.transpose(2, 0, 1)
    before = jnp.asarray(k > q)
    far = rel[REL_BUCKETS - 1, SWA_HEADS:]

    def tiles(tab, fill_prev, extra=()):
        heads = tab.shape[0]
        parts = [jnp.where(before, tab, fill_prev), jnp.where(before, NEG_INF, tab)]
        parts += [jnp.broadcast_to(e, tab.shape) for e in extra]
        return jnp.stack([p.reshape(2, heads // 2 * QB, QB) for p in parts], axis=1)

    tabs_swa = tiles(toep[:SWA_HEADS], NEG_INF)
    tabs_nsa = tiles(toep[SWA_HEADS:] - far[:, None, None], 0.0, extra=[jnp.where(before, 0.0, NEG_INF)])
    ncp = t // NSA_CMP_STRIDE
    m = np.arange(-9, 7)[:, None]
    qr = np.arange(QB)[None, :]
    d = qr - NSA_CMP_STRIDE * m - (NSA_CMP_LEN - 1)
    inband = (d >= 0) & (d < REL_MAX_DIST)
    vals = jnp.take(rel[:, SWA_HEADS:], jnp.asarray(_t5_bucket_np(np.clip(d, 0, None))), axis=0)
    vals = vals.transpose(2, 0, 1) - far[:, None, None]
    band = jnp.where(jnp.asarray(inband), vals, 0.0)
    ztab = jnp.pad(band, ((0, 0), (ncp - 9, ncp - 7), (0, 0)))
    return tabs_swa, tabs_nsa, ztab


def _selection_matrices(t):
    ncp = t // NSA_CMP_STRIDE
    ns = t // NSA_SEL_LEN
    per = NSA_SEL_LEN // NSA_CMP_STRIDE
    ratio = NSA_CMP_LEN // NSA_CMP_STRIDE
    mmat_t = np.zeros((ns, ncp), np.float32)
    for n in range(ncp - 1):
        for j in range(ratio):
            mmat_t[(n + j) // per, n] += 1.0
    emat_t = (np.arange(t)[:, None] // NSA_SEL_LEN == np.arange(LANES)[None, :]).astype(np.float32)
    return jnp.asarray(mmat_t, BF16), (jnp.asarray(emat_t.T, BF16), jnp.asarray(emat_t, BF16))


def _dup(w):
    d = w.shape[0]
    w = w.reshape(d, -1, 1, HEAD_DIM)
    return jnp.broadcast_to(w, (d, w.shape[1], 2, HEAD_DIM)).reshape(d, -1)


def _prep_in_weights(w_in_l):
    sizes = (FOX_W, FOX_W, FOX_W, FOX_HEADS, SWA_W, SWA_KV_W, SWA_KV_W,
             NSA_W, NSA_KV_W, NSA_KV_W, NSA_KV_W, NSA_KV_W, NSA_KV_W, NSA_KV_W, NSA_HEADS * NSA_N_BRANCH)
    splits = [int(s) for s in np.cumsum(sizes)[:-1]]
    (fq, fk, fv, ff, sq, sk, sv, nq, nkc, nvc, nks, nvs, nkw, nvw, ng) = jnp.split(w_in_l, splits, axis=-1)
    scale = HEAD_DIM ** -0.5 * LOG2E
    w_proj = jnp.concatenate([fq * scale, fk, fv, sq * scale, nq * scale, _dup(sk), _dup(nks), _dup(nkw),
                              sv, nvs, nvw, nkc, nvc], axis=-1).astype(BF16)
    pad = jnp.zeros((w_in_l.shape[0], D_GATE - FOX_HEADS - NSA_HEADS * NSA_N_BRANCH), w_in_l.dtype)
    w_gate = jnp.concatenate([ff, ng, pad], axis=-1).astype(BF16)
    return w_proj, w_gate


def _compress_inputs(proj, b, t):
    g = NSA_KV_HEADS
    flats = []
    for c0 in (C_NKC, C_NVC):
        x = proj[:, c0:c0 + NSA_KV_W].reshape(b, t, g, HEAD_DIM).transpose(0, 2, 1, 3)
        width = NSA_CMP_LEN * HEAD_DIM
        even = x.reshape(b, g, t // NSA_CMP_LEN, width)
        odd = x[:, :, NSA_CMP_STRIDE:t - NSA_CMP_STRIDE].reshape(b, g, t // NSA_CMP_LEN - 1, width)
        odd = jnp.pad(odd, ((0, 0), (0, 0), (0, 1), (0, 0)))
        flats.append(jnp.stack([even, odd], axis=3).reshape(b * g * (t // NSA_CMP_STRIDE), width))
    return jnp.stack(flats, axis=0)


def _mixer(h, gain, b, t, w_in_l, forget_bias, sinks, cmp_pos, cmp_w1, cmp_b1, cmp_w2, cmp_b2, tables):
    tabs_swa, tabs_nsa, ztab, mmat_t, emats = tables
    w_proj, w_gate = _prep_in_weights(w_in_l)
    proj, gates = _in_proj(h, gain, w_proj, w_gate)
    fv_t = proj[:, C_FV:C_FV + FOX_W].reshape(b, t, FOX_W).transpose(0, 2, 1)
    v_t = proj[:, C_V:C_V + 3 * NSA_KV_W].reshape(b, t, 3 * NSA_KV_W).transpose(0, 2, 1)

    qx, kx = _fox_decay_operands(gates, forget_bias, b, t)
    o_fox = _fox_attention(proj, fv_t, qx, kx, b, t)

    o_swa = _band_attention(proj, v_t, tabs_swa, b, t, mode="swa", hg=SWA_GROUP,
                            c_q=C_SQ, c_k=C_SK, v_blk=0, sinks=sinks)

    flat = _compress_inputs(proj, b, t)
    pos = cmp_pos.reshape(2, 1, NSA_CMP_LEN * HEAD_DIM).astype(F32)
    w2d = jnp.concatenate([cmp_w2, cmp_w2], axis=-1).astype(BF16)
    b2d = jnp.concatenate([cmp_b2, cmp_b2], axis=-1).reshape(2, 1, LANES).astype(F32)
    kvc = _compress(flat, pos, cmp_w1.astype(BF16), cmp_b1.reshape(2, 1, NSA_CMP_HIDDEN).astype(F32), w2d, b2d)
    ncp = t // NSA_CMP_STRIDE
    kvc = kvc.reshape(2, b, NSA_KV_HEADS, ncp, LANES)
    vc_t = kvc[1, :, :, :, :HEAD_DIM].transpose(0, 1, 3, 2)
    o_cmp, sel = _cmp_attention(proj, kvc[0], vc_t, ztab, mmat_t, b, t)
    o_slc = _band_attention(proj, v_t, tabs_nsa, b, t, mode="sel", hg=NSA_GROUP,
                            c_q=C_NQ, c_k=C_NKS, v_blk=2, sel=sel, emats=emats)
    o_win = _band_attention(proj, v_t, tabs_nsa, b, t, mode="win", hg=NSA_GROUP,
                            c_q=C_NQ, c_k=C_NKW, v_blk=4)
    o_nsa = _nsa_combine(o_cmp, o_slc, o_win, gates)
    return o_fox, o_swa, o_nsa


def _moe(hn, h, router, w_gate, w_up, w_down, gain, final_norm, tm=1024, tf=256):
    n, d = hn.shape
    w_r = jnp.zeros((d, LANES), BF16).at[:, :N_EXPERTS].set(router.astype(BF16))
    logits = _matmul([hn], [w_r], F32, tn=LANES, name="router_logits")
    top = _router_top2(logits)
    e_idx = top[:, N_EXPERTS:N_EXPERTS + TOP_K].astype(jnp.int32)
    e_flat = e_idx.reshape(-1)
    onehot = (e_flat[:, None] == jnp.arange(N_EXPERTS)[None, :]).astype(jnp.int32)
    csum = jnp.cumsum(onehot, axis=0)
    counts = csum[-1]
    rank = jnp.take_along_axis(csum, e_flat[:, None], axis=1)[:, 0] - 1
    padded = ((counts + tm - 1) // tm) * tm
    ends = jnp.cumsum(padded)
    starts = ends - padded
    dest = starts[e_flat] + rank
    r_pad = n * TOP_K + N_EXPERTS * tm
    src_tok = jnp.zeros((r_pad,), jnp.int32).at[dest].set(jnp.arange(n * TOP_K, dtype=jnp.int32) // TOP_K)
    tile_start = jnp.arange(r_pad // tm, dtype=jnp.int32) * tm
    tile_expert = jnp.minimum(jnp.sum(tile_start[:, None] >= ends[None, :], axis=1), N_EXPERTS - 1).astype(jnp.int32)
    n_used = (ends[-1:] // tm).astype(jnp.int32)
    xs = hn.at[src_tok].get(mode="promise_in_bounds")
    y = _ffn(xs, w_gate, w_up, w_down, tile_expert, n_used, BF16, tm=tm, tf=tf)
    dest = dest.reshape(n, TOP_K)
    ya = y.at[dest[:, 0]].get(mode="promise_in_bounds")
    yb = y.at[dest[:, 1]].get(mode="promise_in_bounds")
    return _moe_combine(h, ya, yb, top, gain, final_norm)


def kernel(x, attn_norm, w_in, fox_forget_bias, swa_sinks, nsa_cmp_pos, nsa_cmp_w1, nsa_cmp_b1, nsa_cmp_w2,
           nsa_cmp_b2, w_out, rel_bias_table, ffn_norm, dense_w_gate, dense_w_up, dense_w_down, moe_router,
           moe_w_gate, moe_w_up, moe_w_down, final_norm):
    b, t, d = x.shape
    n = b * t
    depth = w_in.shape[0]
    tables = _band_tables(rel_bias_table, t) + _selection_matrices(t)
    h = x.reshape(n, d)
    for layer in range(depth):
        o_fox, o_swa, o_nsa = _mixer(h, attn_norm[layer], b, t, w_in[layer], fox_forget_bias[layer],
                                     swa_sinks[layer], nsa_cmp_pos[layer], nsa_cmp_w1[layer], nsa_cmp_b1[layer],
                                     nsa_cmp_w2[layer], nsa_cmp_b2[layer], tables)
        wo = w_out[layer].astype(BF16)
        h = _matmul([o_fox, o_swa, o_nsa], [wo[:FOX_W], wo[FOX_W:FOX_W + SWA_W], wo[FOX_W + SWA_W:]],
                    F32, residual=h, tn=1024, name="out_proj")
        i = layer // 2
        last = layer == depth - 1
        if layer % 2 == 0:
            tm = 512
            zeros = jnp.zeros((n // tm,), jnp.int32)
            h = _ffn(h, dense_w_gate[i][None].astype(BF16), dense_w_up[i][None].astype(BF16),
                     dense_w_down[i][None].astype(BF16), zeros, jnp.full((1,), n // tm, jnp.int32),
                     F32, gain=ffn_norm[layer], tm=tm)
            if last:
                h = _rmsnorm(h, final_norm, F32)
        else:
            hn = _rmsnorm(h, ffn_norm[layer], BF16)
            h = _moe(hn, h, moe_router[i], moe_w_gate[i], moe_w_up[i], moe_w_down[i], final_norm, last)
    return h.reshape(b, t, d)
```

```python
import functools
import math

import numpy as np
import jax
import jax.numpy as jnp
from jax import lax
from jax.experimental import pallas as pl
from jax.experimental.pallas import tpu as pltpu

F32 = jnp.float32
BF16 = jnp.bfloat16

D_MODEL = 2048
HEAD_DIM = 64
FOX_HEADS = 8
FOX_W = FOX_HEADS * HEAD_DIM
SWA_HEADS = 8
SWA_KV_HEADS = 2
SWA_GROUP = SWA_HEADS // SWA_KV_HEADS
SWA_W = SWA_HEADS * HEAD_DIM
SWA_KV_W = SWA_KV_HEADS * HEAD_DIM
SWA_WINDOW = 128
NSA_HEADS = 16
NSA_KV_HEADS = 2
NSA_GROUP = NSA_HEADS // NSA_KV_HEADS
NSA_W = NSA_HEADS * HEAD_DIM
NSA_KV_W = NSA_KV_HEADS * HEAD_DIM
NSA_CMP_LEN = 32
NSA_CMP_STRIDE = 16
NSA_CMP_HIDDEN = 256
NSA_SEL_LEN = 64
NSA_SEL_TOPK = 16
NSA_WINDOW = 512
NSA_N_BRANCH = 3
FORCE_SCORE = 1e9
REL_BUCKETS = 32
REL_MAX_DIST = 128
D_FF = 5632
N_EXPERTS = 8
TOP_K = 2
RMS_EPS = 1e-6
NEG_INF = -1e30
M_INIT = -1e29
REMOVED = -3e38
LOG2E = 1.4426950408889634
ONES_ROWS = 16
MOE_CHUNKS = 4

LANES = 128
QB = 128
VMEM_LIMIT = 56 * 1024 * 1024

C_FQ, C_FK, C_FV = 0, 512, 1024
C_SQ = 1536
C_NQ = 2048
C_SK, C_NKS, C_NKW = 3072, 3328, 3584
C_V = 3840
C_NKC, C_NVC = 4224, 4352
D_PROJ = 4608
PROJ_TN = 768
G_NG = FOX_HEADS
D_GATE = 128


def _cparams(sem, vmem=VMEM_LIMIT):
    return pltpu.CompilerParams(dimension_semantics=sem, vmem_limit_bytes=vmem)


def _dot(a, b):
    return jnp.dot(a, b, preferred_element_type=F32)


def _dot_nt(a, b):
    return lax.dot_general(a, b, (((1,), (1,)), ((), ())), preferred_element_type=F32)


def _split3(x):
    hi = x.astype(BF16)
    r1 = x - hi.astype(F32)
    mid = r1.astype(BF16)
    lo = (r1 - mid.astype(F32)).astype(BF16)
    return hi, mid, lo


def _pipeline3(n, stage1, stage2, stage3):
    first, second, outs = {}, {}, []
    for step in range(n + 2):
        if step < n:
            first[step] = stage1(step)
        if 0 <= step - 1 < n:
            second[step - 1] = stage2(step - 1, first.pop(step - 1))
        if 0 <= step - 2 < n:
            outs.append(stage3(step - 2, *second.pop(step - 2)))
    return outs


def _rmsnorm_kernel(x_ref, g_ref, o_ref):
    x = x_ref[...]
    y = x * lax.rsqrt(jnp.mean(x * x, axis=-1, keepdims=True) + RMS_EPS)
    o_ref[...] = (y * g_ref[...]).astype(o_ref.dtype)


def _rmsnorm(h, g, out_dtype, tm=512):
    n, d = h.shape
    return pl.pallas_call(
        _rmsnorm_kernel,
        grid=(n // tm,),
        in_specs=[pl.BlockSpec((tm, d), lambda i: (i, 0)),
                  pl.BlockSpec((1, d), lambda i: (0, 0))],
        out_specs=pl.BlockSpec((tm, d), lambda i: (i, 0)),
        out_shape=jax.ShapeDtypeStruct((n, d), out_dtype),
        compiler_params=_cparams(("parallel",)),
        name="rmsnorm",
    )(h, g.reshape(1, d).astype(F32))


def _mm_kernel(*refs, n_in, has_res):
    o_ref = refs[-1]
    acc = None
    for a in range(n_in):
        d = _dot(refs[a][...], refs[n_in + a][...])
        acc = d if acc is None else acc + d
    if has_res:
        acc = acc + refs[2 * n_in][...]
    o_ref[...] = acc.astype(o_ref.dtype)


def _matmul(xs, ws, out_dtype, residual=None, tm=1024, tn=256, name="matmul"):
    n = xs[0].shape[0]
    m = ws[0].shape[1]
    tm = min(tm, n)
    tn = min(tn, m)
    in_specs = [pl.BlockSpec((tm, x.shape[1]), lambda i, j: (i, 0)) for x in xs]
    in_specs += [pl.BlockSpec((w.shape[0], tn), lambda i, j: (0, j)) for w in ws]
    args = list(xs) + list(ws)
    if residual is not None:
        in_specs.append(pl.BlockSpec((tm, tn), lambda i, j: (i, j)))
        args.append(residual)
    return pl.pallas_call(
        functools.partial(_mm_kernel, n_in=len(xs), has_res=residual is not None),
        grid=(n // tm, m // tn),
        in_specs=in_specs,
        out_specs=pl.BlockSpec((tm, tn), lambda i, j: (i, j)),
        out_shape=jax.ShapeDtypeStruct((n, m), out_dtype),
        compiler_params=_cparams(("parallel", "arbitrary")),
        name=name,
    )(*args)


def _in_proj_kernel(h_ref, g_ref, w_ref, wgate_ref, proj_ref, gates_ref, xn_ref):
    @pl.when(pl.program_id(1) == 0)
    def _():
        h = h_ref[...]
        y = h * lax.rsqrt(jnp.mean(h * h, axis=-1, keepdims=True) + RMS_EPS)
        xn_ref[...] = (y * g_ref[...]).astype(BF16)
        gates_ref[...] = _dot(xn_ref[...], wgate_ref[...])

    proj_ref[...] = _dot(xn_ref[...], w_ref[...]).astype(proj_ref.dtype)


def _in_proj(h, gain, w_proj, w_gate, tm=1024, tn=PROJ_TN):
    n, d = h.shape
    return pl.pallas_call(
        _in_proj_kernel,
        grid=(n // tm, D_PROJ // tn),
        in_specs=[pl.BlockSpec((tm, d), lambda i, j: (i, 0)),
                  pl.BlockSpec((1, d), lambda i, j: (0, 0)),
                  pl.BlockSpec((d, tn), lambda i, j: (0, j)),
                  pl.BlockSpec((d, D_GATE), lambda i, j: (0, 0))],
        out_specs=[pl.BlockSpec((tm, tn), lambda i, j: (i, j)),
                   pl.BlockSpec((tm, D_GATE), lambda i, j: (i, 0))],
        out_shape=[jax.ShapeDtypeStruct((n, D_PROJ), BF16), jax.ShapeDtypeStruct((n, D_GATE), F32)],
        scratch_shapes=[pltpu.VMEM((tm, d), BF16)],
        compiler_params=_cparams(("parallel", "arbitrary")),
        name="in_proj",
    )(h, gain.reshape(1, d).astype(F32), w_proj, w_gate)


def _cumsum_kernel(g_ref, b_ref, pq_ref, pk_ref, oq_ref, ok_ref, qx_ref, kx_ref, carry_ref, *, tc):
    @pl.when(pl.program_id(1) == 0)
    def _():
        carry_ref[...] = jnp.zeros_like(carry_ref)

    z = g_ref[...] + b_ref[...]
    log_f = jnp.minimum(z, 0.0) - jnp.log1p(jnp.exp(-jnp.abs(z)))
    row = lax.broadcasted_iota(jnp.int32, (tc, tc), 0)
    col = lax.broadcasted_iota(jnp.int32, (tc, tc), 1)
    tri = jnp.where(col <= row, 1.0, 0.0).astype(BF16)
    hi, mid, lo = _split3(log_f)
    c = _dot(tri, hi) + _dot(tri, mid) + _dot(tri, lo) + carry_ref[...]
    carry_ref[...] = c[tc - 1:tc, :]
    parts = jnp.concatenate(_split3(c * LOG2E), axis=1)
    qx_ref[...] = (_dot(parts, pq_ref[...]) + oq_ref[...]).astype(qx_ref.dtype)
    kx_ref[...] = (_dot(parts, pk_ref[...]) + ok_ref[...]).astype(kx_ref.dtype)


def _fox_decay_operands(gates, forget_bias, b, t, tc=256):
    nt = t // tc
    wide = FOX_HEADS * LANES
    bias = jnp.zeros((1, D_GATE), F32).at[0, :FOX_HEADS].set(forget_bias.astype(F32))
    pq = np.zeros((3 * D_GATE, wide), np.float32)
    pk = np.zeros((3 * D_GATE, wide), np.float32)
    oq = np.zeros((1, wide), np.float32)
    ok = np.zeros((1, wide), np.float32)
    for h in range(FOX_HEADS):
        for part in range(3):
            pq[part * D_GATE + h, h * LANES + part] = 1.0
            pk[part * D_GATE + h, h * LANES + 3 + part] = -1.0
            oq[0, h * LANES + 3 + part] = 1.0
            ok[0, h * LANES + part] = 1.0
    const = lambda shape: pl.BlockSpec(shape, lambda bi, ti: (0, 0))
    out_spec = pl.BlockSpec((tc, wide), lambda bi, ti: (bi * nt + ti, 0))
    return pl.pallas_call(
        functools.partial(_cumsum_kernel, tc=tc),
        grid=(b, nt),
        in_specs=[pl.BlockSpec((tc, D_GATE), lambda bi, ti: (bi * nt + ti, 0)),
                  const((1, D_GATE)), const((3 * D_GATE, wide)), const((3 * D_GATE, wide)),
                  const((1, wide)), const((1, wide))],
        out_specs=[out_spec, out_spec],
        out_shape=[jax.ShapeDtypeStruct((b * t, wide), BF16), jax.ShapeDtypeStruct((b * t, wide), BF16)],
        scratch_shapes=[pltpu.VMEM((1, D_GATE), F32)],
        compiler_params=_cparams(("parallel", "arbitrary")),
        name="fox_cumsum",
    )(gates, bias, jnp.asarray(pq, BF16), jnp.asarray(pk, BF16), jnp.asarray(oq), jnp.asarray(ok))


def _fox_kernel(q_ref, k_ref, vt_ref, qx_ref, kx_ref, o_ref, m_sc, acc_sc, *, tq):
    i = pl.program_id(1)
    j = pl.program_id(2)
    lane = lax.broadcasted_iota(jnp.int32, (1, LANES), 1)
    upper = lane >= HEAD_DIM

    @pl.when(j == 0)
    def _():
        m_sc[...] = jnp.full_like(m_sc, M_INIT)
        acc_sc[...] = jnp.zeros_like(acc_sc)

    def step(diagonal):
        if diagonal:
            key = lax.broadcasted_iota(jnp.int32, (tq, tq), 0)
            qry = lax.broadcasted_iota(jnp.int32, (tq, tq), 1)
            causal = key <= qry
        ones = jnp.ones((ONES_ROWS, tq), BF16)

        def qk(h):
            p, a = divmod(h, 2)
            q2 = q_ref[:, p * LANES:(p + 1) * LANES]
            k2 = k_ref[:, p * LANES:(p + 1) * LANES]
            zero = jnp.zeros_like(q2)
            qa = jnp.where(upper, q2, zero) if a else jnp.where(upper, zero, q2)
            q_aug = jnp.concatenate([qa, qx_ref[:, h * LANES:(h + 1) * LANES]], axis=1)
            k_aug = jnp.concatenate([k2, kx_ref[:, h * LANES:(h + 1) * LANES]], axis=1)
            s = _dot_nt(k_aug, q_aug)
            if diagonal:
                s = jnp.where(causal, s, NEG_INF)
            return s, jnp.max(s, axis=0, keepdims=True)

        def softmax(h, scored):
            s, s_max = scored
            m_prev = m_sc[h]
            m_new = jnp.maximum(m_prev, s_max)
            m_sc[h] = m_new
            return jnp.exp2(m_prev - m_new), jnp.exp2(s - m_new).astype(BF16)

        def pv(h, alpha, pr):
            vt = jnp.concatenate([vt_ref[0, h * HEAD_DIM:(h + 1) * HEAD_DIM, :], ones], axis=0)
            acc_sc[h] = alpha * acc_sc[h] + _dot(vt, pr)

        _pipeline3(FOX_HEADS, qk, softmax, pv)

    @pl.when(j < i)
    def _():
        step(False)

    @pl.when(j == i)
    def _():
        step(True)
        for p in range(FOX_HEADS // 2):
            outs = []
            for h in (2 * p, 2 * p + 1):
                acc = acc_sc[h]
                outs.append(acc[:HEAD_DIM] * (1.0 / acc[HEAD_DIM:HEAD_DIM + 1]))
            o_ref[:, p * LANES:(p + 1) * LANES] = jnp.concatenate(outs, axis=0).T.astype(o_ref.dtype)


def _fox_attention(proj, v_t, qx, kx, b, t, tq=512):
    nt = t // tq
    wide = FOX_HEADS * LANES
    return pl.pallas_call(
        functools.partial(_fox_kernel, tq=tq),
        grid=(b, nt, nt),
        in_specs=[
            pl.BlockSpec((tq, FOX_W), lambda bi, i, j: (bi * nt + i, C_FQ // FOX_W)),
            pl.BlockSpec((tq, FOX_W), lambda bi, i, j: (bi * nt + jnp.minimum(j, i), C_FK // FOX_W)),
            pl.BlockSpec((1, FOX_W, tq), lambda bi, i, j: (bi, 0, jnp.minimum(j, i))),
            pl.BlockSpec((tq, wide), lambda bi, i, j: (bi * nt + i, 0)),
            pl.BlockSpec((tq, wide), lambda bi, i, j: (bi * nt + jnp.minimum(j, i), 0)),
        ],
        out_specs=pl.BlockSpec((tq, FOX_W), lambda bi, i, j: (bi * nt + i, 0)),
        out_shape=jax.ShapeDtypeStruct((b * t, FOX_W), BF16),
        scratch_shapes=[pltpu.VMEM((FOX_HEADS, 1, tq), F32),
                        pltpu.VMEM((FOX_HEADS, HEAD_DIM + ONES_ROWS, tq), F32)],
        compiler_params=_cparams(("parallel", "parallel", "arbitrary")),
        name="fox_attention",
    )(proj, proj, v_t, qx, kx)


def _stack_heads(q_ref, hg):
    lane = lax.broadcasted_iota(jnp.int32, (1, LANES), 1)
    upper = lane >= HEAD_DIM
    qs = []
    for p in range(hg // 2):
        q2 = q_ref[:, p * LANES:(p + 1) * LANES]
        qs.append(jnp.where(upper, jnp.zeros_like(q2), q2))
        qs.append(jnp.where(upper, q2, jnp.zeros_like(q2)))
    return jnp.concatenate(qs, axis=0)


def _store_heads(o_ref, o_t, hg):
    for p in range(hg // 2):
        pair = jnp.concatenate([o_t[:, (2 * p) * QB:(2 * p + 1) * QB],
                                o_t[:, (2 * p + 1) * QB:(2 * p + 2) * QB]], axis=0)
        o_ref[:, p * LANES:(p + 1) * LANES] = pair.T.astype(o_ref.dtype)


def _band_kernel(*refs, mode, hg, chunk):
    if mode == "swa":
        q_ref, k_ref, vt_ref, tab_ref, sink_ref, o_ref = refs
    elif mode == "sel":
        (q_ref, k_ref, vt_ref, tab_ref, sel_ref, e_ref, et_ref, o_ref,
         sa_ref, sb_ref, ma_ref, mb_ref, m_ref, acc_ref) = refs
    else:
        q_ref, k_ref, vt_ref, tab_ref, o_ref = refs
    g = pl.program_id(1)
    i = pl.program_id(2)
    cols = hg * QB
    qstack = _stack_heads(q_ref, hg)
    eye = jnp.where(lax.broadcasted_iota(jnp.int32, (QB, QB), 0) == lax.broadcasted_iota(jnp.int32, (QB, QB), 1),
                    1.0, 0.0).astype(BF16)

    def k_block(blk):
        return k_ref[pl.ds(pl.multiple_of(blk * QB, QB), QB), :]

    def vt_rows(start, size):
        return jnp.concatenate([vt_ref[0, :, pl.ds(start, size)], jnp.ones((ONES_ROWS, size), BF16)], axis=0)

    def vt_block(blk):
        return vt_rows(pl.multiple_of(blk * QB, QB), QB)

    def with_table(table):
        return jnp.concatenate([qstack, table.astype(BF16)], axis=1)

    def grp(x, j):
        return x[j * GW:(j + 1) * GW]

    def lanes(x, j):
        return x[:, j * GW:(j + 1) * GW]

    GW = cols if mode == "sel" else 2 * QB
    n_grp = cols // GW
    ip = jnp.maximum(i - 1, 0)
    t_prev = tab_ref[0, 0]
    if mode == "sel":
        sel_q = sel_ref[0, 0]
        allowed = _dot(sel_q, e_ref[:, pl.ds(pl.multiple_of(ip * QB, QB), QB)])
        t_prev = jnp.where(jnp.concatenate([allowed] * hg, axis=0) > 0.5, t_prev, NEG_INF)
    t_prev = jnp.where(i > 0, t_prev, NEG_INF)
    q_prev = with_table(t_prev)
    q_cur = with_table(tab_ref[0, 1])
    k_prev = jnp.concatenate([k_block(ip), eye], axis=1)
    k_cur = jnp.concatenate([k_block(i), eye], axis=1)
    vt_near = jnp.concatenate([vt_block(ip), vt_block(i)], axis=1)

    if mode == "swa":
        head = lax.broadcasted_iota(jnp.int32, (1, cols), 1) // QB
        sink = jnp.zeros((1, cols), F32)
        for h in range(hg):
            sink = jnp.where(head == h, sink_ref[g * hg + h], sink)

    elif mode == "win":
        n_far = NSA_WINDOW // QB - 1
        backs = range(n_far + 1, 1, -1)
        onehot = jnp.concatenate([eye, jnp.zeros(((n_far - 1) * QB, QB), BF16)], axis=0)
        k_far = jnp.concatenate(
            [jnp.concatenate([k_block(jnp.maximum(i - bk, 0)) for bk in backs], axis=0), onehot], axis=1)
        vt_far = jnp.concatenate([vt_block(jnp.maximum(i - bk, 0)) for bk in backs], axis=1)
        q_tri = with_table(tab_ref[0, 2])
        in_seq = lax.broadcasted_iota(jnp.int32, (n_far * QB, 1), 0) >= (n_far + 1 - i) * QB

    else:
        blk = lax.broadcasted_iota(jnp.int32, (QB, LANES), 1)
        pick = jnp.logical_and(sel_q.astype(F32) > 0.5, blk < 2 * (i - 1))
        sel_bias = jnp.where(pick, 0.0, NEG_INF)
        q_far = with_table(jnp.concatenate([sel_bias] * hg, axis=0))
        per = chunk // QB
        n_chunks = (jnp.maximum(i - 1, 0) + per - 1) // per
        def far_scores(c):
            off = pl.multiple_of(c * chunk, chunk)
            k_rows = jnp.concatenate([k_ref[pl.ds(off, chunk), :], et_ref[pl.ds(off, chunk), :]], axis=1)
            return _dot_nt(k_rows, q_far)

        def put_scores(c, s_ref, smax_ref):
            s = far_scores(c)
            s_ref[...] = s
            smax_ref[...] = jnp.max(s, axis=0, keepdims=True)

        def sweep(c, cur, nxt):
            put_scores(jnp.minimum(c + 1, n_chunks - 1), *nxt)
            s_ref, smax_ref = cur
            m_old = m_ref[...]
            m_new = jnp.maximum(m_old, smax_ref[...])
            pf = jnp.exp2(s_ref[...] - m_new).astype(BF16)
            acc_ref[...] = (jnp.exp2(m_old - m_new) * acc_ref[...]
                            + _dot(vt_rows(pl.multiple_of(c * chunk, chunk), chunk), pf))
            m_ref[...] = m_new

        buf_a, buf_b = (sa_ref, ma_ref), (sb_ref, mb_ref)

        def body(pair, carry):
            sweep(2 * pair, buf_a, buf_b)

            @pl.when(2 * pair + 1 < n_chunks)
            def _():
                sweep(2 * pair + 1, buf_b, buf_a)

            return carry

        m_ref[...] = jnp.full((1, cols), M_INIT, F32)
        acc_ref[...] = jnp.zeros((HEAD_DIM + ONES_ROWS, cols), F32)
        put_scores(0, *buf_a)
        lax.fori_loop(0, (n_chunks + 1) // 2, body, 0)
        m_far = m_ref[...]
        acc_far = acc_ref[...]

    def near_scores(j):
        parts = [_dot_nt(k_prev, grp(q_prev, j)), _dot_nt(k_cur, grp(q_cur, j))]
        if mode == "win":
            parts.append(jnp.where(in_seq, _dot_nt(k_far, grp(q_tri, j)), NEG_INF))
        return parts

    def near_softmax(j, parts):
        m = jnp.max(parts[0], axis=0, keepdims=True)
        for s in parts[1:]:
            m = jnp.maximum(m, jnp.max(s, axis=0, keepdims=True))
        if mode == "swa":
            m = jnp.maximum(m, lanes(sink, j))
        if mode == "sel":
            m = jnp.maximum(m, lanes(m_far, j))
        return m, [jnp.exp2(s - m).astype(BF16) for s in parts]

    def near_output(j, m, probs):
        acc = _dot(vt_near, jnp.concatenate(probs[:2], axis=0))
        if mode == "win":
            acc = acc + _dot(vt_far, probs[2])
        if mode == "sel":
            acc = jnp.exp2(lanes(m_far, j) - m) * lanes(acc_far, j) + acc
        den = acc[HEAD_DIM:HEAD_DIM + 1]
        if mode == "swa":
            den = den + jnp.exp2(lanes(sink, j) - m)
        o_t = acc[:HEAD_DIM] * (1.0 / den)
        for pp in range(GW // (2 * QB)):
            pair = jnp.concatenate([o_t[:, 2 * pp * QB:(2 * pp + 1) * QB],
                                    o_t[:, (2 * pp + 1) * QB:(2 * pp + 2) * QB]], axis=0)
            c0 = (j * (GW // (2 * QB)) + pp) * LANES
            o_ref[:, c0:c0 + LANES] = pair.T.astype(o_ref.dtype)

    _pipeline3(n_grp, near_scores, near_softmax, near_output)


def _band_attention(proj, v_t, tabs, b, t, *, mode, hg, c_q, c_k, v_blk, sinks=None, sel=None, emats=None):
    nb = t // QB
    n_groups = 2
    qw = hg * HEAD_DIM
    chunk = min(512, t)
    in_specs = [
        pl.BlockSpec((QB, qw), lambda bi, g, i: (bi * nb + i, c_q // qw + g)),
        pl.BlockSpec((t, LANES), lambda bi, g, i: (bi, c_k // LANES + g)),
        pl.BlockSpec((1, HEAD_DIM, t), lambda bi, g, i: (bi, v_blk + g, 0)),
        pl.BlockSpec((1,) + tabs.shape[1:], lambda bi, g, i: (g, 0, 0, 0)),
    ]
    args = [proj, proj, v_t, tabs]
    scratch = []
    if mode == "swa":
        in_specs.append(pl.BlockSpec(memory_space=pltpu.SMEM))
        args.append(sinks.astype(F32) * LOG2E)
    if mode == "sel":
        emat, emat_t = emats
        in_specs.append(pl.BlockSpec((1, 1, QB, LANES), lambda bi, g, i: (bi, g, i, 0)))
        in_specs.append(pl.BlockSpec((LANES, t), lambda bi, g, i: (0, 0)))
        in_specs.append(pl.BlockSpec((t, LANES), lambda bi, g, i: (0, 0)))
        args += [sel, emat, emat_t]
        scratch = [pltpu.VMEM((chunk, hg * QB), F32), pltpu.VMEM((chunk, hg * QB), F32),
                   pltpu.VMEM((1, hg * QB), F32), pltpu.VMEM((1, hg * QB), F32),
                   pltpu.VMEM((1, hg * QB), F32), pltpu.VMEM((HEAD_DIM + ONES_ROWS, hg * QB), F32)]
    return pl.pallas_call(
        functools.partial(_band_kernel, mode=mode, hg=hg, chunk=chunk),
        grid=(b, n_groups, nb),
        in_specs=in_specs,
        out_specs=pl.BlockSpec((QB, qw), lambda bi, g, i: (bi * nb + i, g)),
        out_shape=jax.ShapeDtypeStruct((b * t, n_groups * qw), BF16),
        scratch_shapes=scratch,
        compiler_params=_cparams(("parallel", "parallel", "arbitrary")),
        name="band_" + mode,
    )(*args)


def _compress_kernel(x_ref, pos_ref, w1_ref, b1_ref, w2_ref, b2_ref, o_ref, *, nck):
    half = NSA_CMP_STRIDE * HEAD_DIM
    c = x_ref[0, 0].astype(F32)
    first = _dot((c + pos_ref[0, :, :half]).astype(BF16), w1_ref[0, :half, :])
    second = _dot((c + pos_ref[0, :, half:]).astype(BF16), w1_ref[0, half:, :])
    hid = jax.nn.gelu(first + pltpu.roll(second, nck - 1, 0) + b1_ref[0])
    o_ref[0, 0] = (_dot(hid.astype(BF16), w2_ref[0]) + b2_ref[0]).astype(o_ref.dtype)


def _compress(chunks, pos, w1, b1, w2d, b2d):
    _, bg, nck, cin = chunks.shape
    return pl.pallas_call(
        functools.partial(_compress_kernel, nck=nck),
        grid=(2, bg),
        in_specs=[pl.BlockSpec((1, 1, nck, cin), lambda s, i: (s, i, 0, 0)),
                  pl.BlockSpec((1, 1, 2 * cin), lambda s, i: (s, 0, 0)),
                  pl.BlockSpec((1, 2 * cin, NSA_CMP_HIDDEN), lambda s, i: (s, 0, 0)),
                  pl.BlockSpec((1, 1, NSA_CMP_HIDDEN), lambda s, i: (s, 0, 0)),
                  pl.BlockSpec((1, NSA_CMP_HIDDEN, LANES), lambda s, i: (s, 0, 0)),
                  pl.BlockSpec((1, 1, LANES), lambda s, i: (s, 0, 0))],
        out_specs=pl.BlockSpec((1, 1, nck, LANES), lambda s, i: (s, i, 0, 0)),
        out_shape=jax.ShapeDtypeStruct((2, bg, nck, LANES), BF16),
        compiler_params=_cparams(("parallel", "parallel")),
        name="nsa_compress",
    )(chunks, pos, w1, b1, w2d, b2d)


def _cmp_kernel(q_ref, kc_ref, vct_ref, ztab_ref, mmat_ref, o_ref, sel_ref, *, hg, ncp, ns, n_sel):
    i = pl.program_id(2)
    cols = hg * QB
    qstack = _stack_heads(q_ref, hg)
    s = _dot_nt(kc_ref[0, 0], qstack)
    start = pl.multiple_of(ncp - 8 * i, 8)
    delta = jnp.concatenate([ztab_ref[h, pl.ds(start, ncp), :] for h in range(hg)], axis=1)
    nn = lax.broadcasted_iota(jnp.int32, (ncp, QB), 0)
    qq = lax.broadcasted_iota(jnp.int32, (ncp, QB), 1)
    valid1 = (i * QB + qq - NSA_CMP_STRIDE * nn - (NSA_CMP_LEN - 1)) >= 0
    valid = jnp.concatenate([valid1] * hg, axis=1)
    s = jnp.where(valid, s + delta, NEG_INF)
    m = jnp.max(s, axis=0, keepdims=True)
    e = jnp.exp2(s - m)
    p = jnp.where(valid, e * (1.0 / jnp.sum(e, axis=0, keepdims=True)), 0.0)
    _store_heads(o_ref, _dot(vct_ref[0, 0], p.astype(BF16)), hg)

    imp = p[:, 0:QB]
    for h in range(1, hg):
        imp = imp + p[:, h * QB:(h + 1) * QB]
    hi, mid, lo = _split3(imp)
    mm = mmat_ref[...]
    imp_sel = _dot(mm, hi) + _dot(mm, mid) + _dot(mm, lo)
    sb = lax.broadcasted_iota(jnp.int32, (ns, QB), 0)
    tb = (i * QB + lax.broadcasted_iota(jnp.int32, (ns, QB), 1)) // NSA_SEL_LEN
    forced = jnp.logical_or(jnp.logical_or(sb == 0, sb == tb), sb == tb - 1)
    score = jnp.where(forced, FORCE_SCORE, jnp.where(sb <= tb, imp_sel, NEG_INF))
    sub = 8
    rows = [score[v * sub:(v + 1) * sub, :] for v in range(ns // sub)]
    ranks = [jnp.zeros((sub, QB), F32) for _ in rows]
    sub_idx = lax.broadcasted_iota(jnp.int32, (sub, QB), 0)
    for r in range(ns):
        other = score[r:r + 1, :]
        for v, mine in enumerate(rows):
            if v < r // sub:
                inc = jnp.where(other > mine, 1.0, 0.0)
            elif v > r // sub:
                inc = jnp.where(other >= mine, 1.0, 0.0)
            else:
                tie = jnp.where(sub_idx > r % sub, 1.0, 0.0)
                inc = jnp.where(other > mine, 1.0, jnp.where(other == mine, tie, 0.0))
            ranks[v] = ranks[v] + inc
    chosen = [jnp.where(rk < n_sel, 1.0, 0.0) for rk in ranks]
    chosen.append(jnp.zeros((LANES - ns, QB), F32))
    sel_ref[0, 0] = jnp.concatenate(chosen, axis=0).T.astype(sel_ref.dtype)


def _cmp_attention(proj, kc, vc_t, ztab, mmat_t, b, t):
    nb = t // QB
    hg = NSA_GROUP
    qw = hg * HEAD_DIM
    ncp = t // NSA_CMP_STRIDE
    ns = t // NSA_SEL_LEN
    n_sel = min(NSA_SEL_TOPK, ns)
    return pl.pallas_call(
        functools.partial(_cmp_kernel, hg=hg, ncp=ncp, ns=ns, n_sel=n_sel),
        grid=(b, NSA_KV_HEADS, nb),
        in_specs=[
            pl.BlockSpec((QB, qw), lambda bi, g, i: (bi * nb + i, C_NQ // qw + g)),
            pl.BlockSpec((1, 1, ncp, LANES), lambda bi, g, i: (bi, g, 0, 0)),
            pl.BlockSpec((1, 1, HEAD_DIM, ncp), lambda bi, g, i: (bi, g, 0, 0)),
            pl.BlockSpec((hg, 2 * ncp, QB), lambda bi, g, i: (g, 0, 0)),
            pl.BlockSpec((ns, ncp), lambda bi, g, i: (0, 0)),
        ],
        out_specs=[pl.BlockSpec((QB, qw), lambda bi, g, i: (bi * nb + i, g)),
                   pl.BlockSpec((1, 1, QB, LANES), lambda bi, g, i: (bi, g, i, 0))],
        out_shape=[jax.ShapeDtypeStruct((b * t, NSA_W), BF16),
                   jax.ShapeDtypeStruct((b, NSA_KV_HEADS, t, LANES), BF16)],
        compiler_params=_cparams(("parallel", "parallel", "arbitrary")),
        name="nsa_cmp_select",
    )(proj, kc, vc_t, ztab, mmat_t)


def _combine_kernel(oc_ref, os_ref, ow_ref, g_ref, o_ref):
    sg = jax.nn.sigmoid(g_ref[...])
    lane = lax.broadcasted_iota(jnp.int32, (1, LANES), 1)
    upper = lane >= HEAD_DIM
    for p in range(NSA_HEADS // 2):
        acc = None
        for br, ref in enumerate((oc_ref, os_ref, ow_ref)):
            c0 = G_NG + NSA_N_BRANCH * (2 * p) + br
            c1 = G_NG + NSA_N_BRANCH * (2 * p + 1) + br
            gate = jnp.where(upper, sg[:, c1:c1 + 1], sg[:, c0:c0 + 1])
            term = gate * ref[:, p * LANES:(p + 1) * LANES].astype(F32)
            acc = term if acc is None else acc + term
        o_ref[:, p * LANES:(p + 1) * LANES] = acc.astype(o_ref.dtype)


def _nsa_combine(o_cmp, o_slc, o_win, gates, tm=512):
    n = o_cmp.shape[0]
    spec = pl.BlockSpec((tm, NSA_W), lambda i: (i, 0))
    return pl.pallas_call(
        _combine_kernel,
        grid=(n // tm,),
        in_specs=[spec, spec, spec, pl.BlockSpec((tm, D_GATE), lambda i: (i, 0))],
        out_specs=spec,
        out_shape=jax.ShapeDtypeStruct((n, NSA_W), BF16),
        compiler_params=_cparams(("parallel",)),
        name="nsa_combine",
    )(o_cmp, o_slc, o_win, gates)


def _ffn_kernel(te_ref, nu_ref, *refs, fused_norm, chained):
    if fused_norm:
        x_ref, g_ref, wg_ref, wu_ref, wd_ref, o_ref, acc_ref, xn_ref = refs
    elif chained:
        x_ref, wg_ref, wu_ref, wd_ref, _, o_ref, acc_ref = refs
    else:
        x_ref, wg_ref, wu_ref, wd_ref, o_ref, acc_ref = refs
    del te_ref
    i = pl.program_id(0)
    k = pl.program_id(1)
    last = pl.num_programs(1) - 1
    used = i < nu_ref[0]

    @pl.when(jnp.logical_and(used, k == 0))
    def _():
        if fused_norm:
            h = x_ref[...]
            y = h * lax.rsqrt(jnp.mean(h * h, axis=-1, keepdims=True) + RMS_EPS)
            xn_ref[...] = (y * g_ref[...]).astype(BF16)
            acc_ref[...] = h
        else:
            acc_ref[...] = jnp.zeros_like(acc_ref)

    @pl.when(used)
    def _():
        x = xn_ref[...] if fused_norm else x_ref[...]
        gate = _dot(x, wg_ref[0].astype(BF16))
        up = _dot(x, wu_ref[0].astype(BF16))
        hid = (jax.nn.silu(gate) * up).astype(BF16)
        acc_ref[...] += _dot(hid, wd_ref[0].astype(BF16))

    @pl.when(jnp.logical_and(used, k == last))
    def _():
        o_ref[...] = acc_ref[...].astype(o_ref.dtype)

    @pl.when(jnp.logical_and(jnp.logical_not(used), k == last))
    def _():
        o_ref[...] = jnp.zeros_like(o_ref)


def _ffn(x, w_gate, w_up, w_down, tile_expert, n_used, out_dtype, gain=None, tm=512, tf=512,
         out_rows=None, out_tile0=0, out_buf=None):
    r, d = x.shape
    nk = D_FF // tf
    n_tiles = r // tm
    fused_norm = gain is not None
    out_rows = r if out_rows is None else out_rows

    def tile(i, nu):
        return jnp.minimum(i, jnp.maximum(nu[0] - 1, 0))

    def kk(i, k, nu):
        return jnp.where(i < nu[0], k, nk - 1)

    in_specs = [pl.BlockSpec((tm, d), lambda i, k, te, nu: (tile(i, nu), 0))]
    args = [x]
    scratch = [pltpu.VMEM((tm, d), F32)]
    if fused_norm:
        in_specs.append(pl.BlockSpec((1, d), lambda i, k, te, nu: (0, 0)))
        args.append(gain.reshape(1, d).astype(F32))
        scratch.append(pltpu.VMEM((tm, d), BF16))
    in_specs += [
        pl.BlockSpec((1, d, tf), lambda i, k, te, nu: (te[tile(i, nu)], 0, kk(i, k, nu))),
        pl.BlockSpec((1, d, tf), lambda i, k, te, nu: (te[tile(i, nu)], 0, kk(i, k, nu))),
        pl.BlockSpec((1, tf, d), lambda i, k, te, nu: (te[tile(i, nu)], kk(i, k, nu), 0)),
    ]
    args += [w_gate, w_up, w_down]
    aliases = {}
    if out_buf is not None:
        in_specs.append(pl.BlockSpec(memory_space=pl.ANY))
        args.append(out_buf)
        aliases = {2 + len(args) - 1: 0}
    return pl.pallas_call(
        functools.partial(_ffn_kernel, fused_norm=fused_norm, chained=out_buf is not None),
        grid_spec=pltpu.PrefetchScalarGridSpec(
            num_scalar_prefetch=2,
            grid=(n_tiles, nk),
            in_specs=in_specs,
            out_specs=pl.BlockSpec((tm, d), lambda i, k, te, nu: (i + out_tile0, 0)),
            scratch_shapes=scratch,
        ),
        out_shape=jax.ShapeDtypeStruct((out_rows, d), out_dtype),
        input_output_aliases=aliases,
        compiler_params=_cparams(("arbitrary", "arbitrary")),
        name="swiglu_ffn",
    )(tile_expert, n_used, *args)


def _router_kernel(l_ref, o_ref):
    lane = lax.broadcasted_iota(jnp.int32, l_ref.shape, 1)
    lf = lane.astype(F32)
    lg = jnp.where(lane < N_EXPERTS, l_ref[...], REMOVED)
    v1 = jnp.max(lg, axis=1, keepdims=True)
    i1 = jnp.min(jnp.where(lg == v1, lf, float(LANES)), axis=1, keepdims=True)
    lg2 = jnp.where(lf == i1, REMOVED, lg)
    v2 = jnp.max(lg2, axis=1, keepdims=True)
    i2 = jnp.min(jnp.where(lg2 == v2, lf, float(LANES)), axis=1, keepdims=True)
    e2 = jnp.exp(v2 - v1)
    den = 1.0 + e2
    p1 = 1.0 / den
    p2 = e2 / den
    out = jnp.where(lane == N_EXPERTS, i1, 0.0)
    out = jnp.where(lane == N_EXPERTS + 1, i2, out)
    out = jnp.where(lane == N_EXPERTS + 2, p1, out)
    out = jnp.where(lane == N_EXPERTS + 3, p2, out)
    o_ref[...] = out


def _router_top2(logits, tm=512):
    n = logits.shape[0]
    spec = pl.BlockSpec((tm, LANES), lambda i: (i, 0))
    return pl.pallas_call(
        _router_kernel,
        grid=(n // tm,),
        in_specs=[spec],
        out_specs=spec,
        out_shape=jax.ShapeDtypeStruct((n, LANES), F32),
        compiler_params=_cparams(("parallel",)),
        name="moe_router_top2",
    )(logits)


def _moe_combine_kernel(h_ref, a_ref, b_ref, top_ref, g_ref, o_ref, *, final_norm):
    p0 = top_ref[:, N_EXPERTS + TOP_K:N_EXPERTS + TOP_K + 1]
    p1 = top_ref[:, N_EXPERTS + TOP_K + 1:N_EXPERTS + TOP_K + 2]
    y = h_ref[...] + p0 * a_ref[...].astype(F32) + p1 * b_ref[...].astype(F32)
    if final_norm:
        y = y * lax.rsqrt(jnp.mean(y * y, axis=-1, keepdims=True) + RMS_EPS) * g_ref[...]
    o_ref[...] = y


def _moe_combine(h, ya, yb, top, gain, final_norm, tm=512):
    n, d = h.shape
    row = pl.BlockSpec((tm, d), lambda i: (i, 0))
    return pl.pallas_call(
        functools.partial(_moe_combine_kernel, final_norm=final_norm),
        grid=(n // tm,),
        in_specs=[row, row, row, pl.BlockSpec((tm, LANES), lambda i: (i, 0)),
                  pl.BlockSpec((1, d), lambda i: (0, 0))],
        out_specs=row,
        out_shape=jax.ShapeDtypeStruct((n, d), F32),
        compiler_params=_cparams(("parallel",)),
        name="moe_combine",
    )(h, ya, yb, top, gain.reshape(1, d).astype(F32))


def _t5_bucket_np(dist):
    n = np.maximum(dist, 0)
    max_exact = REL_BUCKETS // 2
    nf = np.maximum(n, 1).astype(np.float32)
    large = max_exact + (np.log(nf / np.float32(max_exact)) / np.float32(math.log(REL_MAX_DIST / max_exact))
                         * np.float32(REL_BUCKETS - max_exact)).astype(np.int32)
    large = np.minimum(large, REL_BUCKETS - 1)
    return np.where(n < max_exact, n, large).astype(np.int32)


def _band_tables(rel_tab, t):
    q = np.arange(QB)[:, None]
    k = np.arange(QB)[None, :]
    rel = rel_tab.astype(F32) * LOG2E
    toep = jnp.take(rel, jnp.asarray(_t5_bucket_np((q - k) % QB)), axis=0)
    toep = toep.transpose(2, 0, 1)
    before = jnp.asarray(k > q)
    far = rel[REL_BUCKETS - 1, SWA_HEADS:]

    def tiles(tab, fill_prev, extra=()):
        heads = tab.shape[0]
        parts = [jnp.where(before, tab, fill_prev), jnp.where(before, NEG_INF, tab)]
        parts += [jnp.broadcast_to(e, tab.shape) for e in extra]
        return jnp.stack([p.reshape(2, heads // 2 * QB, QB) for p in parts], axis=1)

    tabs_swa = tiles(toep[:SWA_HEADS], NEG_INF)
    tabs_nsa = tiles(toep[SWA_HEADS:] - far[:, None, None], 0.0, extra=[jnp.where(before, 0.0, NEG_INF)])
    ncp = t // NSA_CMP_STRIDE
    m = np.arange(-9, 7)[:, None]
    qr = np.arange(QB)[None, :]
    d = qr - NSA_CMP_STRIDE * m - (NSA_CMP_LEN - 1)
    inband = (d >= 0) & (d < REL_MAX_DIST)
    vals = jnp.take(rel[:, SWA_HEADS:], jnp.asarray(_t5_bucket_np(np.clip(d, 0, None))), axis=0)
    vals = vals.transpose(2, 0, 1) - far[:, None, None]
    band = jnp.where(jnp.asarray(inband), vals, 0.0)
    ztab = jnp.pad(band, ((0, 0), (ncp - 9, ncp - 7), (0, 0)))
    return tabs_swa, tabs_nsa, ztab


def _selection_matrices(t):
    ncp = t // NSA_CMP_STRIDE
    ns = t // NSA_SEL_LEN
    per = NSA_SEL_LEN // NSA_CMP_STRIDE
    ratio = NSA_CMP_LEN // NSA_CMP_STRIDE
    mmat_t = np.zeros((ns, ncp), np.float32)
    for n in range(ncp - 1):
        for j in range(ratio):
            mmat_t[(n + j) // per, n] += 1.0
    emat_t = (np.arange(t)[:, None] // NSA_SEL_LEN == np.arange(LANES)[None, :]).astype(np.float32)
    return jnp.asarray(mmat_t, BF16), (jnp.asarray(emat_t.T, BF16), jnp.asarray(emat_t, BF16))


def _dup(w):
    d = w.shape[0]
    w = w.reshape(d, -1, 1, HEAD_DIM)
    return jnp.broadcast_to(w, (d, w.shape[1], 2, HEAD_DIM)).reshape(d, -1)


def _prep_in_weights(w_in_l):
    sizes = (FOX_W, FOX_W, FOX_W, FOX_HEADS, SWA_W, SWA_KV_W, SWA_KV_W,
             NSA_W, NSA_KV_W, NSA_KV_W, NSA_KV_W, NSA_KV_W, NSA_KV_W, NSA_KV_W, NSA_HEADS * NSA_N_BRANCH)
    splits = [int(s) for s in np.cumsum(sizes)[:-1]]
    (fq, fk, fv, ff, sq, sk, sv, nq, nkc, nvc, nks, nvs, nkw, nvw, ng) = jnp.split(w_in_l, splits, axis=-1)
    scale = HEAD_DIM ** -0.5 * LOG2E
    cols = [fq * scale, fk, fv, sq * scale, nq * scale, _dup(sk), _dup(nks), _dup(nkw), sv, nvs, nvw, nkc, nvc]
    cols.append(jnp.zeros((w_in_l.shape[0], D_PROJ - sum(c.shape[1] for c in cols)), w_in_l.dtype))
    w_proj = jnp.concatenate(cols, axis=-1).astype(BF16)
    pad = jnp.zeros((w_in_l.shape[0], D_GATE - FOX_HEADS - NSA_HEADS * NSA_N_BRANCH), w_in_l.dtype)
    w_gate = jnp.concatenate([ff, ng, pad], axis=-1).astype(BF16)
    return w_proj, w_gate


def _compress_inputs(proj, b, t):
    g = NSA_KV_HEADS
    nck = t // NSA_CMP_STRIDE
    x = proj[:, C_NKC:C_NKC + 2 * NSA_KV_W].reshape(b, nck, NSA_CMP_STRIDE, 2, g, HEAD_DIM)
    return x.transpose(3, 0, 4, 1, 2, 5).reshape(2, b * g, nck, NSA_CMP_STRIDE * HEAD_DIM)


def _mixer(h, gain, b, t, w_in_l, forget_bias, sinks, cmp_pos, cmp_w1, cmp_b1, cmp_w2, cmp_b2, tables):
    tabs_swa, tabs_nsa, ztab, mmat_t, emats = tables
    w_proj, w_gate = _prep_in_weights(w_in_l)
    proj, gates = _in_proj(h, gain, w_proj, w_gate)
    fv_t = proj[:, C_FV:C_FV + FOX_W].reshape(b, t, FOX_W).transpose(0, 2, 1)
    v_t = proj[:, C_V:C_V + 3 * NSA_KV_W].reshape(b, t, 3 * NSA_KV_W).transpose(0, 2, 1)

    qx, kx = _fox_decay_operands(gates, forget_bias, b, t)
    o_fox = _fox_attention(proj, fv_t, qx, kx, b, t)

    o_swa = _band_attention(proj, v_t, tabs_swa, b, t, mode="swa", hg=SWA_GROUP,
                            c_q=C_SQ, c_k=C_SK, v_blk=0, sinks=sinks)

    flat = _compress_inputs(proj, b, t)
    pos = cmp_pos.reshape(2, 1, NSA_CMP_LEN * HEAD_DIM).astype(F32)
    w2d = jnp.concatenate([cmp_w2, cmp_w2], axis=-1).astype(BF16)
    b2d = jnp.concatenate([cmp_b2, cmp_b2], axis=-1).reshape(2, 1, LANES).astype(F32)
    kvc = _compress(flat, pos, cmp_w1.astype(BF16), cmp_b1.reshape(2, 1, NSA_CMP_HIDDEN).astype(F32), w2d, b2d)
    ncp = t // NSA_CMP_STRIDE
    kvc = kvc.reshape(2, b, NSA_KV_HEADS, ncp, LANES)
    vc_t = kvc[1, :, :, :, :HEAD_DIM].transpose(0, 1, 3, 2)
    o_cmp, sel = _cmp_attention(proj, kvc[0], vc_t, ztab, mmat_t, b, t)
    o_slc = _band_attention(proj, v_t, tabs_nsa, b, t, mode="sel", hg=NSA_GROUP,
                            c_q=C_NQ, c_k=C_NKS, v_blk=2, sel=sel, emats=emats)
    o_win = _band_attention(proj, v_t, tabs_nsa, b, t, mode="win", hg=NSA_GROUP,
                            c_q=C_NQ, c_k=C_NKW, v_blk=4)
    o_nsa = _nsa_combine(o_cmp, o_slc, o_win, gates)
    return o_fox, o_swa, o_nsa


def _moe(hn, h, router, w_gate, w_up, w_down, gain, final_norm, tm=1024, tf=256):
    n, d = hn.shape
    w_r = jnp.zeros((d, LANES), BF16).at[:, :N_EXPERTS].set(router.astype(BF16))
    logits = _matmul([hn], [w_r], F32, tn=LANES, name="router_logits")
    top = _router_top2(logits)
    e_idx = top[:, N_EXPERTS:N_EXPERTS + TOP_K].astype(jnp.int32)
    e_flat = e_idx.reshape(-1)
    onehot = (e_flat[:, None] == jnp.arange(N_EXPERTS)[None, :]).astype(jnp.int32)
    csum = jnp.cumsum(onehot, axis=0)
    counts = csum[-1]
    rank = jnp.take_along_axis(csum, e_flat[:, None], axis=1)[:, 0] - 1
    padded = ((counts + tm - 1) // tm) * tm
    ends = jnp.cumsum(padded)
    starts = ends - padded
    dest = starts[e_flat] + rank
    r_pad = n * TOP_K + N_EXPERTS * tm
    src_tok = jnp.zeros((r_pad,), jnp.int32).at[dest].set(jnp.arange(n * TOP_K, dtype=jnp.int32) // TOP_K)
    tile_start = jnp.arange(r_pad // tm, dtype=jnp.int32) * tm
    tile_expert = jnp.minimum(jnp.sum(tile_start[:, None] >= ends[None, :], axis=1), N_EXPERTS - 1).astype(jnp.int32)
    n_used = (ends[-1:] // tm).astype(jnp.int32)
    n_tiles = r_pad // tm
    per = n_tiles // MOE_CHUNKS
    y = None
    for c in range(MOE_CHUNKS):
        xs = hn.at[src_tok[c * per * tm:(c + 1) * per * tm]].get(mode="promise_in_bounds")
        used_c = jnp.clip(n_used - c * per, 0, per)
        y = _ffn(xs, w_gate, w_up, w_down, tile_expert[c * per:(c + 1) * per], used_c, BF16, tm=tm, tf=tf,
                 out_rows=r_pad, out_tile0=c * per, out_buf=y)
    dest = dest.reshape(n, TOP_K)
    ya = y.at[dest[:, 0]].get(mode="promise_in_bounds")
    yb = y.at[dest[:, 1]].get(mode="promise_in_bounds")
    return _moe_combine(h, ya, yb, top, gain, final_norm)


def kernel(x, attn_norm, w_in, fox_forget_bias, swa_sinks, nsa_cmp_pos, nsa_cmp_w1, nsa_cmp_b1, nsa_cmp_w2,
           nsa_cmp_b2, w_out, rel_bias_table, ffn_norm, dense_w_gate, dense_w_up, dense_w_down, moe_router,
           moe_w_gate, moe_w_up, moe_w_down, final_norm):
    b, t, d = x.shape
    n = b * t
    depth = w_in.shape[0]
    tables = _band_tables(rel_bias_table, t) + _selection_matrices(t)
    h = x.reshape(n, d)
    for layer in range(depth):
        o_fox, o_swa, o_nsa = _mixer(h, attn_norm[layer], b, t, w_in[layer], fox_forget_bias[layer],
                                     swa_sinks[layer], nsa_cmp_pos[layer], nsa_cmp_w1[layer], nsa_cmp_b1[layer],
                                     nsa_cmp_w2[layer], nsa_cmp_b2[layer], tables)
        wo = w_out[layer].astype(BF16)
        h = _matmul([o_fox, o_swa, o_nsa], [wo[:FOX_W], wo[FOX_W:FOX_W + SWA_W], wo[FOX_W + SWA_W:]],
                    F32, residual=h, tn=1024, name="out_proj")
        i = layer // 2
        last = layer == depth - 1
        if layer % 2 == 0:
            tm = 512
            zeros = jnp.zeros((n // tm,), jnp.int32)
            h = _ffn(h, dense_w_gate[i][None].astype(BF16), dense_w_up[i][None].astype(BF16),
                     dense_w_down[i][None].astype(BF16), zeros, jnp.full((1,), n // tm, jnp.int32),
                     F32, gain=ffn_norm[layer], tm=tm)
            if last:
                h = _rmsnorm(h, final_norm, F32)
        else:
            hn = _rmsnorm(h, ffn_norm[layer], BF16)
            h = _moe(hn, h, moe_router[i], moe_w_gate[i], moe_w_up[i], moe_w_down[i], final_norm, last)
    return h.reshape(b, t, d)
```

```python
import functools
import math

import numpy as np
import jax
import jax.numpy as jnp
from jax import lax
from jax.experimental import pallas as pl
from jax.experimental.pallas import tpu as pltpu

F32 = jnp.float32
BF16 = jnp.bfloat16

D_MODEL = 2048
HEAD_DIM = 64
FOX_HEADS = 8
FOX_W = FOX_HEADS * HEAD_DIM
SWA_HEADS = 8
SWA_KV_HEADS = 2
SWA_GROUP = SWA_HEADS // SWA_KV_HEADS
SWA_W = SWA_HEADS * HEAD_DIM
SWA_KV_W = SWA_KV_HEADS * HEAD_DIM
SWA_WINDOW = 128
NSA_HEADS = 16
NSA_KV_HEADS = 2
NSA_GROUP = NSA_HEADS // NSA_KV_HEADS
NSA_W = NSA_HEADS * HEAD_DIM
NSA_KV_W = NSA_KV_HEADS * HEAD_DIM
NSA_CMP_LEN = 32
NSA_CMP_STRIDE = 16
NSA_CMP_HIDDEN = 256
NSA_SEL_LEN = 64
NSA_SEL_TOPK = 16
NSA_WINDOW = 512
NSA_N_BRANCH = 3
FORCE_SCORE = 1e9
REL_BUCKETS = 32
REL_MAX_DIST = 128
D_FF = 5632
N_EXPERTS = 8
TOP_K = 2
RMS_EPS = 1e-6
NEG_INF = -1e30
M_INIT = -1e29
REMOVED = -3e38
LOG2E = 1.4426950408889634
ONES_ROWS = 16
MOE_CHUNKS = 4

LANES = 128
QB = 128
VMEM_LIMIT = 56 * 1024 * 1024
FFN_VMEM_LIMIT = 60 * 1024 * 1024

C_FQ, C_FK, C_FV = 0, 512, 1024
C_SQ = 1536
C_NQ = 2048
C_SK, C_NKS, C_NKW = 3072, 3328, 3584
C_V = 3840
C_NKC, C_NVC = 4224, 4352
D_PROJ = 4608
PROJ_TN = 768
G_NG = FOX_HEADS
D_GATE = 128


def _cparams(sem, vmem=VMEM_LIMIT):
    return pltpu.CompilerParams(dimension_semantics=sem, vmem_limit_bytes=vmem)


def _dot(a, b):
    return jnp.dot(a, b, preferred_element_type=F32)


def _dot_nt(a, b):
    return lax.dot_general(a, b, (((1,), (1,)), ((), ())), preferred_element_type=F32)


def _split3(x):
    hi = x.astype(BF16)
    r1 = x - hi.astype(F32)
    mid = r1.astype(BF16)
    lo = (r1 - mid.astype(F32)).astype(BF16)
    return hi, mid, lo


def _pipeline3(n, stage1, stage2, stage3):
    first, second, outs = {}, {}, []
    for step in range(n + 2):
        if step < n:
            first[step] = stage1(step)
        if 0 <= step - 1 < n:
            second[step - 1] = stage2(step - 1, first.pop(step - 1))
        if 0 <= step - 2 < n:
            outs.append(stage3(step - 2, *second.pop(step - 2)))
    return outs


def _rmsnorm_kernel(x_ref, g_ref, o_ref):
    x = x_ref[...]
    y = x * lax.rsqrt(jnp.mean(x * x, axis=-1, keepdims=True) + RMS_EPS)
    o_ref[...] = (y * g_ref[...]).astype(o_ref.dtype)


def _rmsnorm(h, g, out_dtype, tm=512):
    n, d = h.shape
    return pl.pallas_call(
        _rmsnorm_kernel,
        grid=(n // tm,),
        in_specs=[pl.BlockSpec((tm, d), lambda i: (i, 0)),
                  pl.BlockSpec((1, d), lambda i: (0, 0))],
        out_specs=pl.BlockSpec((tm, d), lambda i: (i, 0)),
        out_shape=jax.ShapeDtypeStruct((n, d), out_dtype),
        compiler_params=_cparams(("parallel",)),
        name="rmsnorm",
    )(h, g.reshape(1, d).astype(F32))


def _mm_kernel(*refs, n_in, has_res):
    o_ref = refs[-1]
    acc = None
    for a in range(n_in):
        d = _dot(refs[a][...], refs[n_in + a][...])
        acc = d if acc is None else acc + d
    if has_res:
        acc = acc + refs[2 * n_in][...]
    o_ref[...] = acc.astype(o_ref.dtype)


def _matmul(xs, ws, out_dtype, residual=None, tm=1024, tn=256, name="matmul"):
    n = xs[0].shape[0]
    m = ws[0].shape[1]
    tm = min(tm, n)
    tn = min(tn, m)
    in_specs = [pl.BlockSpec((tm, x.shape[1]), lambda i, j: (i, 0)) for x in xs]
    in_specs += [pl.BlockSpec((w.shape[0], tn), lambda i, j: (0, j)) for w in ws]
    args = list(xs) + list(ws)
    if residual is not None:
        in_specs.append(pl.BlockSpec((tm, tn), lambda i, j: (i, j)))
        args.append(residual)
    return pl.pallas_call(
        functools.partial(_mm_kernel, n_in=len(xs), has_res=residual is not None),
        grid=(n // tm, m // tn),
        in_specs=in_specs,
        out_specs=pl.BlockSpec((tm, tn), lambda i, j: (i, j)),
        out_shape=jax.ShapeDtypeStruct((n, m), out_dtype),
        compiler_params=_cparams(("parallel", "arbitrary")),
        name=name,
    )(*args)


def _in_proj_kernel(h_ref, g_ref, w_ref, wgate_ref, proj_ref, gates_ref, xn_ref):
    @pl.when(pl.program_id(1) == 0)
    def _():
        h = h_ref[...]
        y = h * lax.rsqrt(jnp.mean(h * h, axis=-1, keepdims=True) + RMS_EPS)
        xn_ref[...] = (y * g_ref[...]).astype(BF16)
        gates_ref[...] = _dot(xn_ref[...], wgate_ref[...])

    proj_ref[...] = _dot(xn_ref[...], w_ref[...]).astype(proj_ref.dtype)


def _in_proj(h, gain, w_proj, w_gate, tm=1024, tn=PROJ_TN):
    n, d = h.shape
    return pl.pallas_call(
        _in_proj_kernel,
        grid=(n // tm, D_PROJ // tn),
        in_specs=[pl.BlockSpec((tm, d), lambda i, j: (i, 0)),
                  pl.BlockSpec((1, d), lambda i, j: (0, 0)),
                  pl.BlockSpec((d, tn), lambda i, j: (0, j)),
                  pl.BlockSpec((d, D_GATE), lambda i, j: (0, 0))],
        out_specs=[pl.BlockSpec((tm, tn), lambda i, j: (i, j)),
                   pl.BlockSpec((tm, D_GATE), lambda i, j: (i, 0))],
        out_shape=[jax.ShapeDtypeStruct((n, D_PROJ), BF16), jax.ShapeDtypeStruct((n, D_GATE), F32)],
        scratch_shapes=[pltpu.VMEM((tm, d), BF16)],
        compiler_params=_cparams(("parallel", "arbitrary")),
        name="in_proj",
    )(h, gain.reshape(1, d).astype(F32), w_proj, w_gate)


def _cumsum_kernel(g_ref, b_ref, pq_ref, pk_ref, oq_ref, ok_ref, qx_ref, kx_ref, carry_ref, *, tc):
    @pl.when(pl.program_id(1) == 0)
    def _():
        carry_ref[...] = jnp.zeros_like(carry_ref)

    z = g_ref[...] + b_ref[...]
    log_f = jnp.minimum(z, 0.0) - jnp.log1p(jnp.exp(-jnp.abs(z)))
    row = lax.broadcasted_iota(jnp.int32, (tc, tc), 0)
    col = lax.broadcasted_iota(jnp.int32, (tc, tc), 1)
    tri = jnp.where(col <= row, 1.0, 0.0).astype(BF16)
    hi, mid, lo = _split3(log_f)
    c = _dot(tri, hi) + _dot(tri, mid) + _dot(tri, lo) + carry_ref[...]
    carry_ref[...] = c[tc - 1:tc, :]
    parts = jnp.concatenate(_split3(c * LOG2E), axis=1)
    qx_ref[...] = (_dot(parts, pq_ref[...]) + oq_ref[...]).astype(qx_ref.dtype)
    kx_ref[...] = (_dot(parts, pk_ref[...]) + ok_ref[...]).astype(kx_ref.dtype)


def _fox_decay_operands(gates, forget_bias, b, t, tc=256):
    nt = t // tc
    wide = FOX_HEADS * LANES
    bias = jnp.zeros((1, D_GATE), F32).at[0, :FOX_HEADS].set(forget_bias.astype(F32))
    pq = np.zeros((3 * D_GATE, wide), np.float32)
    pk = np.zeros((3 * D_GATE, wide), np.float32)
    oq = np.zeros((1, wide), np.float32)
    ok = np.zeros((1, wide), np.float32)
    for h in range(FOX_HEADS):
        for part in range(3):
            pq[part * D_GATE + h, h * LANES + part] = 1.0
            pk[part * D_GATE + h, h * LANES + 3 + part] = -1.0
            oq[0, h * LANES + 3 + part] = 1.0
            ok[0, h * LANES + part] = 1.0
    const = lambda shape: pl.BlockSpec(shape, lambda bi, ti: (0, 0))
    out_spec = pl.BlockSpec((tc, wide), lambda bi, ti: (bi * nt + ti, 0))
    return pl.pallas_call(
        functools.partial(_cumsum_kernel, tc=tc),
        grid=(b, nt),
        in_specs=[pl.BlockSpec((tc, D_GATE), lambda bi, ti: (bi * nt + ti, 0)),
                  const((1, D_GATE)), const((3 * D_GATE, wide)), const((3 * D_GATE, wide)),
                  const((1, wide)), const((1, wide))],
        out_specs=[out_spec, out_spec],
        out_shape=[jax.ShapeDtypeStruct((b * t, wide), BF16), jax.ShapeDtypeStruct((b * t, wide), BF16)],
        scratch_shapes=[pltpu.VMEM((1, D_GATE), F32)],
        compiler_params=_cparams(("parallel", "arbitrary")),
        name="fox_cumsum",
    )(gates, bias, jnp.asarray(pq, BF16), jnp.asarray(pk, BF16), jnp.asarray(oq), jnp.asarray(ok))


def _fox_kernel(qi_ref, kj_ref, q_ref, k_ref, vt_ref, qx_ref, kx_ref, o_ref, m_sc, acc_sc, *, tq):
    i = qi_ref[pl.program_id(1)]
    j = kj_ref[pl.program_id(1)]
    lane = lax.broadcasted_iota(jnp.int32, (1, LANES), 1)
    upper = lane >= HEAD_DIM

    @pl.when(j == 0)
    def _():
        m_sc[...] = jnp.full_like(m_sc, M_INIT)
        acc_sc[...] = jnp.zeros_like(acc_sc)

    def step(diagonal):
        if diagonal:
            key = lax.broadcasted_iota(jnp.int32, (tq, tq), 0)
            qry = lax.broadcasted_iota(jnp.int32, (tq, tq), 1)
            causal = key <= qry
        ones = jnp.ones((ONES_ROWS, tq), BF16)

        def qk(h):
            p, a = divmod(h, 2)
            q2 = q_ref[:, p * LANES:(p + 1) * LANES]
            k2 = k_ref[:, p * LANES:(p + 1) * LANES]
            zero = jnp.zeros_like(q2)
            qa = jnp.where(upper, q2, zero) if a else jnp.where(upper, zero, q2)
            q_aug = jnp.concatenate([qa, qx_ref[:, h * LANES:(h + 1) * LANES]], axis=1)
            k_aug = jnp.concatenate([k2, kx_ref[:, h * LANES:(h + 1) * LANES]], axis=1)
            s = _dot_nt(k_aug, q_aug)
            if diagonal:
                s = jnp.where(causal, s, NEG_INF)
            return s, jnp.max(s, axis=0, keepdims=True)

        def softmax(h, scored):
            s, s_max = scored
            m_prev = m_sc[h]
            m_new = jnp.maximum(m_prev, s_max)
            m_sc[h] = m_new
            return jnp.exp2(m_prev - m_new), jnp.exp2(s - m_new).astype(BF16)

        def pv(h, alpha, pr):
            vt = jnp.concatenate([vt_ref[0, h * HEAD_DIM:(h + 1) * HEAD_DIM, :], ones], axis=0)
            acc_sc[h] = alpha * acc_sc[h] + _dot(vt, pr)

        _pipeline3(FOX_HEADS, qk, softmax, pv)

    @pl.when(j < i)
    def _():
        step(False)

    @pl.when(j == i)
    def _():
        step(True)
        for p in range(FOX_HEADS // 2):
            outs = []
            for h in (2 * p, 2 * p + 1):
                acc = acc_sc[h]
                outs.append(acc[:HEAD_DIM] * (1.0 / acc[HEAD_DIM:HEAD_DIM + 1]))
            o_ref[:, p * LANES:(p + 1) * LANES] = jnp.concatenate(outs, axis=0).T.astype(o_ref.dtype)


def _fox_attention(proj, v_t, qx, kx, b, t, tq=512):
    nt = t // tq
    wide = FOX_HEADS * LANES
    pairs = [(i, j) for i in range(nt) for j in range(i + 1)]
    qi = jnp.asarray([p[0] for p in pairs], jnp.int32)
    kj = jnp.asarray([p[1] for p in pairs], jnp.int32)
    return pl.pallas_call(
        functools.partial(_fox_kernel, tq=tq),
        grid_spec=pltpu.PrefetchScalarGridSpec(
            num_scalar_prefetch=2,
            grid=(b, len(pairs)),
            in_specs=[
                pl.BlockSpec((tq, FOX_W), lambda bi, s, qi, kj: (bi * nt + qi[s], C_FQ // FOX_W)),
                pl.BlockSpec((tq, FOX_W), lambda bi, s, qi, kj: (bi * nt + kj[s], C_FK // FOX_W)),
                pl.BlockSpec((1, FOX_W, tq), lambda bi, s, qi, kj: (bi, 0, kj[s])),
                pl.BlockSpec((tq, wide), lambda bi, s, qi, kj: (bi * nt + qi[s], 0)),
                pl.BlockSpec((tq, wide), lambda bi, s, qi, kj: (bi * nt + kj[s], 0)),
            ],
            out_specs=pl.BlockSpec((tq, FOX_W), lambda bi, s, qi, kj: (bi * nt + qi[s], 0)),
            scratch_shapes=[pltpu.VMEM((FOX_HEADS, 1, tq), F32),
                            pltpu.VMEM((FOX_HEADS, HEAD_DIM + ONES_ROWS, tq), F32)],
        ),
        out_shape=jax.ShapeDtypeStruct((b * t, FOX_W), BF16),
        compiler_params=_cparams(("parallel", "arbitrary")),
        name="fox_attention",
    )(qi, kj, proj, proj, v_t, qx, kx)


def _stack_heads(q_ref, hg):
    lane = lax.broadcasted_iota(jnp.int32, (1, LANES), 1)
    upper = lane >= HEAD_DIM
    qs = []
    for p in range(hg // 2):
        q2 = q_ref[:, p * LANES:(p + 1) * LANES]
        qs.append(jnp.where(upper, jnp.zeros_like(q2), q2))
        qs.append(jnp.where(upper, q2, jnp.zeros_like(q2)))
    return jnp.concatenate(qs, axis=0)


def _store_heads(o_ref, o_t, hg):
    for p in range(hg // 2):
        pair = jnp.concatenate([o_t[:, (2 * p) * QB:(2 * p + 1) * QB],
                                o_t[:, (2 * p + 1) * QB:(2 * p + 2) * QB]], axis=0)
        o_ref[:, p * LANES:(p + 1) * LANES] = pair.T.astype(o_ref.dtype)


def _band_kernel(*refs, mode, hg, chunk):
    if mode == "swa":
        q_ref, k_ref, vt_ref, tab_ref, sink_ref, o_ref = refs
    elif mode == "sel":
        (q_ref, k_ref, vt_ref, tab_ref, sel_ref, e_ref, et_ref, o_ref,
         sa_ref, sb_ref, ma_ref, mb_ref, m_ref, acc_ref) = refs
    else:
        q_ref, k_ref, vt_ref, tab_ref, o_ref = refs
    g = pl.program_id(1)
    i = pl.program_id(2)
    cols = hg * QB
    qstack = _stack_heads(q_ref, hg)
    eye = jnp.where(lax.broadcasted_iota(jnp.int32, (QB, QB), 0) == lax.broadcasted_iota(jnp.int32, (QB, QB), 1),
                    1.0, 0.0).astype(BF16)

    def k_block(blk):
        return k_ref[pl.ds(pl.multiple_of(blk * QB, QB), QB), :]

    def vt_rows(start, size):
        return jnp.concatenate([vt_ref[0, :, pl.ds(start, size)], jnp.ones((ONES_ROWS, size), BF16)], axis=0)

    def vt_block(blk):
        return vt_rows(pl.multiple_of(blk * QB, QB), QB)

    def with_table(table):
        return jnp.concatenate([qstack, table.astype(BF16)], axis=1)

    def grp(x, j):
        return x[j * GW:(j + 1) * GW]

    def lanes(x, j):
        return x[:, j * GW:(j + 1) * GW]

    GW = cols if mode == "sel" else 2 * QB
    n_grp = cols // GW
    ip = jnp.maximum(i - 1, 0)
    t_prev = tab_ref[0, 0]
    if mode == "sel":
        sel_q = sel_ref[0, 0]
        allowed = _dot(sel_q, e_ref[:, pl.ds(pl.multiple_of(ip * QB, QB), QB)])
        t_prev = jnp.where(jnp.concatenate([allowed] * hg, axis=0) > 0.5, t_prev, NEG_INF)
    t_prev = jnp.where(i > 0, t_prev, NEG_INF)
    q_prev = with_table(t_prev)
    q_cur = with_table(tab_ref[0, 1])
    k_prev = jnp.concatenate([k_block(ip), eye], axis=1)
    k_cur = jnp.concatenate([k_block(i), eye], axis=1)
    vt_near = jnp.concatenate([vt_block(ip), vt_block(i)], axis=1)

    if mode == "swa":
        head = lax.broadcasted_iota(jnp.int32, (1, cols), 1) // QB
        sink = jnp.zeros((1, cols), F32)
        for h in range(hg):
            sink = jnp.where(head == h, sink_ref[g * hg + h], sink)

    elif mode == "win":
        n_far = NSA_WINDOW // QB - 1
        backs = range(n_far + 1, 1, -1)
        onehot = jnp.concatenate([eye, jnp.zeros(((n_far - 1) * QB, QB), BF16)], axis=0)
        k_far = jnp.concatenate(
            [jnp.concatenate([k_block(jnp.maximum(i - bk, 0)) for bk in backs], axis=0), onehot], axis=1)
        vt_far = jnp.concatenate([vt_block(jnp.maximum(i - bk, 0)) for bk in backs], axis=1)
        q_tri = with_table(tab_ref[0, 2])
        in_seq = lax.broadcasted_iota(jnp.int32, (n_far * QB, 1), 0) >= (n_far + 1 - i) * QB

    else:
        blk = lax.broadcasted_iota(jnp.int32, (QB, LANES), 1)
        pick = jnp.logical_and(sel_q.astype(F32) > 0.5, blk < 2 * (i - 1))
        sel_bias = jnp.where(pick, 0.0, NEG_INF)
        q_far = with_table(jnp.concatenate([sel_bias] * hg, axis=0))
        per = chunk // QB
        n_chunks = (jnp.maximum(i - 1, 0) + per - 1) // per
        def far_scores(c):
            off = pl.multiple_of(c * chunk, chunk)
            k_rows = jnp.concatenate([k_ref[pl.ds(off, chunk), :], et_ref[pl.ds(off, chunk), :]], axis=1)
            return _dot_nt(k_rows, q_far)

        def put_scores(c, s_ref, smax_ref):
            s = far_scores(c)
            s_ref[...] = s
            smax_ref[...] = jnp.max(s, axis=0, keepdims=True)

        def sweep(c, cur, nxt):
            put_scores(jnp.minimum(c + 1, n_chunks - 1), *nxt)
            s_ref, smax_ref = cur
            m_old = m_ref[...]
            m_new = jnp.maximum(m_old, smax_ref[...])
            pf = jnp.exp2(s_ref[...] - m_new).astype(BF16)
            acc_ref[...] = (jnp.exp2(m_old - m_new) * acc_ref[...]
                            + _dot(vt_rows(pl.multiple_of(c * chunk, chunk), chunk), pf))
            m_ref[...] = m_new

        buf_a, buf_b = (sa_ref, ma_ref), (sb_ref, mb_ref)

        def body(pair, carry):
            sweep(2 * pair, buf_a, buf_b)

            @pl.when(2 * pair + 1 < n_chunks)
            def _():
                sweep(2 * pair + 1, buf_b, buf_a)

            return carry

        m_ref[...] = jnp.full((1, cols), M_INIT, F32)
        acc_ref[...] = jnp.zeros((HEAD_DIM + ONES_ROWS, cols), F32)
        put_scores(0, *buf_a)
        lax.fori_loop(0, (n_chunks + 1) // 2, body, 0)
        m_far = m_ref[...]
        acc_far = acc_ref[...]

    def near_scores(j):
        parts = [_dot_nt(k_prev, grp(q_prev, j)), _dot_nt(k_cur, grp(q_cur, j))]
        if mode == "win":
            parts.append(jnp.where(in_seq, _dot_nt(k_far, grp(q_tri, j)), NEG_INF))
        return parts

    def near_softmax(j, parts):
        m = jnp.max(parts[0], axis=0, keepdims=True)
        for s in parts[1:]:
            m = jnp.maximum(m, jnp.max(s, axis=0, keepdims=True))
        if mode == "swa":
            m = jnp.maximum(m, lanes(sink, j))
        if mode == "sel":
            m = jnp.maximum(m, lanes(m_far, j))
        return m, [jnp.exp2(s - m).astype(BF16) for s in parts]

    def near_output(j, m, probs):
        acc = _dot(vt_near, jnp.concatenate(probs[:2], axis=0))
        if mode == "win":
            acc = acc + _dot(vt_far, probs[2])
        if mode == "sel":
            acc = jnp.exp2(lanes(m_far, j) - m) * lanes(acc_far, j) + acc
        den = acc[HEAD_DIM:HEAD_DIM + 1]
        if mode == "swa":
            den = den + jnp.exp2(lanes(sink, j) - m)
        o_t = acc[:HEAD_DIM] * (1.0 / den)
        for pp in range(GW // (2 * QB)):
            pair = jnp.concatenate([o_t[:, 2 * pp * QB:(2 * pp + 1) * QB],
                                    o_t[:, (2 * pp + 1) * QB:(2 * pp + 2) * QB]], axis=0)
            c0 = (j * (GW // (2 * QB)) + pp) * LANES
            o_ref[:, c0:c0 + LANES] = pair.T.astype(o_ref.dtype)

    _pipeline3(n_grp, near_scores, near_softmax, near_output)


def _band_attention(proj, v_t, tabs, b, t, *, mode, hg, c_q, c_k, v_blk, sinks=None, sel=None, emats=None):
    nb = t // QB
    n_groups = 2
    qw = hg * HEAD_DIM
    chunk = min(512, t)
    in_specs = [
        pl.BlockSpec((QB, qw), lambda bi, g, i: (bi * nb + i, c_q // qw + g)),
        pl.BlockSpec((t, LANES), lambda bi, g, i: (bi, c_k // LANES + g)),
        pl.BlockSpec((1, HEAD_DIM, t), lambda bi, g, i: (bi, v_blk + g, 0)),
        pl.BlockSpec((1,) + tabs.shape[1:], lambda bi, g, i: (g, 0, 0, 0)),
    ]
    args = [proj, proj, v_t, tabs]
    scratch = []
    if mode == "swa":
        in_specs.append(pl.BlockSpec(memory_space=pltpu.SMEM))
        args.append(sinks.astype(F32) * LOG2E)
    if mode == "sel":
        emat, emat_t = emats
        in_specs.append(pl.BlockSpec((1, 1, QB, LANES), lambda bi, g, i: (bi, g, i, 0)))
        in_specs.append(pl.BlockSpec((LANES, t), lambda bi, g, i: (0, 0)))
        in_specs.append(pl.BlockSpec((t, LANES), lambda bi, g, i: (0, 0)))
        args += [sel, emat, emat_t]
        scratch = [pltpu.VMEM((chunk, hg * QB), F32), pltpu.VMEM((chunk, hg * QB), F32),
                   pltpu.VMEM((1, hg * QB), F32), pltpu.VMEM((1, hg * QB), F32),
                   pltpu.VMEM((1, hg * QB), F32), pltpu.VMEM((HEAD_DIM + ONES_ROWS, hg * QB), F32)]
    return pl.pallas_call(
        functools.partial(_band_kernel, mode=mode, hg=hg, chunk=chunk),
        grid=(b, n_groups, nb),
        in_specs=in_specs,
        out_specs=pl.BlockSpec((QB, qw), lambda bi, g, i: (bi * nb + i, g)),
        out_shape=jax.ShapeDtypeStruct((b * t, n_groups * qw), BF16),
        scratch_shapes=scratch,
        compiler_params=_cparams(("parallel", "parallel", "arbitrary")),
        name="band_" + mode,
    )(*args)


def _compress_kernel(x_ref, pos_ref, w1_ref, b1_ref, w2_ref, b2_ref, o_ref, *, nck):
    half = NSA_CMP_STRIDE * HEAD_DIM
    c = x_ref[0, 0].astype(F32)
    first = _dot((c + pos_ref[0, :, :half]).astype(BF16), w1_ref[0, :half, :])
    second = _dot((c + pos_ref[0, :, half:]).astype(BF16), w1_ref[0, half:, :])
    hid = jax.nn.gelu(first + pltpu.roll(second, nck - 1, 0) + b1_ref[0])
    o_ref[0, 0] = (_dot(hid.astype(BF16), w2_ref[0]) + b2_ref[0]).astype(o_ref.dtype)


def _compress(chunks, pos, w1, b1, w2d, b2d):
    _, bg, nck, cin = chunks.shape
    return pl.pallas_call(
        functools.partial(_compress_kernel, nck=nck),
        grid=(2, bg),
        in_specs=[pl.BlockSpec((1, 1, nck, cin), lambda s, i: (s, i, 0, 0)),
                  pl.BlockSpec((1, 1, 2 * cin), lambda s, i: (s, 0, 0)),
                  pl.BlockSpec((1, 2 * cin, NSA_CMP_HIDDEN), lambda s, i: (s, 0, 0)),
                  pl.BlockSpec((1, 1, NSA_CMP_HIDDEN), lambda s, i: (s, 0, 0)),
                  pl.BlockSpec((1, NSA_CMP_HIDDEN, LANES), lambda s, i: (s, 0, 0)),
                  pl.BlockSpec((1, 1, LANES), lambda s, i: (s, 0, 0))],
        out_specs=pl.BlockSpec((1, 1, nck, LANES), lambda s, i: (s, i, 0, 0)),
        out_shape=jax.ShapeDtypeStruct((2, bg, nck, LANES), BF16),
        compiler_params=_cparams(("parallel", "parallel")),
        name="nsa_compress",
    )(chunks, pos, w1, b1, w2d, b2d)


def _cmp_kernel(q_ref, kc_ref, vct_ref, ztab_ref, mmat_ref, o_ref, sel_ref, *, hg, ncp, ns, n_sel):
    i = pl.program_id(2)
    cols = hg * QB
    qstack = _stack_heads(q_ref, hg)
    s = _dot_nt(kc_ref[0, 0], qstack)
    start = pl.multiple_of(ncp - 8 * i, 8)
    delta = jnp.concatenate([ztab_ref[h, pl.ds(start, ncp), :] for h in range(hg)], axis=1)
    s = s + delta
    m = jnp.max(s, axis=0, keepdims=True)
    e = jnp.exp2(s - m)
    p = e * jnp.where(m > M_INIT, 1.0 / jnp.sum(e, axis=0, keepdims=True), 0.0)
    _store_heads(o_ref, _dot(vct_ref[0, 0], p.astype(BF16)), hg)

    imp = p[:, 0:QB]
    for h in range(1, hg):
        imp = imp + p[:, h * QB:(h + 1) * QB]
    hi, mid, lo = _split3(imp)
    mm = mmat_ref[...]
    imp_sel = _dot(mm, hi) + _dot(mm, mid) + _dot(mm, lo)
    sb = lax.broadcasted_iota(jnp.int32, (ns, QB), 0)
    tb = (i * QB + lax.broadcasted_iota(jnp.int32, (ns, QB), 1)) // NSA_SEL_LEN
    forced = jnp.logical_or(jnp.logical_or(sb == 0, sb == tb), sb == tb - 1)
    score = jnp.where(forced, FORCE_SCORE, jnp.where(sb <= tb, imp_sel, NEG_INF))
    sub = 8
    rows = [score[v * sub:(v + 1) * sub, :] for v in range(ns // sub)]
    ranks = [jnp.zeros((sub, QB), F32) for _ in rows]
    sub_idx = lax.broadcasted_iota(jnp.int32, (sub, QB), 0)
    for r in range(ns):
        other = score[r:r + 1, :]
        for v, mine in enumerate(rows):
            if v < r // sub:
                inc = jnp.where(other > mine, 1.0, 0.0)
            elif v > r // sub:
                inc = jnp.where(other >= mine, 1.0, 0.0)
            else:
                tie = jnp.where(sub_idx > r % sub, 1.0, 0.0)
                inc = jnp.where(other > mine, 1.0, jnp.where(other == mine, tie, 0.0))
            ranks[v] = ranks[v] + inc
    chosen = [jnp.where(rk < n_sel, 1.0, 0.0) for rk in ranks]
    chosen.append(jnp.zeros((LANES - ns, QB), F32))
    sel_ref[0, 0] = jnp.concatenate(chosen, axis=0).T.astype(sel_ref.dtype)


def _cmp_attention(proj, kc, vc_t, ztab, mmat_t, b, t):
    nb = t // QB
    hg = NSA_GROUP
    qw = hg * HEAD_DIM
    ncp = t // NSA_CMP_STRIDE
    ns = t // NSA_SEL_LEN
    n_sel = min(NSA_SEL_TOPK, ns)
    return pl.pallas_call(
        functools.partial(_cmp_kernel, hg=hg, ncp=ncp, ns=ns, n_sel=n_sel),
        grid=(b, NSA_KV_HEADS, nb),
        in_specs=[
            pl.BlockSpec((QB, qw), lambda bi, g, i: (bi * nb + i, C_NQ // qw + g)),
            pl.BlockSpec((1, 1, ncp, LANES), lambda bi, g, i: (bi, g, 0, 0)),
            pl.BlockSpec((1, 1, HEAD_DIM, ncp), lambda bi, g, i: (bi, g, 0, 0)),
            pl.BlockSpec((hg, 2 * ncp, QB), lambda bi, g, i: (g, 0, 0)),
            pl.BlockSpec((ns, ncp), lambda bi, g, i: (0, 0)),
        ],
        out_specs=[pl.BlockSpec((QB, qw), lambda bi, g, i: (bi * nb + i, g)),
                   pl.BlockSpec((1, 1, QB, LANES), lambda bi, g, i: (bi, g, i, 0))],
        out_shape=[jax.ShapeDtypeStruct((b * t, NSA_W), BF16),
                   jax.ShapeDtypeStruct((b, NSA_KV_HEADS, t, LANES), BF16)],
        compiler_params=_cparams(("parallel", "parallel", "arbitrary")),
        name="nsa_cmp_select",
    )(proj, kc, vc_t, ztab, mmat_t)


def _combine_kernel(oc_ref, os_ref, ow_ref, g_ref, o_ref):
    sg = jax.nn.sigmoid(g_ref[...])
    lane = lax.broadcasted_iota(jnp.int32, (1, LANES), 1)
    upper = lane >= HEAD_DIM
    for p in range(NSA_HEADS // 2):
        acc = None
        for br, ref in enumerate((oc_ref, os_ref, ow_ref)):
            c0 = G_NG + NSA_N_BRANCH * (2 * p) + br
            c1 = G_NG + NSA_N_BRANCH * (2 * p + 1) + br
            gate = jnp.where(upper, sg[:, c1:c1 + 1], sg[:, c0:c0 + 1])
            term = gate * ref[:, p * LANES:(p + 1) * LANES].astype(F32)
            acc = term if acc is None else acc + term
        o_ref[:, p * LANES:(p + 1) * LANES] = acc.astype(o_ref.dtype)


def _nsa_combine(o_cmp, o_slc, o_win, gates, tm=512):
    n = o_cmp.shape[0]
    spec = pl.BlockSpec((tm, NSA_W), lambda i: (i, 0))
    return pl.pallas_call(
        _combine_kernel,
        grid=(n // tm,),
        in_specs=[spec, spec, spec, pl.BlockSpec((tm, D_GATE), lambda i: (i, 0))],
        out_specs=spec,
        out_shape=jax.ShapeDtypeStruct((n, NSA_W), BF16),
        compiler_params=_cparams(("parallel",)),
        name="nsa_combine",
    )(o_cmp, o_slc, o_win, gates)


def _ffn_kernel(te_ref, nu_ref, rows_ref, *refs, fused_norm, chained, sub):
    if fused_norm:
        x_ref, g_ref, wg_ref, wu_ref, wd_ref, o_ref, acc_ref, xn_ref = refs
    elif chained:
        x_ref, wg_ref, wu_ref, wd_ref, _, o_ref, acc_ref = refs
    else:
        x_ref, wg_ref, wu_ref, wd_ref, o_ref, acc_ref = refs
    del te_ref
    i = pl.program_id(0)
    k = pl.program_id(1)
    last = pl.num_programs(1) - 1
    used = i < nu_ref[0]
    tm = acc_ref.shape[0]
    src_ref = xn_ref if fused_norm else x_ref

    @pl.when(jnp.logical_and(used, k == 0))
    def _():
        if fused_norm:
            h = x_ref[...]
            y = h * lax.rsqrt(jnp.mean(h * h, axis=-1, keepdims=True) + RMS_EPS)
            xn_ref[...] = (y * g_ref[...]).astype(BF16)
            acc_ref[...] = h
        else:
            acc_ref[...] = jnp.zeros_like(acc_ref)

    def mlp(r0, nrows):
        x = src_ref[r0:r0 + nrows, :]
        gate = _dot(x, wg_ref[0].astype(BF16))
        up = _dot(x, wu_ref[0].astype(BF16))
        hid = (jax.nn.silu(gate) * up).astype(BF16)
        acc_ref[r0:r0 + nrows, :] += _dot(hid, wd_ref[0].astype(BF16))

    full = rows_ref[i] > tm - sub

    @pl.when(jnp.logical_and(used, full))
    def _():
        mlp(0, tm)

    for piece in range(tm // sub - 1):
        @pl.when(jnp.logical_and(jnp.logical_and(used, jnp.logical_not(full)), rows_ref[i] > piece * sub))
        def _():
            mlp(piece * sub, sub)

    @pl.when(jnp.logical_and(used, k == last))
    def _():
        o_ref[...] = acc_ref[...].astype(o_ref.dtype)

    @pl.when(jnp.logical_and(jnp.logical_not(used), k == last))
    def _():
        o_ref[...] = jnp.zeros_like(o_ref)


def _ffn(x, w_gate, w_up, w_down, tile_expert, n_used, tile_rows, out_dtype, gain=None, tm=512, tf=512,
         out_rows=None, out_tile0=0, out_buf=None, sub=256):
    r, d = x.shape
    nk = D_FF // tf
    n_tiles = r // tm
    fused_norm = gain is not None
    out_rows = r if out_rows is None else out_rows

    def tile(i, nu):
        return jnp.minimum(i, jnp.maximum(nu[0] - 1, 0))

    def kk(i, k, nu):
        return jnp.where(i < nu[0], k, nk - 1)

    in_specs = [pl.BlockSpec((tm, d), lambda i, k, te, nu, rw: (tile(i, nu), 0))]
    args = [x]
    scratch = [pltpu.VMEM((tm, d), F32)]
    if fused_norm:
        in_specs.append(pl.BlockSpec((1, d), lambda i, k, te, nu, rw: (0, 0)))
        args.append(gain.reshape(1, d).astype(F32))
        scratch.append(pltpu.VMEM((tm, d), BF16))
    in_specs += [
        pl.BlockSpec((1, d, tf), lambda i, k, te, nu, rw: (te[tile(i, nu)], 0, kk(i, k, nu))),
        pl.BlockSpec((1, d, tf), lambda i, k, te, nu, rw: (te[tile(i, nu)], 0, kk(i, k, nu))),
        pl.BlockSpec((1, tf, d), lambda i, k, te, nu, rw: (te[tile(i, nu)], kk(i, k, nu), 0)),
    ]
    args += [w_gate, w_up, w_down]
    aliases = {}
    if out_buf is not None:
        in_specs.append(pl.BlockSpec(memory_space=pl.ANY))
        args.append(out_buf)
        aliases = {3 + len(args) - 1: 0}
    return pl.pallas_call(
        functools.partial(_ffn_kernel, fused_norm=fused_norm, chained=out_buf is not None, sub=sub),
        grid_spec=pltpu.PrefetchScalarGridSpec(
            num_scalar_prefetch=3,
            grid=(n_tiles, nk),
            in_specs=in_specs,
            out_specs=pl.BlockSpec((tm, d), lambda i, k, te, nu, rw: (i + out_tile0, 0)),
            scratch_shapes=scratch,
        ),
        out_shape=jax.ShapeDtypeStruct((out_rows, d), out_dtype),
        input_output_aliases=aliases,
        compiler_params=_cparams(("arbitrary", "arbitrary"), FFN_VMEM_LIMIT),
        name="swiglu_ffn",
    )(tile_expert, n_used, tile_rows, *args)


def _router_kernel(l_ref, o_ref):
    lane = lax.broadcasted_iota(jnp.int32, l_ref.shape, 1)
    lf = lane.astype(F32)
    lg = jnp.where(lane < N_EXPERTS, l_ref[...], REMOVED)
    v1 = jnp.max(lg, axis=1, keepdims=True)
    i1 = jnp.min(jnp.where(lg == v1, lf, float(LANES)), axis=1, keepdims=True)
    lg2 = jnp.where(lf == i1, REMOVED, lg)
    v2 = jnp.max(lg2, axis=1, keepdims=True)
    i2 = jnp.min(jnp.where(lg2 == v2, lf, float(LANES)), axis=1, keepdims=True)
    e2 = jnp.exp(v2 - v1)
    den = 1.0 + e2
    p1 = 1.0 / den
    p2 = e2 / den
    out = jnp.where(lane == N_EXPERTS, i1, 0.0)
    out = jnp.where(lane == N_EXPERTS + 1, i2, out)
    out = jnp.where(lane == N_EXPERTS + 2, p1, out)
    out = jnp.where(lane == N_EXPERTS + 3, p2, out)
    o_ref[...] = out


def _router_top2(logits, tm=512):
    n = logits.shape[0]
    spec = pl.BlockSpec((tm, LANES), lambda i: (i, 0))
    return pl.pallas_call(
        _router_kernel,
        grid=(n // tm,),
        in_specs=[spec],
        out_specs=spec,
        out_shape=jax.ShapeDtypeStruct((n, LANES), F32),
        compiler_params=_cparams(("parallel",)),
        name="moe_router_top2",
    )(logits)


def _moe_combine_kernel(h_ref, a_ref, b_ref, top_ref, g_ref, o_ref, *, final_norm):
    p0 = top_ref[:, N_EXPERTS + TOP_K:N_EXPERTS + TOP_K + 1]
    p1 = top_ref[:, N_EXPERTS + TOP_K + 1:N_EXPERTS + TOP_K + 2]
    y = h_ref[...] + p0 * a_ref[...].astype(F32) + p1 * b_ref[...].astype(F32)
    if final_norm:
        y = y * lax.rsqrt(jnp.mean(y * y, axis=-1, keepdims=True) + RMS_EPS) * g_ref[...]
    o_ref[...] = y


def _moe_combine(h, ya, yb, top, gain, final_norm, tm=512):
    n, d = h.shape
    row = pl.BlockSpec((tm, d), lambda i: (i, 0))
    return pl.pallas_call(
        functools.partial(_moe_combine_kernel, final_norm=final_norm),
        grid=(n // tm,),
        in_specs=[row, row, row, pl.BlockSpec((tm, LANES), lambda i: (i, 0)),
                  pl.BlockSpec((1, d), lambda i: (0, 0))],
        out_specs=row,
        out_shape=jax.ShapeDtypeStruct((n, d), F32),
        compiler_params=_cparams(("parallel",)),
        name="moe_combine",
    )(h, ya, yb, top, gain.reshape(1, d).astype(F32))


def _t5_bucket_np(dist):
    n = np.maximum(dist, 0)
    max_exact = REL_BUCKETS // 2
    nf = np.maximum(n, 1).astype(np.float32)
    large = max_exact + (np.log(nf / np.float32(max_exact)) / np.float32(math.log(REL_MAX_DIST / max_exact))
                         * np.float32(REL_BUCKETS - max_exact)).astype(np.int32)
    large = np.minimum(large, REL_BUCKETS - 1)
    return np.where(n < max_exact, n, large).astype(np.int32)


def _band_tables(rel_tab, t):
    q = np.arange(QB)[:, None]
    k = np.arange(QB)[None, :]
    rel = rel_tab.astype(F32) * LOG2E
    toep = jnp.take(rel, jnp.asarray(_t5_bucket_np((q - k) % QB)), axis=0)
    toep = toep.transpose(2, 0, 1)
    before = jnp.asarray(k > q)
    far = rel[REL_BUCKETS - 1, SWA_HEADS:]

    def tiles(tab, fill_prev, extra=()):
        heads = tab.shape[0]
        parts = [jnp.where(before, tab, fill_prev), jnp.where(before, NEG_INF, tab)]
        parts += [jnp.broadcast_to(e, tab.shape) for e in extra]
        return jnp.stack([p.reshape(2, heads // 2 * QB, QB) for p in parts], axis=1)

    tabs_swa = tiles(toep[:SWA_HEADS], NEG_INF)
    tabs_nsa = tiles(toep[SWA_HEADS:] - far[:, None, None], 0.0, extra=[jnp.where(before, 0.0, NEG_INF)])
    ncp = t // NSA_CMP_STRIDE
    m = np.arange(-9, 7)[:, None]
    qr = np.arange(QB)[None, :]
    d = qr - NSA_CMP_STRIDE * m - (NSA_CMP_LEN - 1)
    vals = jnp.take(rel[:, SWA_HEADS:], jnp.asarray(_t5_bucket_np(np.clip(d, 0, None))), axis=0)
    vals = vals.transpose(2, 0, 1) - far[:, None, None]
    band = jnp.where(jnp.asarray(d >= 0), jnp.where(jnp.asarray(d < REL_MAX_DIST), vals, 0.0), NEG_INF)
    ztab = jnp.concatenate([jnp.zeros((NSA_HEADS, ncp - 9, QB), F32), band,
                            jnp.full((NSA_HEADS, ncp - 7, QB), NEG_INF, F32)], axis=1)
    return tabs_swa, tabs_nsa, ztab


def _selection_matrices(t):
    ncp = t // NSA_CMP_STRIDE
    ns = t // NSA_SEL_LEN
    per = NSA_SEL_LEN // NSA_CMP_STRIDE
    ratio = NSA_CMP_LEN // NSA_CMP_STRIDE
    mmat_t = np.zeros((ns, ncp), np.float32)
    for n in range(ncp - 1):
        for j in range(ratio):
            mmat_t[(n + j) // per, n] += 1.0
    emat_t = (np.arange(t)[:, None] // NSA_SEL_LEN == np.arange(LANES)[None, :]).astype(np.float32)
    return jnp.asarray(mmat_t, BF16), (jnp.asarray(emat_t.T, BF16), jnp.asarray(emat_t, BF16))


def _dup(w):
    d = w.shape[0]
    w = w.reshape(d, -1, 1, HEAD_DIM)
    return jnp.broadcast_to(w, (d, w.shape[1], 2, HEAD_DIM)).reshape(d, -1)


def _prep_in_weights(w_in_l):
    sizes = (FOX_W, FOX_W, FOX_W, FOX_HEADS, SWA_W, SWA_KV_W, SWA_KV_W,
             NSA_W, NSA_KV_W, NSA_KV_W, NSA_KV_W, NSA_KV_W, NSA_KV_W, NSA_KV_W, NSA_HEADS * NSA_N_BRANCH)
    splits = [int(s) for s in np.cumsum(sizes)[:-1]]
    (fq, fk, fv, ff, sq, sk, sv, nq, nkc, nvc, nks, nvs, nkw, nvw, ng) = jnp.split(w_in_l, splits, axis=-1)
    scale = HEAD_DIM ** -0.5 * LOG2E
    cols = [fq * scale, fk, fv, sq * scale, nq * scale, _dup(sk), _dup(nks), _dup(nkw), sv, nvs, nvw, nkc, nvc]
    cols.append(jnp.zeros((w_in_l.shape[0], D_PROJ - sum(c.shape[1] for c in cols)), w_in_l.dtype))
    w_proj = jnp.concatenate(cols, axis=-1).astype(BF16)
    pad = jnp.zeros((w_in_l.shape[0], D_GATE - FOX_HEADS - NSA_HEADS * NSA_N_BRANCH), w_in_l.dtype)
    w_gate = jnp.concatenate([ff, ng, pad], axis=-1).astype(BF16)
    return w_proj, w_gate


def _compress_inputs(proj, b, t):
    g = NSA_KV_HEADS
    nck = t // NSA_CMP_STRIDE
    x = proj[:, C_NKC:C_NKC + 2 * NSA_KV_W].reshape(b, nck, NSA_CMP_STRIDE, 2, g, HEAD_DIM)
    return x.transpose(3, 0, 4, 1, 2, 5).reshape(2, b * g, nck, NSA_CMP_STRIDE * HEAD_DIM)


def _mixer(h, gain, b, t, w_in_l, forget_bias, sinks, cmp_pos, cmp_w1, cmp_b1, cmp_w2, cmp_b2, tables):
    tabs_swa, tabs_nsa, ztab, mmat_t, emats = tables
    w_proj, w_gate = _prep_in_weights(w_in_l)
    proj, gates = _in_proj(h, gain, w_proj, w_gate)
    fv_t = proj[:, C_FV:C_FV + FOX_W].reshape(b, t, FOX_W).transpose(0, 2, 1)
    v_t = proj[:, C_V:C_V + 3 * NSA_KV_W].reshape(b, t, 3 * NSA_KV_W).transpose(0, 2, 1)

    qx, kx = _fox_decay_operands(gates, forget_bias, b, t)
    o_fox = _fox_attention(proj, fv_t, qx, kx, b, t)

    o_swa = _band_attention(proj, v_t, tabs_swa, b, t, mode="swa", hg=SWA_GROUP,
                            c_q=C_SQ, c_k=C_SK, v_blk=0, sinks=sinks)

    flat = _compress_inputs(proj, b, t)
    pos = cmp_pos.reshape(2, 1, NSA_CMP_LEN * HEAD_DIM).astype(F32)
    w2d = jnp.concatenate([cmp_w2, cmp_w2], axis=-1).astype(BF16)
    b2d = jnp.concatenate([cmp_b2, cmp_b2], axis=-1).reshape(2, 1, LANES).astype(F32)
    kvc = _compress(flat, pos, cmp_w1.astype(BF16), cmp_b1.reshape(2, 1, NSA_CMP_HIDDEN).astype(F32), w2d, b2d)
    ncp = t // NSA_CMP_STRIDE
    kvc = kvc.reshape(2, b, NSA_KV_HEADS, ncp, LANES)
    vc_t = kvc[1, :, :, :, :HEAD_DIM].transpose(0, 1, 3, 2)
    o_cmp, sel = _cmp_attention(proj, kvc[0], vc_t, ztab, mmat_t, b, t)
    o_slc = _band_attention(proj, v_t, tabs_nsa, b, t, mode="sel", hg=NSA_GROUP,
                            c_q=C_NQ, c_k=C_NKS, v_blk=2, sel=sel, emats=emats)
    o_win = _band_attention(proj, v_t, tabs_nsa, b, t, mode="win", hg=NSA_GROUP,
                            c_q=C_NQ, c_k=C_NKW, v_blk=4)
    o_nsa = _nsa_combine(o_cmp, o_slc, o_win, gates)
    return o_fox, o_swa, o_nsa


def _moe(hn, h, router, w_gate, w_up, w_down, gain, final_norm, tm=1024, tf=512):
    n, d = hn.shape
    w_r = jnp.zeros((d, LANES), BF16).at[:, :N_EXPERTS].set(router.astype(BF16))
    logits = _matmul([hn], [w_r], F32, tn=LANES, name="router_logits")
    top = _router_top2(logits)
    e_idx = top[:, N_EXPERTS:N_EXPERTS + TOP_K].astype(jnp.int32)
    e_flat = e_idx.reshape(-1)
    onehot = (e_flat[:, None] == jnp.arange(N_EXPERTS)[None, :]).astype(jnp.int32)
    csum = jnp.cumsum(onehot, axis=0)
    counts = csum[-1]
    rank = jnp.take_along_axis(csum, e_flat[:, None], axis=1)[:, 0] - 1
    padded = ((counts + tm - 1) // tm) * tm
    ends = jnp.cumsum(padded)
    starts = ends - padded
    dest = starts[e_flat] + rank
    r_pad = n * TOP_K + N_EXPERTS * tm
    src_tok = jnp.zeros((r_pad,), jnp.int32).at[dest].set(jnp.arange(n * TOP_K, dtype=jnp.int32) // TOP_K)
    tile_start = jnp.arange(r_pad // tm, dtype=jnp.int32) * tm
    tile_expert = jnp.minimum(jnp.sum(tile_start[:, None] >= ends[None, :], axis=1), N_EXPERTS - 1).astype(jnp.int32)
    n_used = (ends[-1:] // tm).astype(jnp.int32)
    tile_rows = jnp.clip((starts + counts)[tile_expert] - tile_start, 0, tm).astype(jnp.int32)
    n_tiles = r_pad // tm
    per = n_tiles // MOE_CHUNKS
    y = None
    for c in range(MOE_CHUNKS):
        xs = hn.at[src_tok[c * per * tm:(c + 1) * per * tm]].get(mode="promise_in_bounds")
        used_c = jnp.clip(n_used - c * per, 0, per)
        y = _ffn(xs, w_gate, w_up, w_down, tile_expert[c * per:(c + 1) * per], used_c,
                 tile_rows[c * per:(c + 1) * per], BF16, tm=tm, tf=tf,
                 out_rows=r_pad, out_tile0=c * per, out_buf=y)
    dest = dest.reshape(n, TOP_K)
    ya = y.at[dest[:, 0]].get(mode="promise_in_bounds")
    yb = y.at[dest[:, 1]].get(mode="promise_in_bounds")
    return _moe_combine(h, ya, yb, top, gain, final_norm)


def kernel(x, attn_norm, w_in, fox_forget_bias, swa_sinks, nsa_cmp_pos, nsa_cmp_w1, nsa_cmp_b1, nsa_cmp_w2,
           nsa_cmp_b2, w_out, rel_bias_table, ffn_norm, dense_w_gate, dense_w_up, dense_w_down, moe_router,
           moe_w_gate, moe_w_up, moe_w_down, final_norm):
    b, t, d = x.shape
    n = b * t
    depth = w_in.shape[0]
    tables = _band_tables(rel_bias_table, t) + _selection_matrices(t)
    h = x.reshape(n, d)
    for layer in range(depth):
        o_fox, o_swa, o_nsa = _mixer(h, attn_norm[layer], b, t, w_in[layer], fox_forget_bias[layer],
                                     swa_sinks[layer], nsa_cmp_pos[layer], nsa_cmp_w1[layer], nsa_cmp_b1[layer],
                                     nsa_cmp_w2[layer], nsa_cmp_b2[layer], tables)
        wo = w_out[layer].astype(BF16)
        h = _matmul([o_fox, o_swa, o_nsa], [wo[:FOX_W], wo[FOX_W:FOX_W + SWA_W], wo[FOX_W + SWA_W:]],
                    F32, residual=h, tn=1024, name="out_proj")
        i = layer // 2
        last = layer == depth - 1
        if layer % 2 == 0:
            tm = 512
            zeros = jnp.zeros((n // tm,), jnp.int32)
            h = _ffn(h, dense_w_gate[i][None].astype(BF16), dense_w_up[i][None].astype(BF16),
                     dense_w_down[i][None].astype(BF16), zeros, jnp.full((1,), n // tm, jnp.int32),
                     jnp.full((n // tm,), tm, jnp.int32), F32, gain=ffn_norm[layer], tm=tm)
            if last:
                h = _rmsnorm(h, final_norm, F32)
        else:
            hn = _rmsnorm(h, ffn_norm[layer], BF16)
            h = _moe(hn, h, moe_router[i], moe_w_gate[i], moe_w_up[i], moe_w_down[i], final_norm, last)
    return h.reshape(b, t, d)
```

```python
import functools
import math

import numpy as np
import jax
import jax.numpy as jnp
from jax import lax
from jax.experimental import pallas as pl
from jax.experimental.pallas import tpu as pltpu

F32 = jnp.float32
BF16 = jnp.bfloat16

D_MODEL = 2048
HEAD_DIM = 64
FOX_HEADS = 8
FOX_W = FOX_HEADS * HEAD_DIM
SWA_HEADS = 8
SWA_KV_HEADS = 2
SWA_GROUP = SWA_HEADS // SWA_KV_HEADS
SWA_W = SWA_HEADS * HEAD_DIM
SWA_KV_W = SWA_KV_HEADS * HEAD_DIM
SWA_WINDOW = 128
NSA_HEADS = 16
NSA_KV_HEADS = 2
NSA_GROUP = NSA_HEADS // NSA_KV_HEADS
NSA_W = NSA_HEADS * HEAD_DIM
NSA_KV_W = NSA_KV_HEADS * HEAD_DIM
NSA_CMP_LEN = 32
NSA_CMP_STRIDE = 16
NSA_CMP_HIDDEN = 256
NSA_SEL_LEN = 64
NSA_SEL_TOPK = 16
NSA_WINDOW = 512
NSA_N_BRANCH = 3
FORCE_SCORE = 1e9
REL_BUCKETS = 32
REL_MAX_DIST = 128
D_FF = 5632
N_EXPERTS = 8
TOP_K = 2
RMS_EPS = 1e-6
NEG_INF = -1e30
M_INIT = -1e29
REMOVED = -3e38
LOG2E = 1.4426950408889634
ONES_ROWS = 16
BAND_TILES = {"swa": 8, "win": 4, "sel": 2}
CMP_TILES = 4
MOE_CHUNKS = 4

LANES = 128
QB = 128
VMEM_LIMIT = 56 * 1024 * 1024
FFN_VMEM_LIMIT = 60 * 1024 * 1024

C_FQ, C_FK, C_FV = 0, 512, 1024
C_SQ = 1536
C_NQ = 2048
C_SK, C_NKS, C_NKW = 3072, 3328, 3584
C_V = 3840
C_NKC, C_NVC = 4224, 4352
D_PROJ = 4608
PROJ_TN = 768
G_NG = FOX_HEADS
D_GATE = 128


def _cparams(sem, vmem=VMEM_LIMIT):
    return pltpu.CompilerParams(dimension_semantics=sem, vmem_limit_bytes=vmem)


def _dot(a, b):
    return jnp.dot(a, b, preferred_element_type=F32)


def _dot_nt(a, b):
    return lax.dot_general(a, b, (((1,), (1,)), ((), ())), preferred_element_type=F32)


def _split3(x):
    hi = x.astype(BF16)
    r1 = x - hi.astype(F32)
    mid = r1.astype(BF16)
    lo = (r1 - mid.astype(F32)).astype(BF16)
    return hi, mid, lo


def _pipeline3(n, stage1, stage2, stage3):
    first, second, outs = {}, {}, []
    for step in range(n + 2):
        if step < n:
            first[step] = stage1(step)
        if 0 <= step - 1 < n:
            second[step - 1] = stage2(step - 1, first.pop(step - 1))
        if 0 <= step - 2 < n:
            outs.append(stage3(step - 2, *second.pop(step - 2)))
    return outs


def _rmsnorm_kernel(x_ref, g_ref, o_ref):
    x = x_ref[...]
    y = x * lax.rsqrt(jnp.mean(x * x, axis=-1, keepdims=True) + RMS_EPS)
    o_ref[...] = (y * g_ref[...]).astype(o_ref.dtype)


def _rmsnorm(h, g, out_dtype, tm=512):
    n, d = h.shape
    return pl.pallas_call(
        _rmsnorm_kernel,
        grid=(n // tm,),
        in_specs=[pl.BlockSpec((tm, d), lambda i: (i, 0)),
                  pl.BlockSpec((1, d), lambda i: (0, 0))],
        out_specs=pl.BlockSpec((tm, d), lambda i: (i, 0)),
        out_shape=jax.ShapeDtypeStruct((n, d), out_dtype),
        compiler_params=_cparams(("parallel",)),
        name="rmsnorm",
    )(h, g.reshape(1, d).astype(F32))


def _mm_kernel(*refs, n_in, has_res):
    o_ref = refs[-1]
    acc = None
    for a in range(n_in):
        d = _dot(refs[a][...], refs[n_in + a][...])
        acc = d if acc is None else acc + d
    if has_res:
        acc = acc + refs[2 * n_in][...]
    o_ref[...] = acc.astype(o_ref.dtype)


def _matmul(xs, ws, out_dtype, residual=None, tm=1024, tn=256, name="matmul"):
    n = xs[0].shape[0]
    m = ws[0].shape[1]
    tm = min(tm, n)
    tn = min(tn, m)
    in_specs = [pl.BlockSpec((tm, x.shape[1]), lambda i, j: (i, 0)) for x in xs]
    in_specs += [pl.BlockSpec((w.shape[0], tn), lambda i, j: (0, j)) for w in ws]
    args = list(xs) + list(ws)
    if residual is not None:
        in_specs.append(pl.BlockSpec((tm, tn), lambda i, j: (i, j)))
        args.append(residual)
    return pl.pallas_call(
        functools.partial(_mm_kernel, n_in=len(xs), has_res=residual is not None),
        grid=(n // tm, m // tn),
        in_specs=in_specs,
        out_specs=pl.BlockSpec((tm, tn), lambda i, j: (i, j)),
        out_shape=jax.ShapeDtypeStruct((n, m), out_dtype),
        compiler_params=_cparams(("parallel", "arbitrary")),
        name=name,
    )(*args)


def _in_proj_kernel(h_ref, g_ref, w_ref, wgate_ref, proj_ref, gates_ref, xn_ref):
    @pl.when(pl.program_id(1) == 0)
    def _():
        h = h_ref[...]
        y = h * lax.rsqrt(jnp.mean(h * h, axis=-1, keepdims=True) + RMS_EPS)
        xn_ref[...] = (y * g_ref[...]).astype(BF16)
        gates_ref[...] = _dot(xn_ref[...], wgate_ref[...])

    proj_ref[...] = _dot(xn_ref[...], w_ref[...]).astype(proj_ref.dtype)


def _in_proj(h, gain, w_proj, w_gate, tm=1024, tn=PROJ_TN):
    n, d = h.shape
    return pl.pallas_call(
        _in_proj_kernel,
        grid=(n // tm, D_PROJ // tn),
        in_specs=[pl.BlockSpec((tm, d), lambda i, j: (i, 0)),
                  pl.BlockSpec((1, d), lambda i, j: (0, 0)),
                  pl.BlockSpec((d, tn), lambda i, j: (0, j)),
                  pl.BlockSpec((d, D_GATE), lambda i, j: (0, 0))],
        out_specs=[pl.BlockSpec((tm, tn), lambda i, j: (i, j)),
                   pl.BlockSpec((tm, D_GATE), lambda i, j: (i, 0))],
        out_shape=[jax.ShapeDtypeStruct((n, D_PROJ), BF16), jax.ShapeDtypeStruct((n, D_GATE), F32)],
        scratch_shapes=[pltpu.VMEM((tm, d), BF16)],
        compiler_params=_cparams(("parallel", "arbitrary")),
        name="in_proj",
    )(h, gain.reshape(1, d).astype(F32), w_proj, w_gate)


def _cumsum_kernel(g_ref, b_ref, pq_ref, pk_ref, oq_ref, ok_ref, qx_ref, kx_ref, carry_ref, *, tc):
    @pl.when(pl.program_id(1) == 0)
    def _():
        carry_ref[...] = jnp.zeros_like(carry_ref)

    z = g_ref[...] + b_ref[...]
    log_f = jnp.minimum(z, 0.0) - jnp.log1p(jnp.exp(-jnp.abs(z)))
    row = lax.broadcasted_iota(jnp.int32, (tc, tc), 0)
    col = lax.broadcasted_iota(jnp.int32, (tc, tc), 1)
    tri = jnp.where(col <= row, 1.0, 0.0).astype(BF16)
    hi, mid, lo = _split3(log_f)
    c = _dot(tri, hi) + _dot(tri, mid) + _dot(tri, lo) + carry_ref[...]
    carry_ref[...] = c[tc - 1:tc, :]
    parts = jnp.concatenate(_split3(c * LOG2E), axis=1)
    qx_ref[...] = (_dot(parts, pq_ref[...]) + oq_ref[...]).astype(qx_ref.dtype)
    kx_ref[...] = (_dot(parts, pk_ref[...]) + ok_ref[...]).astype(kx_ref.dtype)


def _fox_decay_operands(gates, forget_bias, b, t, tc=256):
    nt = t // tc
    wide = FOX_HEADS * LANES
    bias = jnp.zeros((1, D_GATE), F32).at[0, :FOX_HEADS].set(forget_bias.astype(F32))
    pq = np.zeros((3 * D_GATE, wide), np.float32)
    pk = np.zeros((3 * D_GATE, wide), np.float32)
    oq = np.zeros((1, wide), np.float32)
    ok = np.zeros((1, wide), np.float32)
    for h in range(FOX_HEADS):
        for part in range(3):
            pq[part * D_GATE + h, h * LANES + part] = 1.0
            pk[part * D_GATE + h, h * LANES + 3 + part] = -1.0
            oq[0, h * LANES + 3 + part] = 1.0
            ok[0, h * LANES + part] = 1.0
    const = lambda shape: pl.BlockSpec(shape, lambda bi, ti: (0, 0))
    out_spec = pl.BlockSpec((tc, wide), lambda bi, ti: (bi * nt + ti, 0))
    return pl.pallas_call(
        functools.partial(_cumsum_kernel, tc=tc),
        grid=(b, nt),
        in_specs=[pl.BlockSpec((tc, D_GATE), lambda bi, ti: (bi * nt + ti, 0)),
                  const((1, D_GATE)), const((3 * D_GATE, wide)), const((3 * D_GATE, wide)),
                  const((1, wide)), const((1, wide))],
        out_specs=[out_spec, out_spec],
        out_shape=[jax.ShapeDtypeStruct((b * t, wide), BF16), jax.ShapeDtypeStruct((b * t, wide), BF16)],
        scratch_shapes=[pltpu.VMEM((1, D_GATE), F32)],
        compiler_params=_cparams(("parallel", "arbitrary")),
        name="fox_cumsum",
    )(gates, bias, jnp.asarray(pq, BF16), jnp.asarray(pk, BF16), jnp.asarray(oq), jnp.asarray(ok))


def _fox_kernel(qi_ref, kj_ref, q_ref, k_ref, vt_ref, qx_ref, kx_ref, o_ref, m_sc, acc_sc, *, tq):
    i = qi_ref[pl.program_id(1)]
    j = kj_ref[pl.program_id(1)]
    lane = lax.broadcasted_iota(jnp.int32, (1, LANES), 1)
    upper = lane >= HEAD_DIM

    @pl.when(j == 0)
    def _():
        m_sc[...] = jnp.full_like(m_sc, M_INIT)
        acc_sc[...] = jnp.zeros_like(acc_sc)

    def step(diagonal):
        if diagonal:
            key = lax.broadcasted_iota(jnp.int32, (tq, tq), 0)
            qry = lax.broadcasted_iota(jnp.int32, (tq, tq), 1)
            causal = key <= qry
        ones = jnp.ones((ONES_ROWS, tq), BF16)

        def qk(h):
            p, a = divmod(h, 2)
            q2 = q_ref[:, p * LANES:(p + 1) * LANES]
            k2 = k_ref[:, p * LANES:(p + 1) * LANES]
            zero = jnp.zeros_like(q2)
            qa = jnp.where(upper, q2, zero) if a else jnp.where(upper, zero, q2)
            q_aug = jnp.concatenate([qa, qx_ref[:, h * LANES:(h + 1) * LANES]], axis=1)
            k_aug = jnp.concatenate([k2, kx_ref[:, h * LANES:(h + 1) * LANES]], axis=1)
            s = _dot_nt(k_aug, q_aug)
            if diagonal:
                s = jnp.where(causal, s, NEG_INF)
            return s, jnp.max(s, axis=0, keepdims=True)

        def softmax(h, scored):
            s, s_max = scored
            m_prev = m_sc[h]
            m_new = jnp.maximum(m_prev, s_max)
            m_sc[h] = m_new
            return jnp.exp2(m_prev - m_new), jnp.exp2(s - m_new).astype(BF16)

        def pv(h, alpha, pr):
            vt = jnp.concatenate([vt_ref[0, h * HEAD_DIM:(h + 1) * HEAD_DIM, :], ones], axis=0)
            acc_sc[h] = alpha * acc_sc[h] + _dot(vt, pr)

        _pipeline3(FOX_HEADS, qk, softmax, pv)

    @pl.when(j < i)
    def _():
        step(False)

    @pl.when(j == i)
    def _():
        step(True)
        for p in range(FOX_HEADS // 2):
            outs = []
            for h in (2 * p, 2 * p + 1):
                acc = acc_sc[h]
                outs.append(acc[:HEAD_DIM] * (1.0 / acc[HEAD_DIM:HEAD_DIM + 1]))
            o_ref[:, p * LANES:(p + 1) * LANES] = jnp.concatenate(outs, axis=0).T.astype(o_ref.dtype)


def _fox_attention(proj, v_t, qx, kx, b, t, tq=512):
    nt = t // tq
    wide = FOX_HEADS * LANES
    pairs = [(i, j) for i in range(nt) for j in range(i + 1)]
    qi = jnp.asarray([p[0] for p in pairs], jnp.int32)
    kj = jnp.asarray([p[1] for p in pairs], jnp.int32)
    return pl.pallas_call(
        functools.partial(_fox_kernel, tq=tq),
        grid_spec=pltpu.PrefetchScalarGridSpec(
            num_scalar_prefetch=2,
            grid=(b, len(pairs)),
            in_specs=[
                pl.BlockSpec((tq, FOX_W), lambda bi, s, qi, kj: (bi * nt + qi[s], C_FQ // FOX_W)),
                pl.BlockSpec((tq, FOX_W), lambda bi, s, qi, kj: (bi * nt + kj[s], C_FK // FOX_W)),
                pl.BlockSpec((1, FOX_W, tq), lambda bi, s, qi, kj: (bi, 0, kj[s])),
                pl.BlockSpec((tq, wide), lambda bi, s, qi, kj: (bi * nt + qi[s], 0)),
                pl.BlockSpec((tq, wide), lambda bi, s, qi, kj: (bi * nt + kj[s], 0)),
            ],
            out_specs=pl.BlockSpec((tq, FOX_W), lambda bi, s, qi, kj: (bi * nt + qi[s], 0)),
            scratch_shapes=[pltpu.VMEM((FOX_HEADS, 1, tq), F32),
                            pltpu.VMEM((FOX_HEADS, HEAD_DIM + ONES_ROWS, tq), F32)],
        ),
        out_shape=jax.ShapeDtypeStruct((b * t, FOX_W), BF16),
        compiler_params=_cparams(("parallel", "arbitrary")),
        name="fox_attention",
    )(qi, kj, proj, proj, v_t, qx, kx)


def _stack_heads(q_ref, hg, r0=0):
    lane = lax.broadcasted_iota(jnp.int32, (1, LANES), 1)
    upper = lane >= HEAD_DIM
    qs = []
    for p in range(hg // 2):
        q2 = q_ref[r0:r0 + QB, p * LANES:(p + 1) * LANES]
        qs.append(jnp.where(upper, jnp.zeros_like(q2), q2))
        qs.append(jnp.where(upper, q2, jnp.zeros_like(q2)))
    return jnp.concatenate(qs, axis=0)


def _store_heads(o_ref, o_t, hg, r0=0):
    for p in range(hg // 2):
        pair = jnp.concatenate([o_t[:, (2 * p) * QB:(2 * p + 1) * QB],
                                o_t[:, (2 * p + 1) * QB:(2 * p + 2) * QB]], axis=0)
        o_ref[r0:r0 + QB, p * LANES:(p + 1) * LANES] = pair.T.astype(o_ref.dtype)


def _band_kernel(*refs, mode, hg, chunk, tiles):
    for sub in range(tiles):
        _band_tile(*refs, mode=mode, hg=hg, chunk=chunk, i=pl.program_id(2) * tiles + sub, r0=sub * QB)


def _band_tile(*refs, mode, hg, chunk, i, r0):
    if mode == "swa":
        q_ref, k_ref, vt_ref, tab_ref, sink_ref, o_ref = refs
    elif mode == "sel":
        (q_ref, k_ref, vt_ref, tab_ref, sel_ref, e_ref, et_ref, o_ref,
         sa_ref, sb_ref, ma_ref, mb_ref, m_ref, acc_ref) = refs
    else:
        q_ref, k_ref, vt_ref, tab_ref, o_ref = refs
    g = pl.program_id(1)
    cols = hg * QB
    qstack = _stack_heads(q_ref, hg, r0)
    eye = jnp.where(lax.broadcasted_iota(jnp.int32, (QB, QB), 0) == lax.broadcasted_iota(jnp.int32, (QB, QB), 1),
                    1.0, 0.0).astype(BF16)

    def k_block(blk):
        return k_ref[pl.ds(pl.multiple_of(blk * QB, QB), QB), :]

    def vt_rows(start, size):
        return jnp.concatenate([vt_ref[0, :, pl.ds(start, size)], jnp.ones((ONES_ROWS, size), BF16)], axis=0)

    def vt_block(blk):
        return vt_rows(pl.multiple_of(blk * QB, QB), QB)

    def with_table(table):
        return jnp.concatenate([qstack, table.astype(BF16)], axis=1)

    def grp(x, j):
        return x[j * GW:(j + 1) * GW]

    def lanes(x, j):
        return x[:, j * GW:(j + 1) * GW]

    GW = cols if mode == "sel" else 2 * QB
    n_grp = cols // GW
    ip = jnp.maximum(i - 1, 0)
    t_prev = tab_ref[0, 0]
    if mode == "sel":
        sel_q = sel_ref[0, 0, r0:r0 + QB, :]
        allowed = _dot(sel_q, e_ref[:, pl.ds(pl.multiple_of(ip * QB, QB), QB)])
        t_prev = jnp.where(jnp.concatenate([allowed] * hg, axis=0) > 0.5, t_prev, NEG_INF)
    t_prev = jnp.where(i > 0, t_prev, NEG_INF)
    q_prev = with_table(t_prev)
    q_cur = with_table(tab_ref[0, 1])
    k_prev = jnp.concatenate([k_block(ip), eye], axis=1)
    k_cur = jnp.concatenate([k_block(i), eye], axis=1)
    vt_near = jnp.concatenate([vt_block(ip), vt_block(i)], axis=1)

    if mode == "swa":
        head = lax.broadcasted_iota(jnp.int32, (1, cols), 1) // QB
        sink = jnp.zeros((1, cols), F32)
        for h in range(hg):
            sink = jnp.where(head == h, sink_ref[g * hg + h], sink)

    elif mode == "win":
        n_far = NSA_WINDOW // QB - 1
        backs = range(n_far + 1, 1, -1)
        onehot = jnp.concatenate([eye, jnp.zeros(((n_far - 1) * QB, QB), BF16)], axis=0)
        k_far = jnp.concatenate(
            [jnp.concatenate([k_block(jnp.maximum(i - bk, 0)) for bk in backs], axis=0), onehot], axis=1)
        vt_far = jnp.concatenate([vt_block(jnp.maximum(i - bk, 0)) for bk in backs], axis=1)
        q_tri = with_table(tab_ref[0, 2])
        in_seq = lax.broadcasted_iota(jnp.int32, (n_far * QB, 1), 0) >= (n_far + 1 - i) * QB

    else:
        blk = lax.broadcasted_iota(jnp.int32, (QB, LANES), 1)
        pick = jnp.logical_and(sel_q.astype(F32) > 0.5, blk < 2 * (i - 1))
        sel_bias = jnp.where(pick, 0.0, NEG_INF)
        q_far = with_table(jnp.concatenate([sel_bias] * hg, axis=0))
        per = chunk // QB
        n_chunks = (jnp.maximum(i - 1, 0) + per - 1) // per
        def far_scores(c):
            off = pl.multiple_of(c * chunk, chunk)
            k_rows = jnp.concatenate([k_ref[pl.ds(off, chunk), :], et_ref[pl.ds(off, chunk), :]], axis=1)
            return _dot_nt(k_rows, q_far)

        def put_scores(c, s_ref, smax_ref):
            s = far_scores(c)
            s_ref[...] = s
            smax_ref[...] = jnp.max(s, axis=0, keepdims=True)

        def sweep(c, cur, nxt):
            put_scores(jnp.minimum(c + 1, n_chunks - 1), *nxt)
            s_ref, smax_ref = cur
            m_old = m_ref[...]
            m_new = jnp.maximum(m_old, smax_ref[...])
            pf = jnp.exp2(s_ref[...] - m_new).astype(BF16)
            acc_ref[...] = (jnp.exp2(m_old - m_new) * acc_ref[...]
                            + _dot(vt_rows(pl.multiple_of(c * chunk, chunk), chunk), pf))
            m_ref[...] = m_new

        buf_a, buf_b = (sa_ref, ma_ref), (sb_ref, mb_ref)

        def body(quad, carry):
            for u in range(4):
                sweep(4 * quad + u, *((buf_a, buf_b) if u % 2 == 0 else (buf_b, buf_a)))
            return carry

        m_ref[...] = jnp.full((1, cols), M_INIT, F32)
        acc_ref[...] = jnp.zeros((HEAD_DIM + ONES_ROWS, cols), F32)
        put_scores(0, *buf_a)
        lax.fori_loop(0, n_chunks // 4, body, 0)
        done = (n_chunks // 4) * 4

        @pl.when(n_chunks % 4 >= 2)
        def _():
            sweep(done, buf_a, buf_b)
            sweep(done + 1, buf_b, buf_a)

        @pl.when(n_chunks % 2 == 1)
        def _():
            sweep(n_chunks - 1, buf_a, buf_b)

        m_far = m_ref[...]
        acc_far = acc_ref[...]

    def near_scores(j):
        parts = [_dot_nt(k_prev, grp(q_prev, j)), _dot_nt(k_cur, grp(q_cur, j))]
        if mode == "win":
            parts.append(jnp.where(in_seq, _dot_nt(k_far, grp(q_tri, j)), NEG_INF))
        return parts

    def near_softmax(j, parts):
        m = jnp.max(parts[0], axis=0, keepdims=True)
        for s in parts[1:]:
            m = jnp.maximum(m, jnp.max(s, axis=0, keepdims=True))
        if mode == "swa":
            m = jnp.maximum(m, lanes(sink, j))
        if mode == "sel":
            m = jnp.maximum(m, lanes(m_far, j))
        return m, [jnp.exp2(s - m).astype(BF16) for s in parts]

    def near_output(j, m, probs):
        acc = _dot(vt_near, jnp.concatenate(probs[:2], axis=0))
        if mode == "win":
            acc = acc + _dot(vt_far, probs[2])
        if mode == "sel":
            acc = jnp.exp2(lanes(m_far, j) - m) * lanes(acc_far, j) + acc
        den = acc[HEAD_DIM:HEAD_DIM + 1]
        if mode == "swa":
            den = den + jnp.exp2(lanes(sink, j) - m)
        o_t = acc[:HEAD_DIM] * (1.0 / den)
        for pp in range(GW // (2 * QB)):
            pair = jnp.concatenate([o_t[:, 2 * pp * QB:(2 * pp + 1) * QB],
                                    o_t[:, (2 * pp + 1) * QB:(2 * pp + 2) * QB]], axis=0)
            c0 = (j * (GW // (2 * QB)) + pp) * LANES
            o_ref[r0:r0 + QB, c0:c0 + LANES] = pair.T.astype(o_ref.dtype)

    _pipeline3(n_grp, near_scores, near_softmax, near_output)


def _band_attention(proj, v_t, tabs, b, t, *, mode, hg, c_q, c_k, v_blk, sinks=None, sel=None, emats=None):
    tiles = BAND_TILES[mode]
    nb = t // (QB * tiles)
    n_groups = 2
    qw = hg * HEAD_DIM
    chunk = min(512, t)
    in_specs = [
        pl.BlockSpec((QB * tiles, qw), lambda bi, g, i: (bi * nb + i, c_q // qw + g)),
        pl.BlockSpec((t, LANES), lambda bi, g, i: (bi, c_k // LANES + g)),
        pl.BlockSpec((1, HEAD_DIM, t), lambda bi, g, i: (bi, v_blk + g, 0)),
        pl.BlockSpec((1,) + tabs.shape[1:], lambda bi, g, i: (g, 0, 0, 0)),
    ]
    args = [proj, proj, v_t, tabs]
    scratch = []
    if mode == "swa":
        in_specs.append(pl.BlockSpec(memory_space=pltpu.SMEM))
        args.append(sinks.astype(F32) * LOG2E)
    if mode == "sel":
        emat, emat_t = emats
        in_specs.append(pl.BlockSpec((1, 1, QB * tiles, LANES), lambda bi, g, i: (bi, g, i, 0)))
        in_specs.append(pl.BlockSpec((LANES, t), lambda bi, g, i: (0, 0)))
        in_specs.append(pl.BlockSpec((t, LANES), lambda bi, g, i: (0, 0)))
        args += [sel, emat, emat_t]
        scratch = [pltpu.VMEM((chunk, hg * QB), F32), pltpu.VMEM((chunk, hg * QB), F32),
                   pltpu.VMEM((1, hg * QB), F32), pltpu.VMEM((1, hg * QB), F32),
                   pltpu.VMEM((1, hg * QB), F32), pltpu.VMEM((HEAD_DIM + ONES_ROWS, hg * QB), F32)]
    return pl.pallas_call(
        functools.partial(_band_kernel, mode=mode, hg=hg, chunk=chunk, tiles=tiles),
        grid=(b, n_groups, nb),
        in_specs=in_specs,
        out_specs=pl.BlockSpec((QB * tiles, qw), lambda bi, g, i: (bi * nb + i, g)),
        out_shape=jax.ShapeDtypeStruct((b * t, n_groups * qw), BF16),
        scratch_shapes=scratch,
        compiler_params=_cparams(("parallel", "parallel", "arbitrary")),
        name="band_" + mode,
    )(*args)


def _compress_kernel(x_ref, pos_ref, w1_ref, b1_ref, w2_ref, b2_ref, o_ref, *, nck):
    half = NSA_CMP_STRIDE * HEAD_DIM
    c = x_ref[0, 0].astype(F32)
    first = _dot((c + pos_ref[0, :, :half]).astype(BF16), w1_ref[0, :half, :])
    second = _dot((c + pos_ref[0, :, half:]).astype(BF16), w1_ref[0, half:, :])
    hid = jax.nn.gelu(first + pltpu.roll(second, nck - 1, 0) + b1_ref[0])
    o_ref[0, 0] = (_dot(hid.astype(BF16), w2_ref[0]) + b2_ref[0]).astype(o_ref.dtype)


def _compress(chunks, pos, w1, b1, w2d, b2d):
    _, bg, nck, cin = chunks.shape
    return pl.pallas_call(
        functools.partial(_compress_kernel, nck=nck),
        grid=(2, bg),
        in_specs=[pl.BlockSpec((1, 1, nck, cin), lambda s, i: (s, i, 0, 0)),
                  pl.BlockSpec((1, 1, 2 * cin), lambda s, i: (s, 0, 0)),
                  pl.BlockSpec((1, 2 * cin, NSA_CMP_HIDDEN), lambda s, i: (s, 0, 0)),
                  pl.BlockSpec((1, 1, NSA_CMP_HIDDEN), lambda s, i: (s, 0, 0)),
                  pl.BlockSpec((1, NSA_CMP_HIDDEN, LANES), lambda s, i: (s, 0, 0)),
                  pl.BlockSpec((1, 1, LANES), lambda s, i: (s, 0, 0))],
        out_specs=pl.BlockSpec((1, 1, nck, LANES), lambda s, i: (s, i, 0, 0)),
        out_shape=jax.ShapeDtypeStruct((2, bg, nck, LANES), BF16),
        compiler_params=_cparams(("parallel", "parallel")),
        name="nsa_compress",
    )(chunks, pos, w1, b1, w2d, b2d)


def _cmp_kernel(q_ref, kc_ref, vct_ref, ztab_ref, mmat_ref, o_ref, sel_ref, *, hg, ncp, ns, n_sel, tiles):
    for sub in range(tiles):
        _cmp_tile(q_ref, kc_ref, vct_ref, ztab_ref, mmat_ref, o_ref, sel_ref, pl.program_id(2) * tiles + sub,
                  sub * QB, hg=hg, ncp=ncp, ns=ns, n_sel=n_sel)


def _cmp_tile(q_ref, kc_ref, vct_ref, ztab_ref, mmat_ref, o_ref, sel_ref, i, r0, *, hg, ncp, ns, n_sel):
    cols = hg * QB
    qstack = _stack_heads(q_ref, hg, r0)
    s = _dot_nt(kc_ref[0, 0], qstack)
    start = pl.multiple_of(ncp - 8 * i, 8)
    delta = jnp.concatenate([ztab_ref[h, pl.ds(start, ncp), :] for h in range(hg)], axis=1)
    s = s + delta
    m = jnp.max(s, axis=0, keepdims=True)
    e = jnp.exp2(s - m)
    p = e * jnp.where(m > M_INIT, 1.0 / jnp.sum(e, axis=0, keepdims=True), 0.0)
    _store_heads(o_ref, _dot(vct_ref[0, 0], p.astype(BF16)), hg, r0)

    imp = p[:, 0:QB]
    for h in range(1, hg):
        imp = imp + p[:, h * QB:(h + 1) * QB]
    hi, mid, lo = _split3(imp)
    mm = mmat_ref[...]
    imp_sel = _dot(mm, hi) + _dot(mm, mid) + _dot(mm, lo)
    sb = lax.broadcasted_iota(jnp.int32, (ns, QB), 0)
    tb = (i * QB + lax.broadcasted_iota(jnp.int32, (ns, QB), 1)) // NSA_SEL_LEN
    forced = jnp.logical_or(jnp.logical_or(sb == 0, sb == tb), sb == tb - 1)
    score = jnp.where(forced, FORCE_SCORE, jnp.where(sb <= tb, imp_sel, NEG_INF))
    sub = 8
    rows = [score[v * sub:(v + 1) * sub, :] for v in range(ns // sub)]
    ranks = [jnp.zeros((sub, QB), F32) for _ in rows]
    sub_idx = lax.broadcasted_iota(jnp.int32, (sub, QB), 0)
    for r in range(ns):
        other = score[r:r + 1, :]
        for v, mine in enumerate(rows):
            if v < r // sub:
                inc = jnp.where(other > mine, 1.0, 0.0)
            elif v > r // sub:
                inc = jnp.where(other >= mine, 1.0, 0.0)
            else:
                tie = jnp.where(sub_idx > r % sub, 1.0, 0.0)
                inc = jnp.where(other > mine, 1.0, jnp.where(other == mine, tie, 0.0))
            ranks[v] = ranks[v] + inc
    chosen = [jnp.where(rk < n_sel, 1.0, 0.0) for rk in ranks]
    chosen.append(jnp.zeros((LANES - ns, QB), F32))
    sel_ref[0, 0, r0:r0 + QB, :] = jnp.concatenate(chosen, axis=0).T.astype(sel_ref.dtype)


def _cmp_attention(proj, kc, vc_t, ztab, mmat_t, b, t):
    tiles = CMP_TILES
    nb = t // (QB * tiles)
    hg = NSA_GROUP
    qw = hg * HEAD_DIM
    ncp = t // NSA_CMP_STRIDE
    ns = t // NSA_SEL_LEN
    n_sel = min(NSA_SEL_TOPK, ns)
    return pl.pallas_call(
        functools.partial(_cmp_kernel, hg=hg, ncp=ncp, ns=ns, n_sel=n_sel, tiles=tiles),
        grid=(b, NSA_KV_HEADS, nb),
        in_specs=[
            pl.BlockSpec((QB * tiles, qw), lambda bi, g, i: (bi * nb + i, C_NQ // qw + g)),
            pl.BlockSpec((1, 1, ncp, LANES), lambda bi, g, i: (bi, g, 0, 0)),
            pl.BlockSpec((1, 1, HEAD_DIM, ncp), lambda bi, g, i: (bi, g, 0, 0)),
            pl.BlockSpec((hg, 2 * ncp, QB), lambda bi, g, i: (g, 0, 0)),
            pl.BlockSpec((ns, ncp), lambda bi, g, i: (0, 0)),
        ],
        out_specs=[pl.BlockSpec((QB * tiles, qw), lambda bi, g, i: (bi * nb + i, g)),
                   pl.BlockSpec((1, 1, QB * tiles, LANES), lambda bi, g, i: (bi, g, i, 0))],
        out_shape=[jax.ShapeDtypeStruct((b * t, NSA_W), BF16),
                   jax.ShapeDtypeStruct((b, NSA_KV_HEADS, t, LANES), BF16)],
        compiler_params=_cparams(("parallel", "parallel", "arbitrary")),
        name="nsa_cmp_select",
    )(proj, kc, vc_t, ztab, mmat_t)


def _combine_kernel(oc_ref, os_ref, ow_ref, g_ref, o_ref):
    sg = jax.nn.sigmoid(g_ref[...])
    lane = lax.broadcasted_iota(jnp.int32, (1, LANES), 1)
    upper = lane >= HEAD_DIM
    for p in range(NSA_HEADS // 2):
        acc = None
        for br, ref in enumerate((oc_ref, os_ref, ow_ref)):
            c0 = G_NG + NSA_N_BRANCH * (2 * p) + br
            c1 = G_NG + NSA_N_BRANCH * (2 * p + 1) + br
            gate = jnp.where(upper, sg[:, c1:c1 + 1], sg[:, c0:c0 + 1])
            term = gate * ref[:, p * LANES:(p + 1) * LANES].astype(F32)
            acc = term if acc is None else acc + term
        o_ref[:, p * LANES:(p + 1) * LANES] = acc.astype(o_ref.dtype)


def _nsa_combine(o_cmp, o_slc, o_win, gates, tm=512):
    n = o_cmp.shape[0]
    spec = pl.BlockSpec((tm, NSA_W), lambda i: (i, 0))
    return pl.pallas_call(
        _combine_kernel,
        grid=(n // tm,),
        in_specs=[spec, spec, spec, pl.BlockSpec((tm, D_GATE), lambda i: (i, 0))],
        out_specs=spec,
        out_shape=jax.ShapeDtypeStruct((n, NSA_W), BF16),
        compiler_params=_cparams(("parallel",)),
        name="nsa_combine",
    )(o_cmp, o_slc, o_win, gates)


def _ffn_kernel(te_ref, nu_ref, rows_ref, *refs, fused_norm, chained, sub):
    if fused_norm:
        x_ref, g_ref, wg_ref, wu_ref, wd_ref, o_ref, acc_ref, xn_ref = refs
    elif chained:
        x_ref, wg_ref, wu_ref, wd_ref, _, o_ref, acc_ref = refs
    else:
        x_ref, wg_ref, wu_ref, wd_ref, o_ref, acc_ref = refs
    del te_ref
    i = pl.program_id(0)
    k = pl.program_id(1)
    last = pl.num_programs(1) - 1
    used = i < nu_ref[0]
    tm = acc_ref.shape[0]
    src_ref = xn_ref if fused_norm else x_ref

    @pl.when(jnp.logical_and(used, k == 0))
    def _():
        if fused_norm:
            h = x_ref[...]
            y = h * lax.rsqrt(jnp.mean(h * h, axis=-1, keepdims=True) + RMS_EPS)
            xn_ref[...] = (y * g_ref[...]).astype(BF16)
            acc_ref[...] = h
        else:
            acc_ref[...] = jnp.zeros_like(acc_ref)

    def mlp(r0, nrows):
        x = src_ref[r0:r0 + nrows, :]
        gate = _dot(x, wg_ref[0].astype(BF16))
        up = _dot(x, wu_ref[0].astype(BF16))
        hid = (jax.nn.silu(gate) * up).astype(BF16)
        acc_ref[r0:r0 + nrows, :] += _dot(hid, wd_ref[0].astype(BF16))

    full = rows_ref[i] > tm - sub

    @pl.when(jnp.logical_and(used, full))
    def _():
        mlp(0, tm)

    for piece in range(tm // sub - 1):
        @pl.when(jnp.logical_and(jnp.logical_and(used, jnp.logical_not(full)), rows_ref[i] > piece * sub))
        def _():
            mlp(piece * sub, sub)

    @pl.when(jnp.logical_and(used, k == last))
    def _():
        o_ref[...] = acc_ref[...].astype(o_ref.dtype)

    @pl.when(jnp.logical_and(jnp.logical_not(used), k == last))
    def _():
        o_ref[...] = jnp.zeros_like(o_ref)


def _ffn(x, w_gate, w_up, w_down, tile_expert, n_used, tile_rows, out_dtype, gain=None, tm=512, tf=512,
         out_rows=None, out_tile0=0, out_buf=None, sub=256):
    r, d = x.shape
    nk = D_FF // tf
    n_tiles = r // tm
    fused_norm = gain is not None
    out_rows = r if out_rows is None else out_rows

    def tile(i, nu):
        return jnp.minimum(i, jnp.maximum(nu[0] - 1, 0))

    def kk(i, k, nu):
        return jnp.where(i < nu[0], k, nk - 1)

    in_specs = [pl.BlockSpec((tm, d), lambda i, k, te, nu, rw: (tile(i, nu), 0))]
    args = [x]
    scratch = [pltpu.VMEM((tm, d), F32)]
    if fused_norm:
        in_specs.append(pl.BlockSpec((1, d), lambda i, k, te, nu, rw: (0, 0)))
        args.append(gain.reshape(1, d).astype(F32))
        scratch.append(pltpu.VMEM((tm, d), BF16))
    in_specs += [
        pl.BlockSpec((1, d, tf), lambda i, k, te, nu, rw: (te[tile(i, nu)], 0, kk(i, k, nu))),
        pl.BlockSpec((1, d, tf), lambda i, k, te, nu, rw: (te[tile(i, nu)], 0, kk(i, k, nu))),
        pl.BlockSpec((1, tf, d), lambda i, k, te, nu, rw: (te[tile(i, nu)], kk(i, k, nu), 0)),
    ]
    args += [w_gate, w_up, w_down]
    aliases = {}
    if out_buf is not None:
        in_specs.append(pl.BlockSpec(memory_space=pl.ANY))
        args.append(out_buf)
        aliases = {3 + len(args) - 1: 0}
    return pl.pallas_call(
        functools.partial(_ffn_kernel, fused_norm=fused_norm, chained=out_buf is not None, sub=sub),
        grid_spec=pltpu.PrefetchScalarGridSpec(
            num_scalar_prefetch=3,
            grid=(n_tiles, nk),
            in_specs=in_specs,
            out_specs=pl.BlockSpec((tm, d), lambda i, k, te, nu, rw: (i + out_tile0, 0)),
            scratch_shapes=scratch,
        ),
        out_shape=jax.ShapeDtypeStruct((out_rows, d), out_dtype),
        input_output_aliases=aliases,
        compiler_params=_cparams(("arbitrary", "arbitrary"), FFN_VMEM_LIMIT),
        name="swiglu_ffn",
    )(tile_expert, n_used, tile_rows, *args)


def _router_kernel(l_ref, o_ref):
    lane = lax.broadcasted_iota(jnp.int32, l_ref.shape, 1)
    lf = lane.astype(F32)
    lg = jnp.where(lane < N_EXPERTS, l_ref[...], REMOVED)
    v1 = jnp.max(lg, axis=1, keepdims=True)
    i1 = jnp.min(jnp.where(lg == v1, lf, float(LANES)), axis=1, keepdims=True)
    lg2 = jnp.where(lf == i1, REMOVED, lg)
    v2 = jnp.max(lg2, axis=1, keepdims=True)
    i2 = jnp.min(jnp.where(lg2 == v2, lf, float(LANES)), axis=1, keepdims=True)
    e2 = jnp.exp(v2 - v1)
    den = 1.0 + e2
    p1 = 1.0 / den
    p2 = e2 / den
    out = jnp.where(lane == N_EXPERTS, i1, 0.0)
    out = jnp.where(lane == N_EXPERTS + 1, i2, out)
    out = jnp.where(lane == N_EXPERTS + 2, p1, out)
    out = jnp.where(lane == N_EXPERTS + 3, p2, out)
    o_ref[...] = out


def _router_top2(logits, tm=512):
    n = logits.shape[0]
    spec = pl.BlockSpec((tm, LANES), lambda i: (i, 0))
    return pl.pallas_call(
        _router_kernel,
        grid=(n // tm,),
        in_specs=[spec],
        out_specs=spec,
        out_shape=jax.ShapeDtypeStruct((n, LANES), F32),
        compiler_params=_cparams(("parallel",)),
        name="moe_router_top2",
    )(logits)


def _moe_combine_kernel(h_ref, a_ref, b_ref, top_ref, g_ref, o_ref, *, final_norm):
    p0 = top_ref[:, N_EXPERTS + TOP_K:N_EXPERTS + TOP_K + 1]
    p1 = top_ref[:, N_EXPERTS + TOP_K + 1:N_EXPERTS + TOP_K + 2]
    y = h_ref[...] + p0 * a_ref[...].astype(F32) + p1 * b_ref[...].astype(F32)
    if final_norm:
        y = y * lax.rsqrt(jnp.mean(y * y, axis=-1, keepdims=True) + RMS_EPS) * g_ref[...]
    o_ref[...] = y


def _moe_combine(h, ya, yb, top, gain, final_norm, tm=512):
    n, d = h.shape
    row = pl.BlockSpec((tm, d), lambda i: (i, 0))
    return pl.pallas_call(
        functools.partial(_moe_combine_kernel, final_norm=final_norm),
        grid=(n // tm,),
        in_specs=[row, row, row, pl.BlockSpec((tm, LANES), lambda i: (i, 0)),
                  pl.BlockSpec((1, d), lambda i: (0, 0))],
        out_specs=row,
        out_shape=jax.ShapeDtypeStruct((n, d), F32),
        compiler_params=_cparams(("parallel",)),
        name="moe_combine",
    )(h, ya, yb, top, gain.reshape(1, d).astype(F32))


def _t5_bucket_np(dist):
    n = np.maximum(dist, 0)
    max_exact = REL_BUCKETS // 2
    nf = np.maximum(n, 1).astype(np.float32)
    large = max_exact + (np.log(nf / np.float32(max_exact)) / np.float32(math.log(REL_MAX_DIST / max_exact))
                         * np.float32(REL_BUCKETS - max_exact)).astype(np.int32)
    large = np.minimum(large, REL_BUCKETS - 1)
    return np.where(n < max_exact, n, large).astype(np.int32)


def _band_tables(rel_tab, t):
    q = np.arange(QB)[:, None]
    k = np.arange(QB)[None, :]
    rel = rel_tab.astype(F32) * LOG2E
    toep = jnp.take(rel, jnp.asarray(_t5_bucket_np((q - k) % QB)), axis=0)
    toep = toep.transpose(2, 0, 1)
    before = jnp.asarray(k > q)
    far = rel[REL_BUCKETS - 1, SWA_HEADS:]

    def tiles(tab, fill_prev, extra=()):
        heads = tab.shape[0]
        parts = [jnp.where(before, tab, fill_prev), jnp.where(before, NEG_INF, tab)]
        parts += [jnp.broadcast_to(e, tab.shape) for e in extra]
        return jnp.stack([p.reshape(2, heads // 2 * QB, QB) for p in parts], axis=1)

    tabs_swa = tiles(toep[:SWA_HEADS], NEG_INF)
    tabs_nsa = tiles(toep[SWA_HEADS:] - far[:, None, None], 0.0, extra=[jnp.where(before, 0.0, NEG_INF)])
    ncp = t // NSA_CMP_STRIDE
    m = np.arange(-9, 7)[:, None]
    qr = np.arange(QB)[None, :]
    d = qr - NSA_CMP_STRIDE * m - (NSA_CMP_LEN - 1)
    vals = jnp.take(rel[:, SWA_HEADS:], jnp.asarray(_t5_bucket_np(np.clip(d, 0, None))), axis=0)
    vals = vals.transpose(2, 0, 1) - far[:, None, None]
    band = jnp.where(jnp.asarray(d >= 0), jnp.where(jnp.asarray(d < REL_MAX_DIST), vals, 0.0), NEG_INF)
    ztab = jnp.concatenate([jnp.zeros((NSA_HEADS, ncp - 9, QB), F32), band,
                            jnp.full((NSA_HEADS, ncp - 7, QB), NEG_INF, F32)], axis=1)
    return tabs_swa, tabs_nsa, ztab


def _selection_matrices(t):
    ncp = t // NSA_CMP_STRIDE
    ns = t // NSA_SEL_LEN
    per = NSA_SEL_LEN // NSA_CMP_STRIDE
    ratio = NSA_CMP_LEN // NSA_CMP_STRIDE
    mmat_t = np.zeros((ns, ncp), np.float32)
    for n in range(ncp - 1):
        for j in range(ratio):
            mmat_t[(n + j) // per, n] += 1.0
    emat_t = (np.arange(t)[:, None] // NSA_SEL_LEN == np.arange(LANES)[None, :]).astype(np.float32)
    return jnp.asarray(mmat_t, BF16), (jnp.asarray(emat_t.T, BF16), jnp.asarray(emat_t, BF16))


def _dup(w):
    d = w.shape[0]
    w = w.reshape(d, -1, 1, HEAD_DIM)
    return jnp.broadcast_to(w, (d, w.shape[1], 2, HEAD_DIM)).reshape(d, -1)


def _prep_in_weights(w_in_l):
    sizes = (FOX_W, FOX_W, FOX_W, FOX_HEADS, SWA_W, SWA_KV_W, SWA_KV_W,
             NSA_W, NSA_KV_W, NSA_KV_W, NSA_KV_W, NSA_KV_W, NSA_KV_W, NSA_KV_W, NSA_HEADS * NSA_N_BRANCH)
    splits = [int(s) for s in np.cumsum(sizes)[:-1]]
    (fq, fk, fv, ff, sq, sk, sv, nq, nkc, nvc, nks, nvs, nkw, nvw, ng) = jnp.split(w_in_l, splits, axis=-1)
    scale = HEAD_DIM ** -0.5 * LOG2E
    cols = [fq * scale, fk, fv, sq * scale, nq * scale, _dup(sk), _dup(nks), _dup(nkw), sv, nvs, nvw, nkc, nvc]
    cols.append(jnp.zeros((w_in_l.shape[0], D_PROJ - sum(c.shape[1] for c in cols)), w_in_l.dtype))
    w_proj = jnp.concatenate(cols, axis=-1).astype(BF16)
    pad = jnp.zeros((w_in_l.shape[0], D_GATE - FOX_HEADS - NSA_HEADS * NSA_N_BRANCH), w_in_l.dtype)
    w_gate = jnp.concatenate([ff, ng, pad], axis=-1).astype(BF16)
    return w_proj, w_gate


def _compress_inputs(proj, b, t):
    g = NSA_KV_HEADS
    nck = t // NSA_CMP_STRIDE
    x = proj[:, C_NKC:C_NKC + 2 * NSA_KV_W].reshape(b, nck, NSA_CMP_STRIDE, 2, g, HEAD_DIM)
    return x.transpose(3, 0, 4, 1, 2, 5).reshape(2, b * g, nck, NSA_CMP_STRIDE * HEAD_DIM)


def _mixer(h, gain, b, t, w_in_l, forget_bias, sinks, cmp_pos, cmp_w1, cmp_b1, cmp_w2, cmp_b2, tables):
    tabs_swa, tabs_nsa, ztab, mmat_t, emats = tables
    w_proj, w_gate = _prep_in_weights(w_in_l)
    proj, gates = _in_proj(h, gain, w_proj, w_gate)
    fv_t = proj[:, C_FV:C_FV + FOX_W].reshape(b, t, FOX_W).transpose(0, 2, 1)
    v_t = proj[:, C_V:C_V + 3 * NSA_KV_W].reshape(b, t, 3 * NSA_KV_W).transpose(0, 2, 1)

    qx, kx = _fox_decay_operands(gates, forget_bias, b, t)
    o_fox = _fox_attention(proj, fv_t, qx, kx, b, t)

    o_swa = _band_attention(proj, v_t, tabs_swa, b, t, mode="swa", hg=SWA_GROUP,
                            c_q=C_SQ, c_k=C_SK, v_blk=0, sinks=sinks)

    flat = _compress_inputs(proj, b, t)
    pos = cmp_pos.reshape(2, 1, NSA_CMP_LEN * HEAD_DIM).astype(F32)
    w2d = jnp.concatenate([cmp_w2, cmp_w2], axis=-1).astype(BF16)
    b2d = jnp.concatenate([cmp_b2, cmp_b2], axis=-1).reshape(2, 1, LANES).astype(F32)
    kvc = _compress(flat, pos, cmp_w1.astype(BF16), cmp_b1.reshape(2, 1, NSA_CMP_HIDDEN).astype(F32), w2d, b2d)
    ncp = t // NSA_CMP_STRIDE
    kvc = kvc.reshape(2, b, NSA_KV_HEADS, ncp, LANES)
    vc_t = kvc[1, :, :, :, :HEAD_DIM].transpose(0, 1, 3, 2)
    o_cmp, sel = _cmp_attention(proj, kvc[0], vc_t, ztab, mmat_t, b, t)
    o_slc = _band_attention(proj, v_t, tabs_nsa, b, t, mode="sel", hg=NSA_GROUP,
                            c_q=C_NQ, c_k=C_NKS, v_blk=2, sel=sel, emats=emats)
    o_win = _band_attention(proj, v_t, tabs_nsa, b, t, mode="win", hg=NSA_GROUP,
                            c_q=C_NQ, c_k=C_NKW, v_blk=4)
    o_nsa = _nsa_combine(o_cmp, o_slc, o_win, gates)
    return o_fox, o_swa, o_nsa


def _moe(hn, h, router, w_gate, w_up, w_down, gain, final_norm, tm=1024, tf=512):
    n, d = hn.shape
    w_r = jnp.zeros((d, LANES), BF16).at[:, :N_EXPERTS].set(router.astype(BF16))
    logits = _matmul([hn], [w_r], F32, tn=LANES, name="router_logits")
    top = _router_top2(logits)
    e_idx = top[:, N_EXPERTS:N_EXPERTS + TOP_K].astype(jnp.int32)
    e_flat = e_idx.reshape(-1)
    onehot = (e_flat[:, None] == jnp.arange(N_EXPERTS)[None, :]).astype(jnp.int32)
    csum = jnp.cumsum(onehot, axis=0)
    counts = csum[-1]
    rank = jnp.take_along_axis(csum, e_flat[:, None], axis=1)[:, 0] - 1
    padded = ((counts + tm - 1) // tm) * tm
    ends = jnp.cumsum(padded)
    starts = ends - padded
    dest = starts[e_flat] + rank
    r_pad = n * TOP_K + N_EXPERTS * tm
    src_tok = jnp.zeros((r_pad,), jnp.int32).at[dest].set(jnp.arange(n * TOP_K, dtype=jnp.int32) // TOP_K)
    tile_start = jnp.arange(r_pad // tm, dtype=jnp.int32) * tm
    tile_expert = jnp.minimum(jnp.sum(tile_start[:, None] >= ends[None, :], axis=1), N_EXPERTS - 1).astype(jnp.int32)
    n_used = (ends[-1:] // tm).astype(jnp.int32)
    tile_rows = jnp.clip((starts + counts)[tile_expert] - tile_start, 0, tm).astype(jnp.int32)
    n_tiles = r_pad // tm
    per = n_tiles // MOE_CHUNKS
    y = None
    for c in range(MOE_CHUNKS):
        xs = hn.at[src_tok[c * per * tm:(c + 1) * per * tm]].get(mode="promise_in_bounds")
        used_c = jnp.clip(n_used - c * per, 0, per)
        y = _ffn(xs, w_gate, w_up, w_down, tile_expert[c * per:(c + 1) * per], used_c,
                 tile_rows[c * per:(c + 1) * per], BF16, tm=tm, tf=tf,
                 out_rows=r_pad, out_tile0=c * per, out_buf=y)
    dest = dest.reshape(n, TOP_K)
    ya = y.at[dest[:, 0]].get(mode="promise_in_bounds")
    yb = y.at[dest[:, 1]].get(mode="promise_in_bounds")
    return _moe_combine(h, ya, yb, top, gain, final_norm)


def kernel(x, attn_norm, w_in, fox_forget_bias, swa_sinks, nsa_cmp_pos, nsa_cmp_w1, nsa_cmp_b1, nsa_cmp_w2,
           nsa_cmp_b2, w_out, rel_bias_table, ffn_norm, dense_w_gate, dense_w_up, dense_w_down, moe_router,
           moe_w_gate, moe_w_up, moe_w_down, final_norm):
    b, t, d = x.shape
    n = b * t
    depth = w_in.shape[0]
    tables = _band_tables(rel_bias_table, t) + _selection_matrices(t)
    h = x.reshape(n, d)
    for layer in range(depth):
        o_fox, o_swa, o_nsa = _mixer(h, attn_norm[layer], b, t, w_in[layer], fox_forget_bias[layer],
                                     swa_sinks[layer], nsa_cmp_pos[layer], nsa_cmp_w1[layer], nsa_cmp_b1[layer],
                                     nsa_cmp_w2[layer], nsa_cmp_b2[layer], tables)
        wo = w_out[layer].astype(BF16)
        h = _matmul([o_fox, o_swa, o_nsa], [wo[:FOX_W], wo[FOX_W:FOX_W + SWA_W], wo[FOX_W + SWA_W:]],
                    F32, residual=h, tn=1024, name="out_proj")
        i = layer // 2
        last = layer == depth - 1
        if layer % 2 == 0:
            tm = 512
            zeros = jnp.zeros((n // tm,), jnp.int32)
            h = _ffn(h, dense_w_gate[i][None].astype(BF16), dense_w_up[i][None].astype(BF16),
                     dense_w_down[i][None].astype(BF16), zeros, jnp.full((1,), n // tm, jnp.int32),
                     jnp.full((n // tm,), tm, jnp.int32), F32, gain=ffn_norm[layer], tm=tm)
            if last:
                h = _rmsnorm(h, final_norm, F32)
        else:
            hn = _rmsnorm(h, ffn_norm[layer], BF16)
            h = _moe(hn, h, moe_router[i], moe_w_gate[i], moe_w_up[i], moe_w_down[i], final_norm, last)
    return h.reshape(b, t, d)
```

```python
import functools
import math

import numpy as np
import jax
import jax.numpy as jnp
from jax import lax
from jax.experimental import pallas as pl
from jax.experimental.pallas import tpu as pltpu

F32 = jnp.float32
BF16 = jnp.bfloat16

D_MODEL = 2048
HEAD_DIM = 64
FOX_HEADS = 8
FOX_W = FOX_HEADS * HEAD_DIM
SWA_HEADS = 8
SWA_KV_HEADS = 2
SWA_GROUP = SWA_HEADS // SWA_KV_HEADS
SWA_W = SWA_HEADS * HEAD_DIM
SWA_KV_W = SWA_KV_HEADS * HEAD_DIM
SWA_WINDOW = 128
NSA_HEADS = 16
NSA_KV_HEADS = 2
NSA_GROUP = NSA_HEADS // NSA_KV_HEADS
NSA_W = NSA_HEADS * HEAD_DIM
NSA_KV_W = NSA_KV_HEADS * HEAD_DIM
NSA_CMP_LEN = 32
NSA_CMP_STRIDE = 16
NSA_CMP_HIDDEN = 256
NSA_SEL_LEN = 64
NSA_SEL_TOPK = 16
NSA_WINDOW = 512
NSA_N_BRANCH = 3
FORCE_SCORE = 1e9
REL_BUCKETS = 32
REL_MAX_DIST = 128
D_FF = 5632
N_EXPERTS = 8
TOP_K = 2
RMS_EPS = 1e-6
NEG_INF = -1e30
M_INIT = -1e29
REMOVED = -3e38
LOG2E = 1.4426950408889634
ONES_ROWS = 16
BAND_TILES = {"swa": 8, "win": 8, "sel": 4}
CMP_TILES = 8
MOE_CHUNKS = 4

LANES = 128
QB = 128
VMEM_LIMIT = 56 * 1024 * 1024
FFN_VMEM_LIMIT = 60 * 1024 * 1024

C_FQ, C_FK, C_FV = 0, 512, 1024
C_SQ = 1536
C_NQ = 2048
C_SK, C_NKS, C_NKW = 3072, 3328, 3584
C_V = 3840
C_NKC, C_NVC = 4224, 4352
D_PROJ = 4608
PROJ_TN = 768
G_NG = FOX_HEADS
D_GATE = 128


def _cparams(sem, vmem=VMEM_LIMIT):
    return pltpu.CompilerParams(dimension_semantics=sem, vmem_limit_bytes=vmem)


def _dot(a, b):
    return jnp.dot(a, b, preferred_element_type=F32)


def _dot_nt(a, b):
    return lax.dot_general(a, b, (((1,), (1,)), ((), ())), preferred_element_type=F32)


def _split3(x):
    hi = x.astype(BF16)
    r1 = x - hi.astype(F32)
    mid = r1.astype(BF16)
    lo = (r1 - mid.astype(F32)).astype(BF16)
    return hi, mid, lo


def _pipeline3(n, stage1, stage2, stage3):
    first, second, outs = {}, {}, []
    for step in range(n + 2):
        if step < n:
            first[step] = stage1(step)
        if 0 <= step - 1 < n:
            second[step - 1] = stage2(step - 1, first.pop(step - 1))
        if 0 <= step - 2 < n:
            outs.append(stage3(step - 2, *second.pop(step - 2)))
    return outs


def _rmsnorm_kernel(x_ref, g_ref, o_ref):
    x = x_ref[...]
    y = x * lax.rsqrt(jnp.mean(x * x, axis=-1, keepdims=True) + RMS_EPS)
    o_ref[...] = (y * g_ref[...]).astype(o_ref.dtype)


def _rmsnorm(h, g, out_dtype, tm=512):
    n, d = h.shape
    return pl.pallas_call(
        _rmsnorm_kernel,
        grid=(n // tm,),
        in_specs=[pl.BlockSpec((tm, d), lambda i: (i, 0)),
                  pl.BlockSpec((1, d), lambda i: (0, 0))],
        out_specs=pl.BlockSpec((tm, d), lambda i: (i, 0)),
        out_shape=jax.ShapeDtypeStruct((n, d), out_dtype),
        compiler_params=_cparams(("parallel",)),
        name="rmsnorm",
    )(h, g.reshape(1, d).astype(F32))


def _mm_kernel(*refs, n_in, has_res):
    o_ref = refs[-1]
    acc = None
    for a in range(n_in):
        d = _dot(refs[a][...], refs[n_in + a][...])
        acc = d if acc is None else acc + d
    if has_res:
        acc = acc + refs[2 * n_in][...]
    o_ref[...] = acc.astype(o_ref.dtype)


def _matmul(xs, ws, out_dtype, residual=None, tm=1024, tn=256, name="matmul"):
    n = xs[0].shape[0]
    m = ws[0].shape[1]
    tm = min(tm, n)
    tn = min(tn, m)
    in_specs = [pl.BlockSpec((tm, x.shape[1]), lambda i, j: (i, 0)) for x in xs]
    in_specs += [pl.BlockSpec((w.shape[0], tn), lambda i, j: (0, j)) for w in ws]
    args = list(xs) + list(ws)
    if residual is not None:
        in_specs.append(pl.BlockSpec((tm, tn), lambda i, j: (i, j)))
        args.append(residual)
    return pl.pallas_call(
        functools.partial(_mm_kernel, n_in=len(xs), has_res=residual is not None),
        grid=(n // tm, m // tn),
        in_specs=in_specs,
        out_specs=pl.BlockSpec((tm, tn), lambda i, j: (i, j)),
        out_shape=jax.ShapeDtypeStruct((n, m), out_dtype),
        compiler_params=_cparams(("parallel", "arbitrary")),
        name=name,
    )(*args)


def _in_proj_kernel(h_ref, g_ref, w_ref, wgate_ref, proj_ref, gates_ref, xn_ref):
    @pl.when(pl.program_id(1) == 0)
    def _():
        h = h_ref[...]
        y = h * lax.rsqrt(jnp.mean(h * h, axis=-1, keepdims=True) + RMS_EPS)
        xn_ref[...] = (y * g_ref[...]).astype(BF16)
        gates_ref[...] = _dot(xn_ref[...], wgate_ref[...])

    proj_ref[...] = _dot(xn_ref[...], w_ref[...]).astype(proj_ref.dtype)


def _in_proj(h, gain, w_proj, w_gate, tm=1024, tn=PROJ_TN):
    n, d = h.shape
    return pl.pallas_call(
        _in_proj_kernel,
        grid=(n // tm, D_PROJ // tn),
        in_specs=[pl.BlockSpec((tm, d), lambda i, j: (i, 0)),
                  pl.BlockSpec((1, d), lambda i, j: (0, 0)),
                  pl.BlockSpec((d, tn), lambda i, j: (0, j)),
                  pl.BlockSpec((d, D_GATE), lambda i, j: (0, 0))],
        out_specs=[pl.BlockSpec((tm, tn), lambda i, j: (i, j)),
                   pl.BlockSpec((tm, D_GATE), lambda i, j: (i, 0))],
        out_shape=[jax.ShapeDtypeStruct((n, D_PROJ), BF16), jax.ShapeDtypeStruct((n, D_GATE), F32)],
        scratch_shapes=[pltpu.VMEM((tm, d), BF16)],
        compiler_params=_cparams(("parallel", "arbitrary")),
        name="in_proj",
    )(h, gain.reshape(1, d).astype(F32), w_proj, w_gate)


def _cumsum_kernel(g_ref, b_ref, pq_ref, pk_ref, oq_ref, ok_ref, qx_ref, kx_ref, carry_ref, *, tc):
    @pl.when(pl.program_id(1) == 0)
    def _():
        carry_ref[...] = jnp.zeros_like(carry_ref)

    z = g_ref[...] + b_ref[...]
    log_f = jnp.minimum(z, 0.0) - jnp.log1p(jnp.exp(-jnp.abs(z)))
    row = lax.broadcasted_iota(jnp.int32, (tc, tc), 0)
    col = lax.broadcasted_iota(jnp.int32, (tc, tc), 1)
    tri = jnp.where(col <= row, 1.0, 0.0).astype(BF16)
    hi, mid, lo = _split3(log_f)
    c = _dot(tri, hi) + _dot(tri, mid) + _dot(tri, lo) + carry_ref[...]
    carry_ref[...] = c[tc - 1:tc, :]
    parts = jnp.concatenate(_split3(c * LOG2E), axis=1)
    qx_ref[...] = (_dot(parts, pq_ref[...]) + oq_ref[...]).astype(qx_ref.dtype)
    kx_ref[...] = (_dot(parts, pk_ref[...]) + ok_ref[...]).astype(kx_ref.dtype)


def _fox_decay_operands(gates, forget_bias, b, t, tc=512):
    nt = t // tc
    wide = FOX_HEADS * LANES
    bias = jnp.zeros((1, D_GATE), F32).at[0, :FOX_HEADS].set(forget_bias.astype(F32))
    pq = np.zeros((3 * D_GATE, wide), np.float32)
    pk = np.zeros((3 * D_GATE, wide), np.float32)
    oq = np.zeros((1, wide), np.float32)
    ok = np.zeros((1, wide), np.float32)
    for h in range(FOX_HEADS):
        for part in range(3):
            pq[part * D_GATE + h, h * LANES + part] = 1.0
            pk[part * D_GATE + h, h * LANES + 3 + part] = -1.0
            oq[0, h * LANES + 3 + part] = 1.0
            ok[0, h * LANES + part] = 1.0
    const = lambda shape: pl.BlockSpec(shape, lambda bi, ti: (0, 0))
    out_spec = pl.BlockSpec((tc, wide), lambda bi, ti: (bi * nt + ti, 0))
    return pl.pallas_call(
        functools.partial(_cumsum_kernel, tc=tc),
        grid=(b, nt),
        in_specs=[pl.BlockSpec((tc, D_GATE), lambda bi, ti: (bi * nt + ti, 0)),
                  const((1, D_GATE)), const((3 * D_GATE, wide)), const((3 * D_GATE, wide)),
                  const((1, wide)), const((1, wide))],
        out_specs=[out_spec, out_spec],
        out_shape=[jax.ShapeDtypeStruct((b * t, wide), BF16), jax.ShapeDtypeStruct((b * t, wide), BF16)],
        scratch_shapes=[pltpu.VMEM((1, D_GATE), F32)],
        compiler_params=_cparams(("parallel", "arbitrary")),
        name="fox_cumsum",
    )(gates, bias, jnp.asarray(pq, BF16), jnp.asarray(pk, BF16), jnp.asarray(oq), jnp.asarray(ok))


def _fox_kernel(qi_ref, kj_ref, q_ref, k_ref, vt_ref, qx_ref, kx_ref, o_ref, m_sc, acc_sc, *, tq):
    i = qi_ref[pl.program_id(1)]
    j = kj_ref[pl.program_id(1)]
    lane = lax.broadcasted_iota(jnp.int32, (1, LANES), 1)
    upper = lane >= HEAD_DIM

    @pl.when(j == 0)
    def _():
        m_sc[...] = jnp.full_like(m_sc, M_INIT)
        acc_sc[...] = jnp.zeros_like(acc_sc)

    def step(diagonal):
        if diagonal:
            key = lax.broadcasted_iota(jnp.int32, (tq, tq), 0)
            qry = lax.broadcasted_iota(jnp.int32, (tq, tq), 1)
            causal = key <= qry
        ones = jnp.ones((ONES_ROWS, tq), BF16)

        def qk(h):
            p, a = divmod(h, 2)
            q2 = q_ref[:, p * LANES:(p + 1) * LANES]
            k2 = k_ref[:, p * LANES:(p + 1) * LANES]
            zero = jnp.zeros_like(q2)
            qa = jnp.where(upper, q2, zero) if a else jnp.where(upper, zero, q2)
            q_aug = jnp.concatenate([qa, qx_ref[:, h * LANES:(h + 1) * LANES]], axis=1)
            k_aug = jnp.concatenate([k2, kx_ref[:, h * LANES:(h + 1) * LANES]], axis=1)
            s = _dot_nt(k_aug, q_aug)
            if diagonal:
                s = jnp.where(causal, s, NEG_INF)
            return s, jnp.max(s, axis=0, keepdims=True)

        def softmax(h, scored):
            s, s_max = scored
            m_prev = m_sc[h]
            m_new = jnp.maximum(m_prev, s_max)
            m_sc[h] = m_new
            return jnp.exp2(m_prev - m_new), jnp.exp2(s - m_new).astype(BF16)

        def pv(h, alpha, pr):
            vt = jnp.concatenate([vt_ref[0, h * HEAD_DIM:(h + 1) * HEAD_DIM, :], ones], axis=0)
            acc_sc[h] = alpha * acc_sc[h] + _dot(vt, pr)

        _pipeline3(FOX_HEADS, qk, softmax, pv)

    @pl.when(j < i)
    def _():
        step(False)

    @pl.when(j == i)
    def _():
        step(True)
        for p in range(FOX_HEADS // 2):
            outs = []
            for h in (2 * p, 2 * p + 1):
                acc = acc_sc[h]
                outs.append(acc[:HEAD_DIM] * (1.0 / acc[HEAD_DIM:HEAD_DIM + 1]))
            o_ref[:, p * LANES:(p + 1) * LANES] = jnp.concatenate(outs, axis=0).T.astype(o_ref.dtype)


def _fox_attention(proj, v_t, qx, kx, b, t, tq=512):
    nt = t // tq
    wide = FOX_HEADS * LANES
    pairs = [(i, j) for i in range(nt) for j in range(i + 1)]
    qi = jnp.asarray([p[0] for p in pairs], jnp.int32)
    kj = jnp.asarray([p[1] for p in pairs], jnp.int32)
    return pl.pallas_call(
        functools.partial(_fox_kernel, tq=tq),
        grid_spec=pltpu.PrefetchScalarGridSpec(
            num_scalar_prefetch=2,
            grid=(b, len(pairs)),
            in_specs=[
                pl.BlockSpec((tq, FOX_W), lambda bi, s, qi, kj: (bi * nt + qi[s], C_FQ // FOX_W)),
                pl.BlockSpec((tq, FOX_W), lambda bi, s, qi, kj: (bi * nt + kj[s], C_FK // FOX_W)),
                pl.BlockSpec((1, FOX_W, tq), lambda bi, s, qi, kj: (bi, 0, kj[s])),
                pl.BlockSpec((tq, wide), lambda bi, s, qi, kj: (bi * nt + qi[s], 0)),
                pl.BlockSpec((tq, wide), lambda bi, s, qi, kj: (bi * nt + kj[s], 0)),
            ],
            out_specs=pl.BlockSpec((tq, FOX_W), lambda bi, s, qi, kj: (bi * nt + qi[s], 0)),
            scratch_shapes=[pltpu.VMEM((FOX_HEADS, 1, tq), F32),
                            pltpu.VMEM((FOX_HEADS, HEAD_DIM + ONES_ROWS, tq), F32)],
        ),
        out_shape=jax.ShapeDtypeStruct((b * t, FOX_W), BF16),
        compiler_params=_cparams(("parallel", "arbitrary")),
        name="fox_attention",
    )(qi, kj, proj, proj, v_t, qx, kx)


def _stack_heads(q_ref, hg, r0=0):
    lane = lax.broadcasted_iota(jnp.int32, (1, LANES), 1)
    upper = lane >= HEAD_DIM
    qs = []
    for p in range(hg // 2):
        q2 = q_ref[r0:r0 + QB, p * LANES:(p + 1) * LANES]
        qs.append(jnp.where(upper, jnp.zeros_like(q2), q2))
        qs.append(jnp.where(upper, q2, jnp.zeros_like(q2)))
    return jnp.concatenate(qs, axis=0)


def _store_heads(o_ref, o_t, hg, r0=0):
    for p in range(hg // 2):
        pair = jnp.concatenate([o_t[:, (2 * p) * QB:(2 * p + 1) * QB],
                                o_t[:, (2 * p + 1) * QB:(2 * p + 2) * QB]], axis=0)
        o_ref[r0:r0 + QB, p * LANES:(p + 1) * LANES] = pair.T.astype(o_ref.dtype)


def _band_kernel(*refs, mode, hg, chunk, tiles):
    for sub in range(tiles):
        _band_tile(*refs, mode=mode, hg=hg, chunk=chunk, i=pl.program_id(2) * tiles + sub, r0=sub * QB)


def _band_tile(*refs, mode, hg, chunk, i, r0):
    if mode == "swa":
        q_ref, k_ref, vt_ref, tab_ref, sink_ref, o_ref = refs
    elif mode == "sel":
        (q_ref, k_ref, vt_ref, tab_ref, sel_ref, e_ref, et_ref, o_ref,
         sa_ref, sb_ref, ma_ref, mb_ref, m_ref, acc_ref) = refs
    else:
        q_ref, k_ref, vt_ref, tab_ref, o_ref = refs
    g = pl.program_id(1)
    cols = hg * QB
    qstack = _stack_heads(q_ref, hg, r0)
    eye = jnp.where(lax.broadcasted_iota(jnp.int32, (QB, QB), 0) == lax.broadcasted_iota(jnp.int32, (QB, QB), 1),
                    1.0, 0.0).astype(BF16)

    def k_block(blk):
        return k_ref[pl.ds(pl.multiple_of(blk * QB, QB), QB), :]

    def vt_rows(start, size):
        return jnp.concatenate([vt_ref[0, :, pl.ds(start, size)], jnp.ones((ONES_ROWS, size), BF16)], axis=0)

    def vt_block(blk):
        return vt_rows(pl.multiple_of(blk * QB, QB), QB)

    def with_table(table):
        return jnp.concatenate([qstack, table.astype(BF16)], axis=1)

    def grp(x, j):
        return x[j * GW:(j + 1) * GW]

    def lanes(x, j):
        return x[:, j * GW:(j + 1) * GW]

    GW = cols if mode == "sel" else 2 * QB
    n_grp = cols // GW
    ip = jnp.maximum(i - 1, 0)
    t_prev = tab_ref[0, 0]
    if mode == "sel":
        sel_q = sel_ref[0, 0, r0:r0 + QB, :]
        allowed = _dot(sel_q, e_ref[:, pl.ds(pl.multiple_of(ip * QB, QB), QB)])
        t_prev = jnp.where(jnp.concatenate([allowed] * hg, axis=0) > 0.5, t_prev, NEG_INF)
    t_prev = jnp.where(i > 0, t_prev, NEG_INF)
    q_prev = with_table(t_prev)
    q_cur = with_table(tab_ref[0, 1])
    k_prev = jnp.concatenate([k_block(ip), eye], axis=1)
    k_cur = jnp.concatenate([k_block(i), eye], axis=1)
    vt_near = jnp.concatenate([vt_block(ip), vt_block(i)], axis=1)

    if mode == "swa":
        head = lax.broadcasted_iota(jnp.int32, (1, cols), 1) // QB
        sink = jnp.zeros((1, cols), F32)
        for h in range(hg):
            sink = jnp.where(head == h, sink_ref[g * hg + h], sink)

    elif mode == "win":
        n_far = NSA_WINDOW // QB - 1
        backs = range(n_far + 1, 1, -1)
        onehot = jnp.concatenate([eye, jnp.zeros(((n_far - 1) * QB, QB), BF16)], axis=0)
        k_far = jnp.concatenate(
            [jnp.concatenate([k_block(jnp.maximum(i - bk, 0)) for bk in backs], axis=0), onehot], axis=1)
        vt_far = jnp.concatenate([vt_block(jnp.maximum(i - bk, 0)) for bk in backs], axis=1)
        q_tri = with_table(tab_ref[0, 2])
        in_seq = lax.broadcasted_iota(jnp.int32, (n_far * QB, 1), 0) >= (n_far + 1 - i) * QB

    else:
        blk = lax.broadcasted_iota(jnp.int32, (QB, LANES), 1)
        pick = jnp.logical_and(sel_q.astype(F32) > 0.5, blk < 2 * (i - 1))
        sel_bias = jnp.where(pick, 0.0, NEG_INF)
        q_far = with_table(jnp.concatenate([sel_bias] * hg, axis=0))
        per = chunk // QB
        n_chunks = (jnp.maximum(i - 1, 0) + per - 1) // per
        def far_scores(c):
            off = pl.multiple_of(c * chunk, chunk)
            k_rows = jnp.concatenate([k_ref[pl.ds(off, chunk), :], et_ref[pl.ds(off, chunk), :]], axis=1)
            return _dot_nt(k_rows, q_far)

        def put_scores(c, s_ref, smax_ref):
            s = far_scores(c)
            s_ref[...] = s
            smax_ref[...] = jnp.max(s, axis=0, keepdims=True)

        def sweep(c, cur, nxt):
            put_scores(jnp.minimum(c + 1, n_chunks - 1), *nxt)
            s_ref, smax_ref = cur
            m_old = m_ref[...]
            m_new = jnp.maximum(m_old, smax_ref[...])
            pf = jnp.exp2(s_ref[...] - m_new).astype(BF16)
            acc_ref[...] = (jnp.exp2(m_old - m_new) * acc_ref[...]
                            + _dot(vt_rows(pl.multiple_of(c * chunk, chunk), chunk), pf))
            m_ref[...] = m_new

        buf_a, buf_b = (sa_ref, ma_ref), (sb_ref, mb_ref)

        def body(quad, carry):
            for u in range(4):
                sweep(4 * quad + u, *((buf_a, buf_b) if u % 2 == 0 else (buf_b, buf_a)))
            return carry

        m_ref[...] = jnp.full((1, cols), M_INIT, F32)
        acc_ref[...] = jnp.zeros((HEAD_DIM + ONES_ROWS, cols), F32)
        put_scores(0, *buf_a)
        lax.fori_loop(0, n_chunks // 4, body, 0)
        done = (n_chunks // 4) * 4

        @pl.when(n_chunks % 4 >= 2)
        def _():
            sweep(done, buf_a, buf_b)
            sweep(done + 1, buf_b, buf_a)

        @pl.when(n_chunks % 2 == 1)
        def _():
            sweep(n_chunks - 1, buf_a, buf_b)

        m_far = m_ref[...]
        acc_far = acc_ref[...]

    def near_scores(j):
        parts = [_dot_nt(k_prev, grp(q_prev, j)), _dot_nt(k_cur, grp(q_cur, j))]
        if mode == "win":
            parts.append(jnp.where(in_seq, _dot_nt(k_far, grp(q_tri, j)), NEG_INF))
        return parts

    def near_softmax(j, parts):
        m = jnp.max(parts[0], axis=0, keepdims=True)
        for s in parts[1:]:
            m = jnp.maximum(m, jnp.max(s, axis=0, keepdims=True))
        if mode == "swa":
            m = jnp.maximum(m, lanes(sink, j))
        if mode == "sel":
            m = jnp.maximum(m, lanes(m_far, j))
        return m, [jnp.exp2(s - m).astype(BF16) for s in parts]

    def near_output(j, m, probs):
        acc = _dot(vt_near, jnp.concatenate(probs[:2], axis=0))
        if mode == "win":
            acc = acc + _dot(vt_far, probs[2])
        if mode == "sel":
            acc = jnp.exp2(lanes(m_far, j) - m) * lanes(acc_far, j) + acc
        den = acc[HEAD_DIM:HEAD_DIM + 1]
        if mode == "swa":
            den = den + jnp.exp2(lanes(sink, j) - m)
        o_t = acc[:HEAD_DIM] * (1.0 / den)
        for pp in range(GW // (2 * QB)):
            pair = jnp.concatenate([o_t[:, 2 * pp * QB:(2 * pp + 1) * QB],
                                    o_t[:, (2 * pp + 1) * QB:(2 * pp + 2) * QB]], axis=0)
            c0 = (j * (GW // (2 * QB)) + pp) * LANES
            o_ref[r0:r0 + QB, c0:c0 + LANES] = pair.T.astype(o_ref.dtype)

    _pipeline3(n_grp, near_scores, near_softmax, near_output)


def _band_attention(proj, v_t, tabs, b, t, *, mode, hg, c_q, c_k, v_blk, sinks=None, sel=None, emats=None):
    tiles = BAND_TILES[mode]
    nb = t // (QB * tiles)
    n_groups = 2
    qw = hg * HEAD_DIM
    chunk = min(512, t)
    in_specs = [
        pl.BlockSpec((QB * tiles, qw), lambda bi, g, i: (bi * nb + i, c_q // qw + g)),
        pl.BlockSpec((t, LANES), lambda bi, g, i: (bi, c_k // LANES + g)),
        pl.BlockSpec((1, HEAD_DIM, t), lambda bi, g, i: (bi, v_blk + g, 0)),
        pl.BlockSpec((1,) + tabs.shape[1:], lambda bi, g, i: (g, 0, 0, 0)),
    ]
    args = [proj, proj, v_t, tabs]
    scratch = []
    if mode == "swa":
        in_specs.append(pl.BlockSpec(memory_space=pltpu.SMEM))
        args.append(sinks.astype(F32) * LOG2E)
    if mode == "sel":
        emat, emat_t = emats
        in_specs.append(pl.BlockSpec((1, 1, QB * tiles, LANES), lambda bi, g, i: (bi, g, i, 0)))
        in_specs.append(pl.BlockSpec((LANES, t), lambda bi, g, i: (0, 0)))
        in_specs.append(pl.BlockSpec((t, LANES), lambda bi, g, i: (0, 0)))
        args += [sel, emat, emat_t]
        scratch = [pltpu.VMEM((chunk, hg * QB), F32), pltpu.VMEM((chunk, hg * QB), F32),
                   pltpu.VMEM((1, hg * QB), F32), pltpu.VMEM((1, hg * QB), F32),
                   pltpu.VMEM((1, hg * QB), F32), pltpu.VMEM((HEAD_DIM + ONES_ROWS, hg * QB), F32)]
    return pl.pallas_call(
        functools.partial(_band_kernel, mode=mode, hg=hg, chunk=chunk, tiles=tiles),
        grid=(b, n_groups, nb),
        in_specs=in_specs,
        out_specs=pl.BlockSpec((QB * tiles, qw), lambda bi, g, i: (bi * nb + i, g)),
        out_shape=jax.ShapeDtypeStruct((b * t, n_groups * qw), BF16),
        scratch_shapes=scratch,
        compiler_params=_cparams(("parallel", "parallel", "arbitrary")),
        name="band_" + mode,
    )(*args)


def _compress_kernel(x_ref, pos_ref, w1_ref, b1_ref, w2_ref, b2_ref, o_ref, *, nck):
    half = NSA_CMP_STRIDE * HEAD_DIM
    c = x_ref[0, 0].astype(F32)
    first = _dot((c + pos_ref[0, :, :half]).astype(BF16), w1_ref[0, :half, :])
    second = _dot((c + pos_ref[0, :, half:]).astype(BF16), w1_ref[0, half:, :])
    hid = jax.nn.gelu(first + pltpu.roll(second, nck - 1, 0) + b1_ref[0])
    o_ref[0, 0] = (_dot(hid.astype(BF16), w2_ref[0]) + b2_ref[0]).astype(o_ref.dtype)


def _compress(chunks, pos, w1, b1, w2d, b2d):
    _, bg, nck, cin = chunks.shape
    return pl.pallas_call(
        functools.partial(_compress_kernel, nck=nck),
        grid=(2, bg),
        in_specs=[pl.BlockSpec((1, 1, nck, cin), lambda s, i: (s, i, 0, 0)),
                  pl.BlockSpec((1, 1, 2 * cin), lambda s, i: (s, 0, 0)),
                  pl.BlockSpec((1, 2 * cin, NSA_CMP_HIDDEN), lambda s, i: (s, 0, 0)),
                  pl.BlockSpec((1, 1, NSA_CMP_HIDDEN), lambda s, i: (s, 0, 0)),
                  pl.BlockSpec((1, NSA_CMP_HIDDEN, LANES), lambda s, i: (s, 0, 0)),
                  pl.BlockSpec((1, 1, LANES), lambda s, i: (s, 0, 0))],
        out_specs=pl.BlockSpec((1, 1, nck, LANES), lambda s, i: (s, i, 0, 0)),
        out_shape=jax.ShapeDtypeStruct((2, bg, nck, LANES), BF16),
        compiler_params=_cparams(("parallel", "parallel")),
        name="nsa_compress",
    )(chunks, pos, w1, b1, w2d, b2d)


def _cmp_kernel(q_ref, kc_ref, vct_ref, ztab_ref, mmat_ref, o_ref, sel_ref, *, hg, ncp, ns, n_sel, tiles):
    for sub in range(tiles):
        _cmp_tile(q_ref, kc_ref, vct_ref, ztab_ref, mmat_ref, o_ref, sel_ref, pl.program_id(2) * tiles + sub,
                  sub * QB, hg=hg, ncp=ncp, ns=ns, n_sel=n_sel)


def _cmp_tile(q_ref, kc_ref, vct_ref, ztab_ref, mmat_ref, o_ref, sel_ref, i, r0, *, hg, ncp, ns, n_sel):
    cols = hg * QB
    qstack = _stack_heads(q_ref, hg, r0)
    s = _dot_nt(kc_ref[0, 0], qstack)
    start = pl.multiple_of(ncp - 8 * i, 8)
    delta = jnp.concatenate([ztab_ref[h, pl.ds(start, ncp), :] for h in range(hg)], axis=1)
    s = s + delta
    m = jnp.max(s, axis=0, keepdims=True)
    e = jnp.exp2(s - m)
    p = e * jnp.where(m > M_INIT, 1.0 / jnp.sum(e, axis=0, keepdims=True), 0.0)
    _store_heads(o_ref, _dot(vct_ref[0, 0], p.astype(BF16)), hg, r0)

    imp = p[:, 0:QB]
    for h in range(1, hg):
        imp = imp + p[:, h * QB:(h + 1) * QB]
    hi, mid, lo = _split3(imp)
    mm = mmat_ref[...]
    imp_sel = _dot(mm, hi) + _dot(mm, mid) + _dot(mm, lo)
    sb = lax.broadcasted_iota(jnp.int32, (ns, QB), 0)
    tb = (i * QB + lax.broadcasted_iota(jnp.int32, (ns, QB), 1)) // NSA_SEL_LEN
    forced = jnp.logical_or(jnp.logical_or(sb == 0, sb == tb), sb == tb - 1)
    score = jnp.where(forced, FORCE_SCORE, jnp.where(sb <= tb, imp_sel, NEG_INF))
    sub = 8
    rows = [score[v * sub:(v + 1) * sub, :] for v in range(ns // sub)]
    ranks = [jnp.zeros((sub, QB), F32) for _ in rows]
    sub_idx = lax.broadcasted_iota(jnp.int32, (sub, QB), 0)
    for r in range(ns):
        other = score[r:r + 1, :]
        for v, mine in enumerate(rows):
            if v < r // sub:
                inc = jnp.where(other > mine, 1.0, 0.0)
            elif v > r // sub:
                inc = jnp.where(other >= mine, 1.0, 0.0)
            else:
                tie = jnp.where(sub_idx > r % sub, 1.0, 0.0)
                inc = jnp.where(other > mine, 1.0, jnp.where(other == mine, tie, 0.0))
            ranks[v] = ranks[v] + inc
    chosen = [jnp.where(rk < n_sel, 1.0, 0.0) for rk in ranks]
    chosen.append(jnp.zeros((LANES - ns, QB), F32))
    sel_ref[0, 0, r0:r0 + QB, :] = jnp.concatenate(chosen, axis=0).T.astype(sel_ref.dtype)


def _cmp_attention(proj, kc, vc_t, ztab, mmat_t, b, t):
    tiles = CMP_TILES
    nb = t // (QB * tiles)
    hg = NSA_GROUP
    qw = hg * HEAD_DIM
    ncp = t // NSA_CMP_STRIDE
    ns = t // NSA_SEL_LEN
    n_sel = min(NSA_SEL_TOPK, ns)
    return pl.pallas_call(
        functools.partial(_cmp_kernel, hg=hg, ncp=ncp, ns=ns, n_sel=n_sel, tiles=tiles),
        grid=(b, NSA_KV_HEADS, nb),
        in_specs=[
            pl.BlockSpec((QB * tiles, qw), lambda bi, g, i: (bi * nb + i, C_NQ // qw + g)),
            pl.BlockSpec((1, 1, ncp, LANES), lambda bi, g, i: (bi, g, 0, 0)),
            pl.BlockSpec((1, 1, HEAD_DIM, ncp), lambda bi, g, i: (bi, g, 0, 0)),
            pl.BlockSpec((hg, 2 * ncp, QB), lambda bi, g, i: (g, 0, 0)),
            pl.BlockSpec((ns, ncp), lambda bi, g, i: (0, 0)),
        ],
        out_specs=[pl.BlockSpec((QB * tiles, qw), lambda bi, g, i: (bi * nb + i, g)),
                   pl.BlockSpec((1, 1, QB * tiles, LANES), lambda bi, g, i: (bi, g, i, 0))],
        out_shape=[jax.ShapeDtypeStruct((b * t, NSA_W), BF16),
                   jax.ShapeDtypeStruct((b, NSA_KV_HEADS, t, LANES), BF16)],
        compiler_params=_cparams(("parallel", "parallel", "arbitrary")),
        name="nsa_cmp_select",
    )(proj, kc, vc_t, ztab, mmat_t)


def _combine_kernel(oc_ref, os_ref, ow_ref, g_ref, e_ref, o_ref):
    sg = jax.nn.sigmoid(g_ref[...]).astype(BF16)
    acc = None
    for br, ref in enumerate((oc_ref, os_ref, ow_ref)):
        term = _dot(sg, e_ref[br]) * ref[...].astype(F32)
        acc = term if acc is None else acc + term
    o_ref[...] = acc.astype(o_ref.dtype)


def _nsa_combine(o_cmp, o_slc, o_win, gates, tm=512):
    n = o_cmp.shape[0]
    spread = np.zeros((NSA_N_BRANCH, D_GATE, NSA_W), np.float32)
    for h in range(NSA_HEADS):
        for br in range(NSA_N_BRANCH):
            spread[br, G_NG + NSA_N_BRANCH * h + br, h * HEAD_DIM:(h + 1) * HEAD_DIM] = 1.0
    spec = pl.BlockSpec((tm, NSA_W), lambda i: (i, 0))
    return pl.pallas_call(
        _combine_kernel,
        grid=(n // tm,),
        in_specs=[spec, spec, spec, pl.BlockSpec((tm, D_GATE), lambda i: (i, 0)),
                  pl.BlockSpec((NSA_N_BRANCH, D_GATE, NSA_W), lambda i: (0, 0, 0))],
        out_specs=spec,
        out_shape=jax.ShapeDtypeStruct((n, NSA_W), BF16),
        compiler_params=_cparams(("parallel",)),
        name="nsa_combine",
    )(o_cmp, o_slc, o_win, gates, jnp.asarray(spread, BF16))


def _ffn_kernel(te_ref, nu_ref, rows_ref, *refs, fused_norm, chained, sub):
    if fused_norm:
        x_ref, g_ref, wg_ref, wu_ref, wd_ref, o_ref, acc_ref, xn_ref = refs
    elif chained:
        x_ref, wg_ref, wu_ref, wd_ref, _, o_ref, acc_ref = refs
    else:
        x_ref, wg_ref, wu_ref, wd_ref, o_ref, acc_ref = refs
    del te_ref
    i = pl.program_id(0)
    k = pl.program_id(1)
    last = pl.num_programs(1) - 1
    used = i < nu_ref[0]
    tm = acc_ref.shape[0]
    src_ref = xn_ref if fused_norm else x_ref

    @pl.when(jnp.logical_and(used, k == 0))
    def _():
        if fused_norm:
            h = x_ref[...]
            y = h * lax.rsqrt(jnp.mean(h * h, axis=-1, keepdims=True) + RMS_EPS)
            xn_ref[...] = (y * g_ref[...]).astype(BF16)
            acc_ref[...] = h
        else:
            acc_ref[...] = jnp.zeros_like(acc_ref)

    def mlp(r0, nrows):
        x = src_ref[r0:r0 + nrows, :]
        gate = _dot(x, wg_ref[0].astype(BF16))
        up = _dot(x, wu_ref[0].astype(BF16))
        hid = (jax.nn.silu(gate) * up).astype(BF16)
        acc_ref[r0:r0 + nrows, :] += _dot(hid, wd_ref[0].astype(BF16))

    full = rows_ref[i] > tm - sub

    @pl.when(jnp.logical_and(used, full))
    def _():
        mlp(0, tm)

    for piece in range(tm // sub - 1):
        @pl.when(jnp.logical_and(jnp.logical_and(used, jnp.logical_not(full)), rows_ref[i] > piece * sub))
        def _():
            mlp(piece * sub, sub)

    @pl.when(jnp.logical_and(used, k == last))
    def _():
        o_ref[...] = acc_ref[...].astype(o_ref.dtype)

    @pl.when(jnp.logical_and(jnp.logical_not(used), k == last))
    def _():
        o_ref[...] = jnp.zeros_like(o_ref)


def _ffn(x, w_gate, w_up, w_down, tile_expert, n_used, tile_rows, out_dtype, gain=None, tm=512, tf=512,
         out_rows=None, out_tile0=0, out_buf=None, sub=256):
    r, d = x.shape
    nk = D_FF // tf
    n_tiles = r // tm
    fused_norm = gain is not None
    out_rows = r if out_rows is None else out_rows

    def tile(i, nu):
        return jnp.minimum(i, jnp.maximum(nu[0] - 1, 0))

    def kk(i, k, nu):
        return jnp.where(i < nu[0], k, nk - 1)

    in_specs = [pl.BlockSpec((tm, d), lambda i, k, te, nu, rw: (tile(i, nu), 0))]
    args = [x]
    scratch = [pltpu.VMEM((tm, d), F32)]
    if fused_norm:
        in_specs.append(pl.BlockSpec((1, d), lambda i, k, te, nu, rw: (0, 0)))
        args.append(gain.reshape(1, d).astype(F32))
        scratch.append(pltpu.VMEM((tm, d), BF16))
    in_specs += [
        pl.BlockSpec((1, d, tf), lambda i, k, te, nu, rw: (te[tile(i, nu)], 0, kk(i, k, nu))),
        pl.BlockSpec((1, d, tf), lambda i, k, te, nu, rw: (te[tile(i, nu)], 0, kk(i, k, nu))),
        pl.BlockSpec((1, tf, d), lambda i, k, te, nu, rw: (te[tile(i, nu)], kk(i, k, nu), 0)),
    ]
    args += [w_gate, w_up, w_down]
    aliases = {}
    if out_buf is not None:
        in_specs.append(pl.BlockSpec(memory_space=pl.ANY))
        args.append(out_buf)
        aliases = {3 + len(args) - 1: 0}
    return pl.pallas_call(
        functools.partial(_ffn_kernel, fused_norm=fused_norm, chained=out_buf is not None, sub=sub),
        grid_spec=pltpu.PrefetchScalarGridSpec(
            num_scalar_prefetch=3,
            grid=(n_tiles, nk),
            in_specs=in_specs,
            out_specs=pl.BlockSpec((tm, d), lambda i, k, te, nu, rw: (i + out_tile0, 0)),
            scratch_shapes=scratch,
        ),
        out_shape=jax.ShapeDtypeStruct((out_rows, d), out_dtype),
        input_output_aliases=aliases,
        compiler_params=_cparams(("arbitrary", "arbitrary"), FFN_VMEM_LIMIT),
        name="swiglu_ffn",
    )(tile_expert, n_used, tile_rows, *args)


def _router_kernel(l_ref, o_ref):
    lane = lax.broadcasted_iota(jnp.int32, l_ref.shape, 1)
    lf = lane.astype(F32)
    lg = jnp.where(lane < N_EXPERTS, l_ref[...], REMOVED)
    v1 = jnp.max(lg, axis=1, keepdims=True)
    i1 = jnp.min(jnp.where(lg == v1, lf, float(LANES)), axis=1, keepdims=True)
    lg2 = jnp.where(lf == i1, REMOVED, lg)
    v2 = jnp.max(lg2, axis=1, keepdims=True)
    i2 = jnp.min(jnp.where(lg2 == v2, lf, float(LANES)), axis=1, keepdims=True)
    e2 = jnp.exp(v2 - v1)
    den = 1.0 + e2
    p1 = 1.0 / den
    p2 = e2 / den
    out = jnp.where(lane == N_EXPERTS, i1, 0.0)
    out = jnp.where(lane == N_EXPERTS + 1, i2, out)
    out = jnp.where(lane == N_EXPERTS + 2, p1, out)
    out = jnp.where(lane == N_EXPERTS + 3, p2, out)
    o_ref[...] = out


def _router_top2(logits, tm=512):
    n = logits.shape[0]
    spec = pl.BlockSpec((tm, LANES), lambda i: (i, 0))
    return pl.pallas_call(
        _router_kernel,
        grid=(n // tm,),
        in_specs=[spec],
        out_specs=spec,
        out_shape=jax.ShapeDtypeStruct((n, LANES), F32),
        compiler_params=_cparams(("parallel",)),
        name="moe_router_top2",
    )(logits)


def _moe_combine_kernel(h_ref, a_ref, b_ref, top_ref, g_ref, o_ref, *, final_norm):
    p0 = top_ref[:, N_EXPERTS + TOP_K:N_EXPERTS + TOP_K + 1]
    p1 = top_ref[:, N_EXPERTS + TOP_K + 1:N_EXPERTS + TOP_K + 2]
    y = h_ref[...] + p0 * a_ref[...].astype(F32) + p1 * b_ref[...].astype(F32)
    if final_norm:
        y = y * lax.rsqrt(jnp.mean(y * y, axis=-1, keepdims=True) + RMS_EPS) * g_ref[...]
    o_ref[...] = y


def _moe_combine(h, ya, yb, top, gain, final_norm, tm=512):
    n, d = h.shape
    row = pl.BlockSpec((tm, d), lambda i: (i, 0))
    return pl.pallas_call(
        functools.partial(_moe_combine_kernel, final_norm=final_norm),
        grid=(n // tm,),
        in_specs=[row, row, row, pl.BlockSpec((tm, LANES), lambda i: (i, 0)),
                  pl.BlockSpec((1, d), lambda i: (0, 0))],
        out_specs=row,
        out_shape=jax.ShapeDtypeStruct((n, d), F32),
        compiler_params=_cparams(("parallel",)),
        name="moe_combine",
    )(h, ya, yb, top, gain.reshape(1, d).astype(F32))


def _t5_bucket_np(dist):
    n = np.maximum(dist, 0)
    max_exact = REL_BUCKETS // 2
    nf = np.maximum(n, 1).astype(np.float32)
    large = max_exact + (np.log(nf / np.float32(max_exact)) / np.float32(math.log(REL_MAX_DIST / max_exact))
                         * np.float32(REL_BUCKETS - max_exact)).astype(np.int32)
    large = np.minimum(large, REL_BUCKETS - 1)
    return np.where(n < max_exact, n, large).astype(np.int32)


def _band_tables(rel_tab, t):
    q = np.arange(QB)[:, None]
    k = np.arange(QB)[None, :]
    rel = rel_tab.astype(F32) * LOG2E
    toep = jnp.take(rel, jnp.asarray(_t5_bucket_np((q - k) % QB)), axis=0)
    toep = toep.transpose(2, 0, 1)
    before = jnp.asarray(k > q)
    far = rel[REL_BUCKETS - 1, SWA_HEADS:]

    def tiles(tab, fill_prev, extra=()):
        heads = tab.shape[0]
        parts = [jnp.where(before, tab, fill_prev), jnp.where(before, NEG_INF, tab)]
        parts += [jnp.broadcast_to(e, tab.shape) for e in extra]
        return jnp.stack([p.reshape(2, heads // 2 * QB, QB) for p in parts], axis=1)

    tabs_swa = tiles(toep[:SWA_HEADS], NEG_INF)
    tabs_nsa = tiles(toep[SWA_HEADS:] - far[:, None, None], 0.0, extra=[jnp.where(before, 0.0, NEG_INF)])
    ncp = t // NSA_CMP_STRIDE
    m = np.arange(-9, 7)[:, None]
    qr = np.arange(QB)[None, :]
    d = qr - NSA_CMP_STRIDE * m - (NSA_CMP_LEN - 1)
    vals = jnp.take(rel[:, SWA_HEADS:], jnp.asarray(_t5_bucket_np(np.clip(d, 0, None))), axis=0)
    vals = vals.transpose(2, 0, 1) - far[:, None, None]
    band = jnp.where(jnp.asarray(d >= 0), jnp.where(jnp.asarray(d < REL_MAX_DIST), vals, 0.0), NEG_INF)
    ztab = jnp.concatenate([jnp.zeros((NSA_HEADS, ncp - 9, QB), F32), band,
                            jnp.full((NSA_HEADS, ncp - 7, QB), NEG_INF, F32)], axis=1)
    return tabs_swa, tabs_nsa, ztab


def _selection_matrices(t):
    ncp = t // NSA_CMP_STRIDE
    ns = t // NSA_SEL_LEN
    per = NSA_SEL_LEN // NSA_CMP_STRIDE
    ratio = NSA_CMP_LEN // NSA_CMP_STRIDE
    mmat_t = np.zeros((ns, ncp), np.float32)
    for n in range(ncp - 1):
        for j in range(ratio):
            mmat_t[(n + j) // per, n] += 1.0
    emat_t = (np.arange(t)[:, None] // NSA_SEL_LEN == np.arange(LANES)[None, :]).astype(np.float32)
    return jnp.asarray(mmat_t, BF16), (jnp.asarray(emat_t.T, BF16), jnp.asarray(emat_t, BF16))


def _dup(w):
    d = w.shape[0]
    w = w.reshape(d, -1, 1, HEAD_DIM)
    return jnp.broadcast_to(w, (d, w.shape[1], 2, HEAD_DIM)).reshape(d, -1)


def _prep_in_weights(w_in_l):
    sizes = (FOX_W, FOX_W, FOX_W, FOX_HEADS, SWA_W, SWA_KV_W, SWA_KV_W,
             NSA_W, NSA_KV_W, NSA_KV_W, NSA_KV_W, NSA_KV_W, NSA_KV_W, NSA_KV_W, NSA_HEADS * NSA_N_BRANCH)
    splits = [int(s) for s in np.cumsum(sizes)[:-1]]
    (fq, fk, fv, ff, sq, sk, sv, nq, nkc, nvc, nks, nvs, nkw, nvw, ng) = jnp.split(w_in_l, splits, axis=-1)
    scale = HEAD_DIM ** -0.5 * LOG2E
    cols = [fq * scale, fk, fv, sq * scale, nq * scale, _dup(sk), _dup(nks), _dup(nkw), sv, nvs, nvw, nkc, nvc]
    cols.append(jnp.zeros((w_in_l.shape[0], D_PROJ - sum(c.shape[1] for c in cols)), w_in_l.dtype))
    w_proj = jnp.concatenate(cols, axis=-1).astype(BF16)
    pad = jnp.zeros((w_in_l.shape[0], D_GATE - FOX_HEADS - NSA_HEADS * NSA_N_BRANCH), w_in_l.dtype)
    w_gate = jnp.concatenate([ff, ng, pad], axis=-1).astype(BF16)
    return w_proj, w_gate


def _compress_inputs(proj, b, t):
    g = NSA_KV_HEADS
    nck = t // NSA_CMP_STRIDE
    x = proj[:, C_NKC:C_NKC + 2 * NSA_KV_W].reshape(b, nck, NSA_CMP_STRIDE, 2, g, HEAD_DIM)
    return x.transpose(3, 0, 4, 1, 2, 5).reshape(2, b * g, nck, NSA_CMP_STRIDE * HEAD_DIM)


def _mixer(h, gain, b, t, w_in_l, forget_bias, sinks, cmp_pos, cmp_w1, cmp_b1, cmp_w2, cmp_b2, tables):
    tabs_swa, tabs_nsa, ztab, mmat_t, emats = tables
    w_proj, w_gate = _prep_in_weights(w_in_l)
    proj, gates = _in_proj(h, gain, w_proj, w_gate)
    fv_t = proj[:, C_FV:C_FV + FOX_W].reshape(b, t, FOX_W).transpose(0, 2, 1)
    v_t = proj[:, C_V:C_V + 3 * NSA_KV_W].reshape(b, t, 3 * NSA_KV_W).transpose(0, 2, 1)

    qx, kx = _fox_decay_operands(gates, forget_bias, b, t)
    o_fox = _fox_attention(proj, fv_t, qx, kx, b, t)

    o_swa = _band_attention(proj, v_t, tabs_swa, b, t, mode="swa", hg=SWA_GROUP,
                            c_q=C_SQ, c_k=C_SK, v_blk=0, sinks=sinks)

    flat = _compress_inputs(proj, b, t)
    pos = cmp_pos.reshape(2, 1, NSA_CMP_LEN * HEAD_DIM).astype(F32)
    w2d = jnp.concatenate([cmp_w2, cmp_w2], axis=-1).astype(BF16)
    b2d = jnp.concatenate([cmp_b2, cmp_b2], axis=-1).reshape(2, 1, LANES).astype(F32)
    kvc = _compress(flat, pos, cmp_w1.astype(BF16), cmp_b1.reshape(2, 1, NSA_CMP_HIDDEN).astype(F32), w2d, b2d)
    ncp = t // NSA_CMP_STRIDE
    kvc = kvc.reshape(2, b, NSA_KV_HEADS, ncp, LANES)
    vc_t = kvc[1, :, :, :, :HEAD_DIM].transpose(0, 1, 3, 2)
    o_cmp, sel = _cmp_attention(proj, kvc[0], vc_t, ztab, mmat_t, b, t)
    o_slc = _band_attention(proj, v_t, tabs_nsa, b, t, mode="sel", hg=NSA_GROUP,
                            c_q=C_NQ, c_k=C_NKS, v_blk=2, sel=sel, emats=emats)
    o_win = _band_attention(proj, v_t, tabs_nsa, b, t, mode="win", hg=NSA_GROUP,
                            c_q=C_NQ, c_k=C_NKW, v_blk=4)
    o_nsa = _nsa_combine(o_cmp, o_slc, o_win, gates)
    return o_fox, o_swa, o_nsa


def _moe(hn, h, router, w_gate, w_up, w_down, gain, final_norm, tm=1024, tf=512):
    n, d = hn.shape
    w_r = jnp.zeros((d, LANES), BF16).at[:, :N_EXPERTS].set(router.astype(BF16))
    logits = _matmul([hn], [w_r], F32, tn=LANES, name="router_logits")
    top = _router_top2(logits)
    e_idx = top[:, N_EXPERTS:N_EXPERTS + TOP_K].astype(jnp.int32)
    e_flat = e_idx.reshape(-1)
    onehot = (e_flat[:, None] == jnp.arange(N_EXPERTS)[None, :]).astype(jnp.int32)
    csum = jnp.cumsum(onehot, axis=0)
    counts = csum[-1]
    rank = jnp.take_along_axis(csum, e_flat[:, None], axis=1)[:, 0] - 1
    padded = ((counts + tm - 1) // tm) * tm
    ends = jnp.cumsum(padded)
    starts = ends - padded
    dest = starts[e_flat] + rank
    r_pad = n * TOP_K + N_EXPERTS * tm
    src_tok = jnp.zeros((r_pad,), jnp.int32).at[dest].set(jnp.arange(n * TOP_K, dtype=jnp.int32) // TOP_K)
    tile_start = jnp.arange(r_pad // tm, dtype=jnp.int32) * tm
    tile_expert = jnp.minimum(jnp.sum(tile_start[:, None] >= ends[None, :], axis=1), N_EXPERTS - 1).astype(jnp.int32)
    n_used = (ends[-1:] // tm).astype(jnp.int32)
    tile_rows = jnp.clip((starts + counts)[tile_expert] - tile_start, 0, tm).astype(jnp.int32)
    n_tiles = r_pad // tm
    per = n_tiles // MOE_CHUNKS
    y = jnp.zeros((r_pad, d), BF16)
    for c in range(MOE_CHUNKS):
        xs = hn.at[src_tok[c * per * tm:(c + 1) * per * tm]].get(mode="promise_in_bounds")
        used_c = jnp.clip(n_used - c * per, 0, per)
        y = _ffn(xs, w_gate, w_up, w_down, tile_expert[c * per:(c + 1) * per], used_c,
                 tile_rows[c * per:(c + 1) * per], BF16, tm=tm, tf=tf,
                 out_rows=r_pad, out_tile0=c * per, out_buf=y)
    dest = dest.reshape(n, TOP_K)
    ya = y.at[dest[:, 0]].get(mode="promise_in_bounds")
    yb = y.at[dest[:, 1]].get(mode="promise_in_bounds")
    return _moe_combine(h, ya, yb, top, gain, final_norm)


def kernel(x, attn_norm, w_in, fox_forget_bias, swa_sinks, nsa_cmp_pos, nsa_cmp_w1, nsa_cmp_b1, nsa_cmp_w2,
           nsa_cmp_b2, w_out, rel_bias_table, ffn_norm, dense_w_gate, dense_w_up, dense_w_down, moe_router,
           moe_w_gate, moe_w_up, moe_w_down, final_norm):
    b, t, d = x.shape
    n = b * t
    depth = w_in.shape[0]
    tables = _band_tables(rel_bias_table, t) + _selection_matrices(t)
    h = x.reshape(n, d)
    for layer in range(depth):
        o_fox, o_swa, o_nsa = _mixer(h, attn_norm[layer], b, t, w_in[layer], fox_forget_bias[layer],
                                     swa_sinks[layer], nsa_cmp_pos[layer], nsa_cmp_w1[layer], nsa_cmp_b1[layer],
                                     nsa_cmp_w2[layer], nsa_cmp_b2[layer], tables)
        wo = w_out[layer].astype(BF16)
        h = _matmul([o_fox, o_swa, o_nsa], [wo[:FOX_W], wo[FOX_W:FOX_W + SWA_W], wo[FOX_W + SWA_W:]],
                    F32, residual=h, tn=1024, name="out_proj")
        i = layer // 2
        last = layer == depth - 1
        if layer % 2 == 0:
            tm = 512
            zeros = jnp.zeros((n // tm,), jnp.int32)
            h = _ffn(h, dense_w_gate[i][None].astype(BF16), dense_w_up[i][None].astype(BF16),
                     dense_w_down[i][None].astype(BF16), zeros, jnp.full((1,), n // tm, jnp.int32),
                     jnp.full((n // tm,), tm, jnp.int32), F32, gain=ffn_norm[layer], tm=tm)
            if last:
                h = _rmsnorm(h, final_norm, F32)
        else:
            hn = _rmsnorm(h, ffn_norm[layer], BF16)
            h = _moe(hn, h, moe_router[i], moe_w_gate[i], moe_w_up[i], moe_w_down[i], final_norm, last)
    return h.reshape(b, t, d)
```

```python
import functools
import math

import numpy as np
import jax
import jax.numpy as jnp
from jax import lax
from jax.experimental import pallas as pl
from jax.experimental.pallas import tpu as pltpu

F32 = jnp.float32
BF16 = jnp.bfloat16

D_MODEL = 2048
HEAD_DIM = 64
FOX_HEADS = 8
FOX_W = FOX_HEADS * HEAD_DIM
SWA_HEADS = 8
SWA_KV_HEADS = 2
SWA_GROUP = SWA_HEADS // SWA_KV_HEADS
SWA_W = SWA_HEADS * HEAD_DIM
SWA_KV_W = SWA_KV_HEADS * HEAD_DIM
SWA_WINDOW = 128
NSA_HEADS = 16
NSA_KV_HEADS = 2
NSA_GROUP = NSA_HEADS // NSA_KV_HEADS
NSA_W = NSA_HEADS * HEAD_DIM
NSA_KV_W = NSA_KV_HEADS * HEAD_DIM
NSA_CMP_LEN = 32
NSA_CMP_STRIDE = 16
NSA_CMP_HIDDEN = 256
NSA_SEL_LEN = 64
NSA_SEL_TOPK = 16
NSA_WINDOW = 512
NSA_N_BRANCH = 3
FORCE_SCORE = 1e9
REL_BUCKETS = 32
REL_MAX_DIST = 128
D_FF = 5632
N_EXPERTS = 8
TOP_K = 2
RMS_EPS = 1e-6
NEG_INF = -1e30
M_INIT = -1e29
REMOVED = -3e38
LOG2E = 1.4426950408889634
ONES_ROWS = 16
BAND_TILES = {"swa": 8, "win": 8, "sel": 4}
CMP_TILES = 8
MOE_CHUNKS = 8

LANES = 128
QB = 128
VMEM_LIMIT = 56 * 1024 * 1024
FFN_VMEM_LIMIT = 60 * 1024 * 1024

C_FQ, C_FK, C_FV = 0, 512, 1024
C_SQ = 1536
C_NQ = 2048
C_SK, C_NKS, C_NKW = 3072, 3328, 3584
C_V = 3840
C_NKC, C_NVC = 4224, 4352
D_PROJ = 4608
PROJ_TN = 1536
G_NG = FOX_HEADS
D_GATE = 128


def _cparams(sem, vmem=VMEM_LIMIT):
    return pltpu.CompilerParams(dimension_semantics=sem, vmem_limit_bytes=vmem)


def _dot(a, b):
    return jnp.dot(a, b, preferred_element_type=F32)


def _dot_nt(a, b):
    return lax.dot_general(a, b, (((1,), (1,)), ((), ())), preferred_element_type=F32)


def _split3(x):
    hi = x.astype(BF16)
    r1 = x - hi.astype(F32)
    mid = r1.astype(BF16)
    lo = (r1 - mid.astype(F32)).astype(BF16)
    return hi, mid, lo


def _pipeline3(n, stage1, stage2, stage3):
    first, second, outs = {}, {}, []
    for step in range(n + 2):
        if step < n:
            first[step] = stage1(step)
        if 0 <= step - 1 < n:
            second[step - 1] = stage2(step - 1, first.pop(step - 1))
        if 0 <= step - 2 < n:
            outs.append(stage3(step - 2, *second.pop(step - 2)))
    return outs


def _rmsnorm_kernel(x_ref, g_ref, o_ref):
    x = x_ref[...]
    y = x * lax.rsqrt(jnp.mean(x * x, axis=-1, keepdims=True) + RMS_EPS)
    o_ref[...] = (y * g_ref[...]).astype(o_ref.dtype)


def _rmsnorm(h, g, out_dtype, tm=512):
    n, d = h.shape
    return pl.pallas_call(
        _rmsnorm_kernel,
        grid=(n // tm,),
        in_specs=[pl.BlockSpec((tm, d), lambda i: (i, 0)),
                  pl.BlockSpec((1, d), lambda i: (0, 0))],
        out_specs=pl.BlockSpec((tm, d), lambda i: (i, 0)),
        out_shape=jax.ShapeDtypeStruct((n, d), out_dtype),
        compiler_params=_cparams(("parallel",)),
        name="rmsnorm",
    )(h, g.reshape(1, d).astype(F32))


def _mm_kernel(*refs, n_in, has_res):
    o_ref = refs[-1]
    acc = None
    for a in range(n_in):
        d = _dot(refs[a][...], refs[n_in + a][...].astype(BF16))
        acc = d if acc is None else acc + d
    if has_res:
        acc = acc + refs[2 * n_in][...]
    o_ref[...] = acc.astype(o_ref.dtype)


def _matmul(xs, ws, out_dtype, residual=None, tm=1024, tn=256, name="matmul"):
    n = xs[0].shape[0]
    m = ws[0].shape[1]
    tm = min(tm, n)
    tn = min(tn, m)
    in_specs = [pl.BlockSpec((tm, x.shape[1]), lambda i, j: (i, 0)) for x in xs]
    in_specs += [pl.BlockSpec((w.shape[0], tn), lambda i, j: (0, j)) for w in ws]
    args = list(xs) + list(ws)
    if residual is not None:
        in_specs.append(pl.BlockSpec((tm, tn), lambda i, j: (i, j)))
        args.append(residual)
    return pl.pallas_call(
        functools.partial(_mm_kernel, n_in=len(xs), has_res=residual is not None),
        grid=(n // tm, m // tn),
        in_specs=in_specs,
        out_specs=pl.BlockSpec((tm, tn), lambda i, j: (i, j)),
        out_shape=jax.ShapeDtypeStruct((n, m), out_dtype),
        compiler_params=_cparams(("parallel", "arbitrary")),
        name=name,
    )(*args)


def _in_proj_kernel(h_ref, g_ref, w_ref, wgate_ref, proj_ref, gates_ref, xn_ref):
    @pl.when(pl.program_id(1) == 0)
    def _():
        h = h_ref[...]
        y = h * lax.rsqrt(jnp.mean(h * h, axis=-1, keepdims=True) + RMS_EPS)
        xn_ref[...] = (y * g_ref[...]).astype(BF16)
        gates_ref[...] = _dot(xn_ref[...], wgate_ref[...])

    proj_ref[...] = _dot(xn_ref[...], w_ref[...]).astype(proj_ref.dtype)


def _in_proj(h, gain, w_proj, w_gate, tm=1024, tn=PROJ_TN):
    n, d = h.shape
    return pl.pallas_call(
        _in_proj_kernel,
        grid=(n // tm, D_PROJ // tn),
        in_specs=[pl.BlockSpec((tm, d), lambda i, j: (i, 0)),
                  pl.BlockSpec((1, d), lambda i, j: (0, 0)),
                  pl.BlockSpec((d, tn), lambda i, j: (0, j)),
                  pl.BlockSpec((d, D_GATE), lambda i, j: (0, 0))],
        out_specs=[pl.BlockSpec((tm, tn), lambda i, j: (i, j)),
                   pl.BlockSpec((tm, D_GATE), lambda i, j: (i, 0))],
        out_shape=[jax.ShapeDtypeStruct((n, D_PROJ), BF16), jax.ShapeDtypeStruct((n, D_GATE), F32)],
        scratch_shapes=[pltpu.VMEM((tm, d), BF16)],
        compiler_params=_cparams(("parallel", "arbitrary")),
        name="in_proj",
    )(h, gain.reshape(1, d).astype(F32), w_proj, w_gate)


def _cumsum_kernel(g_ref, b_ref, pq_ref, pk_ref, oq_ref, ok_ref, qx_ref, kx_ref, carry_ref, *, tc):
    @pl.when(pl.program_id(1) == 0)
    def _():
        carry_ref[...] = jnp.zeros_like(carry_ref)

    z = g_ref[...] + b_ref[...]
    log_f = jnp.minimum(z, 0.0) - jnp.log1p(jnp.exp(-jnp.abs(z)))
    row = lax.broadcasted_iota(jnp.int32, (tc, tc), 0)
    col = lax.broadcasted_iota(jnp.int32, (tc, tc), 1)
    tri = jnp.where(col <= row, 1.0, 0.0).astype(BF16)
    hi, mid, lo = _split3(log_f)
    c = _dot(tri, hi) + _dot(tri, mid) + _dot(tri, lo) + carry_ref[...]
    carry_ref[...] = c[tc - 1:tc, :]
    parts = jnp.concatenate(_split3(c * LOG2E), axis=1)
    qx_ref[...] = (_dot(parts, pq_ref[...]) + oq_ref[...]).astype(qx_ref.dtype)
    kx_ref[...] = (_dot(parts, pk_ref[...]) + ok_ref[...]).astype(kx_ref.dtype)


def _fox_decay_operands(gates, forget_bias, b, t, tc=512):
    nt = t // tc
    wide = FOX_HEADS * LANES
    bias = jnp.zeros((1, D_GATE), F32).at[0, :FOX_HEADS].set(forget_bias.astype(F32))
    pq = np.zeros((3 * D_GATE, wide), np.float32)
    pk = np.zeros((3 * D_GATE, wide), np.float32)
    oq = np.zeros((1, wide), np.float32)
    ok = np.zeros((1, wide), np.float32)
    for h in range(FOX_HEADS):
        for part in range(3):
            pq[part * D_GATE + h, h * LANES + part] = 1.0
            pk[part * D_GATE + h, h * LANES + 3 + part] = -1.0
            oq[0, h * LANES + 3 + part] = 1.0
            ok[0, h * LANES + part] = 1.0
    const = lambda shape: pl.BlockSpec(shape, lambda bi, ti: (0, 0))
    out_spec = pl.BlockSpec((tc, wide), lambda bi, ti: (bi * nt + ti, 0))
    return pl.pallas_call(
        functools.partial(_cumsum_kernel, tc=tc),
        grid=(b, nt),
        in_specs=[pl.BlockSpec((tc, D_GATE), lambda bi, ti: (bi * nt + ti, 0)),
                  const((1, D_GATE)), const((3 * D_GATE, wide)), const((3 * D_GATE, wide)),
                  const((1, wide)), const((1, wide))],
        out_specs=[out_spec, out_spec],
        out_shape=[jax.ShapeDtypeStruct((b * t, wide), BF16), jax.ShapeDtypeStruct((b * t, wide), BF16)],
        scratch_shapes=[pltpu.VMEM((1, D_GATE), F32)],
        compiler_params=_cparams(("parallel", "arbitrary")),
        name="fox_cumsum",
    )(gates, bias, jnp.asarray(pq, BF16), jnp.asarray(pk, BF16), jnp.asarray(oq), jnp.asarray(ok))


def _fox_kernel(qi_ref, kj_ref, q_ref, k_ref, vt_ref, qx_ref, kx_ref, o_ref, m_sc, acc_sc, *, tq):
    i = qi_ref[pl.program_id(1)]
    j = kj_ref[pl.program_id(1)]
    lane = lax.broadcasted_iota(jnp.int32, (1, LANES), 1)
    upper = lane >= HEAD_DIM

    @pl.when(j == 0)
    def _():
        m_sc[...] = jnp.full_like(m_sc, M_INIT)
        acc_sc[...] = jnp.zeros_like(acc_sc)

    def step(diagonal):
        if diagonal:
            key = lax.broadcasted_iota(jnp.int32, (tq, tq), 0)
            qry = lax.broadcasted_iota(jnp.int32, (tq, tq), 1)
            causal = key <= qry
        ones = jnp.ones((ONES_ROWS, tq), BF16)

        def qk(h):
            p, a = divmod(h, 2)
            q2 = q_ref[:, p * LANES:(p + 1) * LANES]
            k2 = k_ref[:, p * LANES:(p + 1) * LANES]
            zero = jnp.zeros_like(q2)
            qa = jnp.where(upper, q2, zero) if a else jnp.where(upper, zero, q2)
            q_aug = jnp.concatenate([qa, qx_ref[:, h * LANES:(h + 1) * LANES]], axis=1)
            k_aug = jnp.concatenate([k2, kx_ref[:, h * LANES:(h + 1) * LANES]], axis=1)
            s = _dot_nt(k_aug, q_aug)
            if diagonal:
                s = jnp.where(causal, s, NEG_INF)
            return s, jnp.max(s, axis=0, keepdims=True)

        def softmax(h, scored):
            s, s_max = scored
            m_prev = m_sc[h]
            m_new = jnp.maximum(m_prev, s_max)
            m_sc[h] = m_new
            return jnp.exp2(m_prev - m_new), jnp.exp2(s - m_new).astype(BF16)

        def pv(h, alpha, pr):
            vt = jnp.concatenate([vt_ref[0, h * HEAD_DIM:(h + 1) * HEAD_DIM, :], ones], axis=0)
            acc_sc[h] = alpha * acc_sc[h] + _dot(vt, pr)

        _pipeline3(FOX_HEADS, qk, softmax, pv)

    @pl.when(j < i)
    def _():
        step(False)

    @pl.when(j == i)
    def _():
        step(True)
        for p in range(FOX_HEADS // 2):
            outs = []
            for h in (2 * p, 2 * p + 1):
                acc = acc_sc[h]
                outs.append(acc[:HEAD_DIM] * (1.0 / acc[HEAD_DIM:HEAD_DIM + 1]))
            o_ref[:, p * LANES:(p + 1) * LANES] = jnp.concatenate(outs, axis=0).T.astype(o_ref.dtype)


def _fox_attention(proj, v_t, qx, kx, b, t, tq=512):
    nt = t // tq
    wide = FOX_HEADS * LANES
    pairs = [(i, j) for i in range(nt) for j in range(i + 1)]
    qi = jnp.asarray([p[0] for p in pairs], jnp.int32)
    kj = jnp.asarray([p[1] for p in pairs], jnp.int32)
    return pl.pallas_call(
        functools.partial(_fox_kernel, tq=tq),
        grid_spec=pltpu.PrefetchScalarGridSpec(
            num_scalar_prefetch=2,
            grid=(b, len(pairs)),
            in_specs=[
                pl.BlockSpec((tq, FOX_W), lambda bi, s, qi, kj: (bi * nt + qi[s], C_FQ // FOX_W)),
                pl.BlockSpec((tq, FOX_W), lambda bi, s, qi, kj: (bi * nt + kj[s], C_FK // FOX_W)),
                pl.BlockSpec((1, FOX_W, tq), lambda bi, s, qi, kj: (bi, 0, kj[s])),
                pl.BlockSpec((tq, wide), lambda bi, s, qi, kj: (bi * nt + qi[s], 0)),
                pl.BlockSpec((tq, wide), lambda bi, s, qi, kj: (bi * nt + kj[s], 0)),
            ],
            out_specs=pl.BlockSpec((tq, FOX_W), lambda bi, s, qi, kj: (bi * nt + qi[s], 0)),
            scratch_shapes=[pltpu.VMEM((FOX_HEADS, 1, tq), F32),
                            pltpu.VMEM((FOX_HEADS, HEAD_DIM + ONES_ROWS, tq), F32)],
        ),
        out_shape=jax.ShapeDtypeStruct((b * t, FOX_W), BF16),
        compiler_params=_cparams(("parallel", "arbitrary")),
        name="fox_attention",
    )(qi, kj, proj, proj, v_t, qx, kx)


def _stack_heads(q_ref, hg, r0=0):
    lane = lax.broadcasted_iota(jnp.int32, (1, LANES), 1)
    upper = lane >= HEAD_DIM
    qs = []
    for p in range(hg // 2):
        q2 = q_ref[r0:r0 + QB, p * LANES:(p + 1) * LANES]
        qs.append(jnp.where(upper, jnp.zeros_like(q2), q2))
        qs.append(jnp.where(upper, q2, jnp.zeros_like(q2)))
    return jnp.concatenate(qs, axis=0)


def _store_heads(o_ref, o_t, hg, r0=0):
    for p in range(hg // 2):
        pair = jnp.concatenate([o_t[:, (2 * p) * QB:(2 * p + 1) * QB],
                                o_t[:, (2 * p + 1) * QB:(2 * p + 2) * QB]], axis=0)
        o_ref[r0:r0 + QB, p * LANES:(p + 1) * LANES] = pair.T.astype(o_ref.dtype)


def _band_kernel(*refs, mode, hg, chunk, tiles):
    for sub in range(tiles):
        _band_tile(*refs, mode=mode, hg=hg, chunk=chunk, i=pl.program_id(2) * tiles + sub, r0=sub * QB)


def _band_tile(*refs, mode, hg, chunk, i, r0):
    if mode == "swa":
        q_ref, k_ref, vt_ref, tab_ref, sink_ref, o_ref = refs
    elif mode == "sel":
        (q_ref, k_ref, vt_ref, tab_ref, sel_ref, e_ref, et_ref, o_ref,
         sa_ref, sb_ref, ma_ref, mb_ref, m_ref, acc_ref) = refs
    else:
        q_ref, k_ref, vt_ref, tab_ref, o_ref = refs
    g = pl.program_id(1)
    cols = hg * QB
    qstack = _stack_heads(q_ref, hg, r0)
    eye = jnp.where(lax.broadcasted_iota(jnp.int32, (QB, QB), 0) == lax.broadcasted_iota(jnp.int32, (QB, QB), 1),
                    1.0, 0.0).astype(BF16)

    def k_block(blk):
        return k_ref[pl.ds(pl.multiple_of(blk * QB, QB), QB), :]

    def vt_rows(start, size):
        return jnp.concatenate([vt_ref[0, :, pl.ds(start, size)], jnp.ones((ONES_ROWS, size), BF16)], axis=0)

    def vt_block(blk):
        return vt_rows(pl.multiple_of(blk * QB, QB), QB)

    def with_table(table):
        return jnp.concatenate([qstack, table.astype(BF16)], axis=1)

    def grp(x, j):
        return x[j * GW:(j + 1) * GW]

    def lanes(x, j):
        return x[:, j * GW:(j + 1) * GW]

    GW = cols if mode == "sel" else 2 * QB
    n_grp = cols // GW
    ip = jnp.maximum(i - 1, 0)
    t_prev = tab_ref[0, 0]
    if mode == "sel":
        sel_q = sel_ref[0, 0, r0:r0 + QB, :]
        allowed = _dot(sel_q, e_ref[:, pl.ds(pl.multiple_of(ip * QB, QB), QB)])
        t_prev = jnp.where(jnp.concatenate([allowed] * hg, axis=0) > 0.5, t_prev, NEG_INF)
    t_prev = jnp.where(i > 0, t_prev, NEG_INF)
    q_prev = with_table(t_prev)
    q_cur = with_table(tab_ref[0, 1])
    k_prev = jnp.concatenate([k_block(ip), eye], axis=1)
    k_cur = jnp.concatenate([k_block(i), eye], axis=1)
    vt_near = jnp.concatenate([vt_block(ip), vt_block(i)], axis=1)

    if mode == "swa":
        head = lax.broadcasted_iota(jnp.int32, (1, cols), 1) // QB
        sink = jnp.zeros((1, cols), F32)
        for h in range(hg):
            sink = jnp.where(head == h, sink_ref[g * hg + h], sink)

    elif mode == "win":
        n_far = NSA_WINDOW // QB - 1
        backs = range(n_far + 1, 1, -1)
        onehot = jnp.concatenate([eye, jnp.zeros(((n_far - 1) * QB, QB), BF16)], axis=0)
        k_far = jnp.concatenate(
            [jnp.concatenate([k_block(jnp.maximum(i - bk, 0)) for bk in backs], axis=0), onehot], axis=1)
        vt_far = jnp.concatenate([vt_block(jnp.maximum(i - bk, 0)) for bk in backs], axis=1)
        q_tri = with_table(tab_ref[0, 2])
        in_seq = lax.broadcasted_iota(jnp.int32, (n_far * QB, 1), 0) >= (n_far + 1 - i) * QB

    else:
        blk = lax.broadcasted_iota(jnp.int32, (QB, LANES), 1)
        pick = jnp.logical_and(sel_q.astype(F32) > 0.5, blk < 2 * (i - 1))
        sel_bias = jnp.where(pick, 0.0, NEG_INF)
        q_far = with_table(jnp.concatenate([sel_bias] * hg, axis=0))
        per = chunk // QB
        n_chunks = (jnp.maximum(i - 1, 0) + per - 1) // per
        def far_scores(c):
            off = pl.multiple_of(c * chunk, chunk)
            k_rows = jnp.concatenate([k_ref[pl.ds(off, chunk), :], et_ref[pl.ds(off, chunk), :]], axis=1)
            return _dot_nt(k_rows, q_far)

        def put_scores(c, s_ref, smax_ref):
            s = far_scores(c)
            s_ref[...] = s
            smax_ref[...] = jnp.max(s, axis=0, keepdims=True)

        def sweep(c, cur, nxt):
            put_scores(jnp.minimum(c + 1, n_chunks - 1), *nxt)
            s_ref, smax_ref = cur
            m_old = m_ref[...]
            m_new = jnp.maximum(m_old, smax_ref[...])
            pf = jnp.exp2(s_ref[...] - m_new).astype(BF16)
            acc_ref[...] = (jnp.exp2(m_old - m_new) * acc_ref[...]
                            + _dot(vt_rows(pl.multiple_of(c * chunk, chunk), chunk), pf))
            m_ref[...] = m_new

        buf_a, buf_b = (sa_ref, ma_ref), (sb_ref, mb_ref)

        def body(quad, carry):
            for u in range(4):
                sweep(4 * quad + u, *((buf_a, buf_b) if u % 2 == 0 else (buf_b, buf_a)))
            return carry

        m_ref[...] = jnp.full((1, cols), M_INIT, F32)
        acc_ref[...] = jnp.zeros((HEAD_DIM + ONES_ROWS, cols), F32)
        put_scores(0, *buf_a)
        lax.fori_loop(0, n_chunks // 4, body, 0)
        done = (n_chunks // 4) * 4

        @pl.when(n_chunks % 4 >= 2)
        def _():
            sweep(done, buf_a, buf_b)
            sweep(done + 1, buf_b, buf_a)

        @pl.when(n_chunks % 2 == 1)
        def _():
            sweep(n_chunks - 1, buf_a, buf_b)

        m_far = m_ref[...]
        acc_far = acc_ref[...]

    def near_scores(j):
        parts = [_dot_nt(k_prev, grp(q_prev, j)), _dot_nt(k_cur, grp(q_cur, j))]
        if mode == "win":
            parts.append(jnp.where(in_seq, _dot_nt(k_far, grp(q_tri, j)), NEG_INF))
        return parts

    def near_softmax(j, parts):
        m = jnp.max(parts[0], axis=0, keepdims=True)
        for s in parts[1:]:
            m = jnp.maximum(m, jnp.max(s, axis=0, keepdims=True))
        if mode == "swa":
            m = jnp.maximum(m, lanes(sink, j))
        if mode == "sel":
            m = jnp.maximum(m, lanes(m_far, j))
        return m, [jnp.exp2(s - m).astype(BF16) for s in parts]

    def near_output(j, m, probs):
        acc = _dot(vt_near, jnp.concatenate(probs[:2], axis=0))
        if mode == "win":
            acc = acc + _dot(vt_far, probs[2])
        if mode == "sel":
            acc = jnp.exp2(lanes(m_far, j) - m) * lanes(acc_far, j) + acc
        den = acc[HEAD_DIM:HEAD_DIM + 1]
        if mode == "swa":
            den = den + jnp.exp2(lanes(sink, j) - m)
        o_t = acc[:HEAD_DIM] * (1.0 / den)
        for pp in range(GW // (2 * QB)):
            pair = jnp.concatenate([o_t[:, 2 * pp * QB:(2 * pp + 1) * QB],
                                    o_t[:, (2 * pp + 1) * QB:(2 * pp + 2) * QB]], axis=0)
            c0 = (j * (GW // (2 * QB)) + pp) * LANES
            o_ref[r0:r0 + QB, c0:c0 + LANES] = pair.T.astype(o_ref.dtype)

    _pipeline3(n_grp, near_scores, near_softmax, near_output)


def _band_attention(proj, v_t, tabs, b, t, *, mode, hg, c_q, c_k, v_blk, sinks=None, sel=None, emats=None):
    tiles = BAND_TILES[mode]
    nb = t // (QB * tiles)
    n_groups = 2
    qw = hg * HEAD_DIM
    chunk = min(512, t)
    in_specs = [
        pl.BlockSpec((QB * tiles, qw), lambda bi, g, i: (bi * nb + i, c_q // qw + g)),
        pl.BlockSpec((t, LANES), lambda bi, g, i: (bi, c_k // LANES + g)),
        pl.BlockSpec((1, HEAD_DIM, t), lambda bi, g, i: (bi, v_blk + g, 0)),
        pl.BlockSpec((1,) + tabs.shape[1:], lambda bi, g, i: (g, 0, 0, 0)),
    ]
    args = [proj, proj, v_t, tabs]
    scratch = []
    if mode == "swa":
        in_specs.append(pl.BlockSpec(memory_space=pltpu.SMEM))
        args.append(sinks.astype(F32) * LOG2E)
    if mode == "sel":
        emat, emat_t = emats
        in_specs.append(pl.BlockSpec((1, 1, QB * tiles, LANES), lambda bi, g, i: (bi, g, i, 0)))
        in_specs.append(pl.BlockSpec((LANES, t), lambda bi, g, i: (0, 0)))
        in_specs.append(pl.BlockSpec((t, LANES), lambda bi, g, i: (0, 0)))
        args += [sel, emat, emat_t]
        scratch = [pltpu.VMEM((chunk, hg * QB), F32), pltpu.VMEM((chunk, hg * QB), F32),
                   pltpu.VMEM((1, hg * QB), F32), pltpu.VMEM((1, hg * QB), F32),
                   pltpu.VMEM((1, hg * QB), F32), pltpu.VMEM((HEAD_DIM + ONES_ROWS, hg * QB), F32)]
    return pl.pallas_call(
        functools.partial(_band_kernel, mode=mode, hg=hg, chunk=chunk, tiles=tiles),
        grid=(b, n_groups, nb),
        in_specs=in_specs,
        out_specs=pl.BlockSpec((QB * tiles, qw), lambda bi, g, i: (bi * nb + i, g)),
        out_shape=jax.ShapeDtypeStruct((b * t, n_groups * qw), BF16),
        scratch_shapes=scratch,
        compiler_params=_cparams(("parallel", "parallel", "arbitrary")),
        name="band_" + mode,
    )(*args)


def _compress_kernel(x_ref, pos_ref, w1_ref, b1_ref, w2_ref, b2_ref, o_ref, *, nck):
    half = NSA_CMP_STRIDE * HEAD_DIM
    c = x_ref[0, 0].astype(F32)
    first = _dot((c + pos_ref[0, :, :half]).astype(BF16), w1_ref[0, :half, :])
    second = _dot((c + pos_ref[0, :, half:]).astype(BF16), w1_ref[0, half:, :])
    hid = jax.nn.gelu(first + pltpu.roll(second, nck - 1, 0) + b1_ref[0])
    o_ref[0, 0] = (_dot(hid.astype(BF16), w2_ref[0]) + b2_ref[0]).astype(o_ref.dtype)


def _compress(chunks, pos, w1, b1, w2d, b2d):
    _, bg, nck, cin = chunks.shape
    return pl.pallas_call(
        functools.partial(_compress_kernel, nck=nck),
        grid=(2, bg),
        in_specs=[pl.BlockSpec((1, 1, nck, cin), lambda s, i: (s, i, 0, 0)),
                  pl.BlockSpec((1, 1, 2 * cin), lambda s, i: (s, 0, 0)),
                  pl.BlockSpec((1, 2 * cin, NSA_CMP_HIDDEN), lambda s, i: (s, 0, 0)),
                  pl.BlockSpec((1, 1, NSA_CMP_HIDDEN), lambda s, i: (s, 0, 0)),
                  pl.BlockSpec((1, NSA_CMP_HIDDEN, LANES), lambda s, i: (s, 0, 0)),
                  pl.BlockSpec((1, 1, LANES), lambda s, i: (s, 0, 0))],
        out_specs=pl.BlockSpec((1, 1, nck, LANES), lambda s, i: (s, i, 0, 0)),
        out_shape=jax.ShapeDtypeStruct((2, bg, nck, LANES), BF16),
        compiler_params=_cparams(("parallel", "parallel")),
        name="nsa_compress",
    )(chunks, pos, w1, b1, w2d, b2d)


def _cmp_kernel(q_ref, kc_ref, vct_ref, ztab_ref, mmat_ref, o_ref, sel_ref, *, hg, ncp, ns, n_sel, tiles):
    for sub in range(tiles):
        _cmp_tile(q_ref, kc_ref, vct_ref, ztab_ref, mmat_ref, o_ref, sel_ref, pl.program_id(2) * tiles + sub,
                  sub * QB, hg=hg, ncp=ncp, ns=ns, n_sel=n_sel)


def _cmp_tile(q_ref, kc_ref, vct_ref, ztab_ref, mmat_ref, o_ref, sel_ref, i, r0, *, hg, ncp, ns, n_sel):
    cols = hg * QB
    qstack = _stack_heads(q_ref, hg, r0)
    s = _dot_nt(kc_ref[0, 0], qstack)
    start = pl.multiple_of(ncp - 8 * i, 8)
    delta = jnp.concatenate([ztab_ref[h, pl.ds(start, ncp), :] for h in range(hg)], axis=1)
    s = s + delta
    m = jnp.max(s, axis=0, keepdims=True)
    e = jnp.exp2(s - m)
    p = e * jnp.where(m > M_INIT, 1.0 / jnp.sum(e, axis=0, keepdims=True), 0.0)
    _store_heads(o_ref, _dot(vct_ref[0, 0], p.astype(BF16)), hg, r0)

    imp = p[:, 0:QB]
    for h in range(1, hg):
        imp = imp + p[:, h * QB:(h + 1) * QB]
    hi, mid, lo = _split3(imp)
    mm = mmat_ref[...]
    imp_sel = _dot(mm, hi) + _dot(mm, mid) + _dot(mm, lo)
    sb = lax.broadcasted_iota(jnp.int32, (ns, QB), 0)
    tb = (i * QB + lax.broadcasted_iota(jnp.int32, (ns, QB), 1)) // NSA_SEL_LEN
    forced = jnp.logical_or(jnp.logical_or(sb == 0, sb == tb), sb == tb - 1)
    score = jnp.where(forced, FORCE_SCORE, jnp.where(sb <= tb, imp_sel, NEG_INF))
    sub = 8
    rows = [score[v * sub:(v + 1) * sub, :] for v in range(ns // sub)]
    ranks = [jnp.zeros((sub, QB), F32) for _ in rows]
    sub_idx = lax.broadcasted_iota(jnp.int32, (sub, QB), 0)
    for r in range(ns):
        other = score[r:r + 1, :]
        for v, mine in enumerate(rows):
            if v < r // sub:
                inc = jnp.where(other > mine, 1.0, 0.0)
            elif v > r // sub:
                inc = jnp.where(other >= mine, 1.0, 0.0)
            else:
                tie = jnp.where(sub_idx > r % sub, 1.0, 0.0)
                inc = jnp.where(other > mine, 1.0, jnp.where(other == mine, tie, 0.0))
            ranks[v] = ranks[v] + inc
    chosen = [jnp.where(rk < n_sel, 1.0, 0.0) for rk in ranks]
    chosen.append(jnp.zeros((LANES - ns, QB), F32))
    sel_ref[0, 0, r0:r0 + QB, :] = jnp.concatenate(chosen, axis=0).T.astype(sel_ref.dtype)


def _cmp_attention(proj, kc, vc_t, ztab, mmat_t, b, t):
    tiles = CMP_TILES
    nb = t // (QB * tiles)
    hg = NSA_GROUP
    qw = hg * HEAD_DIM
    ncp = t // NSA_CMP_STRIDE
    ns = t // NSA_SEL_LEN
    n_sel = min(NSA_SEL_TOPK, ns)
    return pl.pallas_call(
        functools.partial(_cmp_kernel, hg=hg, ncp=ncp, ns=ns, n_sel=n_sel, tiles=tiles),
        grid=(b, NSA_KV_HEADS, nb),
        in_specs=[
            pl.BlockSpec((QB * tiles, qw), lambda bi, g, i: (bi * nb + i, C_NQ // qw + g)),
            pl.BlockSpec((1, 1, ncp, LANES), lambda bi, g, i: (bi, g, 0, 0)),
            pl.BlockSpec((1, 1, HEAD_DIM, ncp), lambda bi, g, i: (bi, g, 0, 0)),
            pl.BlockSpec((hg, 2 * ncp, QB), lambda bi, g, i: (g, 0, 0)),
            pl.BlockSpec((ns, ncp), lambda bi, g, i: (0, 0)),
        ],
        out_specs=[pl.BlockSpec((QB * tiles, qw), lambda bi, g, i: (bi * nb + i, g)),
                   pl.BlockSpec((1, 1, QB * tiles, LANES), lambda bi, g, i: (bi, g, i, 0))],
        out_shape=[jax.ShapeDtypeStruct((b * t, NSA_W), BF16),
                   jax.ShapeDtypeStruct((b, NSA_KV_HEADS, t, LANES), BF16)],
        compiler_params=_cparams(("parallel", "parallel", "arbitrary")),
        name="nsa_cmp_select",
    )(proj, kc, vc_t, ztab, mmat_t)


def _combine_kernel(oc_ref, os_ref, ow_ref, g_ref, e_ref, o_ref):
    sg = jax.nn.sigmoid(g_ref[...]).astype(BF16)
    acc = None
    for br, ref in enumerate((oc_ref, os_ref, ow_ref)):
        term = _dot(sg, e_ref[br]) * ref[...].astype(F32)
        acc = term if acc is None else acc + term
    o_ref[...] = acc.astype(o_ref.dtype)


def _nsa_combine(o_cmp, o_slc, o_win, gates, tm=512):
    n = o_cmp.shape[0]
    spread = np.zeros((NSA_N_BRANCH, D_GATE, NSA_W), np.float32)
    for h in range(NSA_HEADS):
        for br in range(NSA_N_BRANCH):
            spread[br, G_NG + NSA_N_BRANCH * h + br, h * HEAD_DIM:(h + 1) * HEAD_DIM] = 1.0
    spec = pl.BlockSpec((tm, NSA_W), lambda i: (i, 0))
    return pl.pallas_call(
        _combine_kernel,
        grid=(n // tm,),
        in_specs=[spec, spec, spec, pl.BlockSpec((tm, D_GATE), lambda i: (i, 0)),
                  pl.BlockSpec((NSA_N_BRANCH, D_GATE, NSA_W), lambda i: (0, 0, 0))],
        out_specs=spec,
        out_shape=jax.ShapeDtypeStruct((n, NSA_W), BF16),
        compiler_params=_cparams(("parallel",)),
        name="nsa_combine",
    )(o_cmp, o_slc, o_win, gates, jnp.asarray(spread, BF16))


def _ffn_kernel(te_ref, nu_ref, rows_ref, *refs, fused_norm, chained, sub):
    if fused_norm:
        x_ref, g_ref, wg_ref, wu_ref, wd_ref, o_ref, acc_ref, xn_ref = refs
    elif chained:
        x_ref, wg_ref, wu_ref, wd_ref, _, o_ref, acc_ref = refs
    else:
        x_ref, wg_ref, wu_ref, wd_ref, o_ref, acc_ref = refs
    del te_ref
    i = pl.program_id(0)
    k = pl.program_id(1)
    last = pl.num_programs(1) - 1
    used = i < nu_ref[0]
    tm = acc_ref.shape[0]
    src_ref = xn_ref if fused_norm else x_ref

    @pl.when(jnp.logical_and(used, k == 0))
    def _():
        if fused_norm:
            h = x_ref[...]
            y = h * lax.rsqrt(jnp.mean(h * h, axis=-1, keepdims=True) + RMS_EPS)
            xn_ref[...] = (y * g_ref[...]).astype(BF16)
            acc_ref[...] = h
        else:
            acc_ref[...] = jnp.zeros_like(acc_ref)

    def mlp(r0, nrows):
        x = src_ref[r0:r0 + nrows, :]
        gate = _dot(x, wg_ref[0].astype(BF16))
        up = _dot(x, wu_ref[0].astype(BF16))
        hid = (jax.nn.silu(gate) * up).astype(BF16)
        acc_ref[r0:r0 + nrows, :] += _dot(hid, wd_ref[0].astype(BF16))

    full = rows_ref[i] > tm - sub

    @pl.when(jnp.logical_and(used, full))
    def _():
        mlp(0, tm)

    for piece in range(tm // sub - 1):
        @pl.when(jnp.logical_and(jnp.logical_and(used, jnp.logical_not(full)), rows_ref[i] > piece * sub))
        def _():
            mlp(piece * sub, sub)

    @pl.when(jnp.logical_and(used, k == last))
    def _():
        o_ref[...] = acc_ref[...].astype(o_ref.dtype)

    @pl.when(jnp.logical_and(jnp.logical_not(used), k == last))
    def _():
        o_ref[...] = jnp.zeros_like(o_ref)


def _ffn(x, w_gate, w_up, w_down, tile_expert, n_used, tile_rows, out_dtype, gain=None, tm=512, tf=512,
         out_rows=None, out_tile0=0, out_buf=None, sub=256):
    r, d = x.shape
    nk = D_FF // tf
    n_tiles = r // tm
    fused_norm = gain is not None
    out_rows = r if out_rows is None else out_rows

    def tile(i, nu):
        return jnp.minimum(i, jnp.maximum(nu[0] - 1, 0))

    def kk(i, k, nu):
        return jnp.where(i < nu[0], k, nk - 1)

    in_specs = [pl.BlockSpec((tm, d), lambda i, k, te, nu, rw: (tile(i, nu), 0))]
    args = [x]
    scratch = [pltpu.VMEM((tm, d), F32)]
    if fused_norm:
        in_specs.append(pl.BlockSpec((1, d), lambda i, k, te, nu, rw: (0, 0)))
        args.append(gain.reshape(1, d).astype(F32))
        scratch.append(pltpu.VMEM((tm, d), BF16))
    in_specs += [
        pl.BlockSpec((1, d, tf), lambda i, k, te, nu, rw: (te[tile(i, nu)], 0, kk(i, k, nu))),
        pl.BlockSpec((1, d, tf), lambda i, k, te, nu, rw: (te[tile(i, nu)], 0, kk(i, k, nu))),
        pl.BlockSpec((1, tf, d), lambda i, k, te, nu, rw: (te[tile(i, nu)], kk(i, k, nu), 0)),
    ]
    args += [w_gate, w_up, w_down]
    aliases = {}
    if out_buf is not None:
        in_specs.append(pl.BlockSpec(memory_space=pl.ANY))
        args.append(out_buf)
        aliases = {3 + len(args) - 1: 0}
    return pl.pallas_call(
        functools.partial(_ffn_kernel, fused_norm=fused_norm, chained=out_buf is not None, sub=sub),
        grid_spec=pltpu.PrefetchScalarGridSpec(
            num_scalar_prefetch=3,
            grid=(n_tiles, nk),
            in_specs=in_specs,
            out_specs=pl.BlockSpec((tm, d), lambda i, k, te, nu, rw: (i + out_tile0, 0)),
            scratch_shapes=scratch,
        ),
        out_shape=jax.ShapeDtypeStruct((out_rows, d), out_dtype),
        input_output_aliases=aliases,
        compiler_params=_cparams(("arbitrary", "arbitrary"), FFN_VMEM_LIMIT),
        name="swiglu_ffn",
    )(tile_expert, n_used, tile_rows, *args)


def _router_kernel(l_ref, o_ref):
    lane = lax.broadcasted_iota(jnp.int32, l_ref.shape, 1)
    lf = lane.astype(F32)
    lg = jnp.where(lane < N_EXPERTS, l_ref[...], REMOVED)
    v1 = jnp.max(lg, axis=1, keepdims=True)
    i1 = jnp.min(jnp.where(lg == v1, lf, float(LANES)), axis=1, keepdims=True)
    lg2 = jnp.where(lf == i1, REMOVED, lg)
    v2 = jnp.max(lg2, axis=1, keepdims=True)
    i2 = jnp.min(jnp.where(lg2 == v2, lf, float(LANES)), axis=1, keepdims=True)
    e2 = jnp.exp(v2 - v1)
    den = 1.0 + e2
    p1 = 1.0 / den
    p2 = e2 / den
    out = jnp.where(lane == N_EXPERTS, i1, 0.0)
    out = jnp.where(lane == N_EXPERTS + 1, i2, out)
    out = jnp.where(lane == N_EXPERTS + 2, p1, out)
    out = jnp.where(lane == N_EXPERTS + 3, p2, out)
    o_ref[...] = out


def _router_top2(logits, tm=512):
    n = logits.shape[0]
    spec = pl.BlockSpec((tm, LANES), lambda i: (i, 0))
    return pl.pallas_call(
        _router_kernel,
        grid=(n // tm,),
        in_specs=[spec],
        out_specs=spec,
        out_shape=jax.ShapeDtypeStruct((n, LANES), F32),
        compiler_params=_cparams(("parallel",)),
        name="moe_router_top2",
    )(logits)


def _moe_combine_kernel(h_ref, a_ref, b_ref, top_ref, g_ref, o_ref, *, final_norm):
    p0 = top_ref[:, N_EXPERTS + TOP_K:N_EXPERTS + TOP_K + 1]
    p1 = top_ref[:, N_EXPERTS + TOP_K + 1:N_EXPERTS + TOP_K + 2]
    y = h_ref[...] + p0 * a_ref[...].astype(F32) + p1 * b_ref[...].astype(F32)
    if final_norm:
        y = y * lax.rsqrt(jnp.mean(y * y, axis=-1, keepdims=True) + RMS_EPS) * g_ref[...]
    o_ref[...] = y


def _moe_combine(h, ya, yb, top, gain, final_norm, tm=512):
    n, d = h.shape
    row = pl.BlockSpec((tm, d), lambda i: (i, 0))
    return pl.pallas_call(
        functools.partial(_moe_combine_kernel, final_norm=final_norm),
        grid=(n // tm,),
        in_specs=[row, row, row, pl.BlockSpec((tm, LANES), lambda i: (i, 0)),
                  pl.BlockSpec((1, d), lambda i: (0, 0))],
        out_specs=row,
        out_shape=jax.ShapeDtypeStruct((n, d), F32),
        compiler_params=_cparams(("parallel",)),
        name="moe_combine",
    )(h, ya, yb, top, gain.reshape(1, d).astype(F32))


def _t5_bucket_np(dist):
    n = np.maximum(dist, 0)
    max_exact = REL_BUCKETS // 2
    nf = np.maximum(n, 1).astype(np.float32)
    large = max_exact + (np.log(nf / np.float32(max_exact)) / np.float32(math.log(REL_MAX_DIST / max_exact))
                         * np.float32(REL_BUCKETS - max_exact)).astype(np.int32)
    large = np.minimum(large, REL_BUCKETS - 1)
    return np.where(n < max_exact, n, large).astype(np.int32)


def _lookup(table, idx):
    onehot = (idx.reshape(-1, 1) == np.arange(table.shape[0])[None, :]).astype(np.float32)
    out = jnp.dot(jnp.asarray(onehot), table, precision=lax.Precision.HIGHEST)
    return out.reshape(idx.shape + (table.shape[1],))


def _band_tables(rel_tab, t):
    q = np.arange(QB)[:, None]
    k = np.arange(QB)[None, :]
    rel = rel_tab.astype(F32) * LOG2E
    toep = _lookup(rel, _t5_bucket_np((q - k) % QB))
    toep = toep.transpose(2, 0, 1)
    before = jnp.asarray(k > q)
    far = rel[REL_BUCKETS - 1, SWA_HEADS:]

    def tiles(tab, fill_prev, extra=()):
        heads = tab.shape[0]
        parts = [jnp.where(before, tab, fill_prev), jnp.where(before, NEG_INF, tab)]
        parts += [jnp.broadcast_to(e, tab.shape) for e in extra]
        return jnp.stack([p.reshape(2, heads // 2 * QB, QB) for p in parts], axis=1)

    tabs_swa = tiles(toep[:SWA_HEADS], NEG_INF)
    tabs_nsa = tiles(toep[SWA_HEADS:] - far[:, None, None], 0.0, extra=[jnp.where(before, 0.0, NEG_INF)])
    ncp = t // NSA_CMP_STRIDE
    m = np.arange(-9, 7)[:, None]
    qr = np.arange(QB)[None, :]
    d = qr - NSA_CMP_STRIDE * m - (NSA_CMP_LEN - 1)
    vals = _lookup(rel[:, SWA_HEADS:], _t5_bucket_np(np.clip(d, 0, None)))
    vals = vals.transpose(2, 0, 1) - far[:, None, None]
    band = jnp.where(jnp.asarray(d >= 0), jnp.where(jnp.asarray(d < REL_MAX_DIST), vals, 0.0), NEG_INF)
    ztab = jnp.concatenate([jnp.zeros((NSA_HEADS, ncp - 9, QB), F32), band,
                            jnp.full((NSA_HEADS, ncp - 7, QB), NEG_INF, F32)], axis=1)
    return tabs_swa, tabs_nsa, ztab


def _selection_matrices(t):
    ncp = t // NSA_CMP_STRIDE
    ns = t // NSA_SEL_LEN
    per = NSA_SEL_LEN // NSA_CMP_STRIDE
    ratio = NSA_CMP_LEN // NSA_CMP_STRIDE
    mmat_t = np.zeros((ns, ncp), np.float32)
    for n in range(ncp - 1):
        for j in range(ratio):
            mmat_t[(n + j) // per, n] += 1.0
    emat_t = (np.arange(t)[:, None] // NSA_SEL_LEN == np.arange(LANES)[None, :]).astype(np.float32)
    return jnp.asarray(mmat_t, BF16), (jnp.asarray(emat_t.T, BF16), jnp.asarray(emat_t, BF16))


def _dup(w):
    d = w.shape[0]
    w = w.reshape(d, -1, 1, HEAD_DIM)
    return jnp.broadcast_to(w, (d, w.shape[1], 2, HEAD_DIM)).reshape(d, -1)


def _prep_in_weights(w_in_l):
    sizes = (FOX_W, FOX_W, FOX_W, FOX_HEADS, SWA_W, SWA_KV_W, SWA_KV_W,
             NSA_W, NSA_KV_W, NSA_KV_W, NSA_KV_W, NSA_KV_W, NSA_KV_W, NSA_KV_W, NSA_HEADS * NSA_N_BRANCH)
    splits = [int(s) for s in np.cumsum(sizes)[:-1]]
    (fq, fk, fv, ff, sq, sk, sv, nq, nkc, nvc, nks, nvs, nkw, nvw, ng) = jnp.split(w_in_l, splits, axis=-1)
    scale = HEAD_DIM ** -0.5 * LOG2E
    cols = [fq * scale, fk, fv, sq * scale, nq * scale, _dup(sk), _dup(nks), _dup(nkw), sv, nvs, nvw, nkc, nvc]
    cols.append(jnp.zeros((w_in_l.shape[0], D_PROJ - sum(c.shape[1] for c in cols)), w_in_l.dtype))
    w_proj = jnp.concatenate(cols, axis=-1).astype(BF16)
    pad = jnp.zeros((w_in_l.shape[0], D_GATE - FOX_HEADS - NSA_HEADS * NSA_N_BRANCH), w_in_l.dtype)
    w_gate = jnp.concatenate([ff, ng, pad], axis=-1).astype(BF16)
    return w_proj, w_gate


def _compress_inputs(proj, b, t):
    g = NSA_KV_HEADS
    nck = t // NSA_CMP_STRIDE
    x = proj[:, C_NKC:C_NKC + 2 * NSA_KV_W].reshape(b, nck, NSA_CMP_STRIDE, 2, g, HEAD_DIM)
    return x.transpose(3, 0, 4, 1, 2, 5).reshape(2, b * g, nck, NSA_CMP_STRIDE * HEAD_DIM)


def _mixer(h, gain, b, t, w_in_l, forget_bias, sinks, cmp_pos, cmp_w1, cmp_b1, cmp_w2, cmp_b2, tables):
    tabs_swa, tabs_nsa, ztab, mmat_t, emats = tables
    w_proj, w_gate = _prep_in_weights(w_in_l)
    proj, gates = _in_proj(h, gain, w_proj, w_gate)
    fv_t = proj[:, C_FV:C_FV + FOX_W].reshape(b, t, FOX_W).transpose(0, 2, 1)
    v_t = proj[:, C_V:C_V + 3 * NSA_KV_W].reshape(b, t, 3 * NSA_KV_W).transpose(0, 2, 1)

    qx, kx = _fox_decay_operands(gates, forget_bias, b, t)
    o_fox = _fox_attention(proj, fv_t, qx, kx, b, t)

    o_swa = _band_attention(proj, v_t, tabs_swa, b, t, mode="swa", hg=SWA_GROUP,
                            c_q=C_SQ, c_k=C_SK, v_blk=0, sinks=sinks)

    flat = _compress_inputs(proj, b, t)
    pos = cmp_pos.reshape(2, 1, NSA_CMP_LEN * HEAD_DIM).astype(F32)
    w2d = jnp.concatenate([cmp_w2, cmp_w2], axis=-1).astype(BF16)
    b2d = jnp.concatenate([cmp_b2, cmp_b2], axis=-1).reshape(2, 1, LANES).astype(F32)
    kvc = _compress(flat, pos, cmp_w1.astype(BF16), cmp_b1.reshape(2, 1, NSA_CMP_HIDDEN).astype(F32), w2d, b2d)
    ncp = t // NSA_CMP_STRIDE
    kvc = kvc.reshape(2, b, NSA_KV_HEADS, ncp, LANES)
    vc_t = kvc[1, :, :, :, :HEAD_DIM].transpose(0, 1, 3, 2)
    o_cmp, sel = _cmp_attention(proj, kvc[0], vc_t, ztab, mmat_t, b, t)
    o_slc = _band_attention(proj, v_t, tabs_nsa, b, t, mode="sel", hg=NSA_GROUP,
                            c_q=C_NQ, c_k=C_NKS, v_blk=2, sel=sel, emats=emats)
    o_win = _band_attention(proj, v_t, tabs_nsa, b, t, mode="win", hg=NSA_GROUP,
                            c_q=C_NQ, c_k=C_NKW, v_blk=4)
    o_nsa = _nsa_combine(o_cmp, o_slc, o_win, gates)
    return o_fox, o_swa, o_nsa


def _moe(hn, h, router, w_gate, w_up, w_down, gain, final_norm, tm=1024, tf=512):
    n, d = hn.shape
    w_r = jnp.zeros((d, LANES), BF16).at[:, :N_EXPERTS].set(router.astype(BF16))
    logits = _matmul([hn], [w_r], F32, tn=LANES, name="router_logits")
    top = _router_top2(logits)
    e_idx = top[:, N_EXPERTS:N_EXPERTS + TOP_K].astype(jnp.int32)
    e_flat = e_idx.reshape(-1)
    onehot = (e_flat[:, None] == jnp.arange(N_EXPERTS)[None, :]).astype(jnp.int32)
    csum = jnp.cumsum(onehot, axis=0)
    counts = csum[-1]
    rank = jnp.take_along_axis(csum, e_flat[:, None], axis=1)[:, 0] - 1
    padded = ((counts + tm - 1) // tm) * tm
    ends = jnp.cumsum(padded)
    starts = ends - padded
    dest = starts[e_flat] + rank
    r_pad = n * TOP_K + N_EXPERTS * tm
    src_tok = jnp.zeros((r_pad,), jnp.int32).at[dest].set(jnp.arange(n * TOP_K, dtype=jnp.int32) // TOP_K)
    tile_start = jnp.arange(r_pad // tm, dtype=jnp.int32) * tm
    tile_expert = jnp.minimum(jnp.sum(tile_start[:, None] >= ends[None, :], axis=1), N_EXPERTS - 1).astype(jnp.int32)
    n_used = (ends[-1:] // tm).astype(jnp.int32)
    tile_rows = jnp.clip((starts + counts)[tile_expert] - tile_start, 0, tm).astype(jnp.int32)
    n_tiles = r_pad // tm
    per = n_tiles // MOE_CHUNKS
    y = jnp.zeros((r_pad, d), BF16)
    for c in range(MOE_CHUNKS):
        xs = hn.at[src_tok[c * per * tm:(c + 1) * per * tm]].get(mode="promise_in_bounds")
        used_c = jnp.clip(n_used - c * per, 0, per)
        y = _ffn(xs, w_gate, w_up, w_down, tile_expert[c * per:(c + 1) * per], used_c,
                 tile_rows[c * per:(c + 1) * per], BF16, tm=tm, tf=tf,
                 out_rows=r_pad, out_tile0=c * per, out_buf=y)
    dest = dest.reshape(n, TOP_K)
    ya = y.at[dest[:, 0]].get(mode="promise_in_bounds")
    yb = y.at[dest[:, 1]].get(mode="promise_in_bounds")
    return _moe_combine(h, ya, yb, top, gain, final_norm)


def kernel(x, attn_norm, w_in, fox_forget_bias, swa_sinks, nsa_cmp_pos, nsa_cmp_w1, nsa_cmp_b1, nsa_cmp_w2,
           nsa_cmp_b2, w_out, rel_bias_table, ffn_norm, dense_w_gate, dense_w_up, dense_w_down, moe_router,
           moe_w_gate, moe_w_up, moe_w_down, final_norm):
    b, t, d = x.shape
    n = b * t
    depth = w_in.shape[0]
    tables = _band_tables(rel_bias_table, t) + _selection_matrices(t)
    h = x.reshape(n, d)
    for layer in range(depth):
        o_fox, o_swa, o_nsa = _mixer(h, attn_norm[layer], b, t, w_in[layer], fox_forget_bias[layer],
                                     swa_sinks[layer], nsa_cmp_pos[layer], nsa_cmp_w1[layer], nsa_cmp_b1[layer],
                                     nsa_cmp_w2[layer], nsa_cmp_b2[layer], tables)
        wo = w_out[layer]
        h = _matmul([o_fox, o_swa, o_nsa], [wo[:FOX_W], wo[FOX_W:FOX_W + SWA_W], wo[FOX_W + SWA_W:]],
                    F32, residual=h, tn=1024, name="out_proj")
        i = layer // 2
        last = layer == depth - 1
        if layer % 2 == 0:
            tm = 512
            zeros = jnp.zeros((n // tm,), jnp.int32)
            h = _ffn(h, dense_w_gate[i][None].astype(BF16), dense_w_up[i][None].astype(BF16),
                     dense_w_down[i][None].astype(BF16), zeros, jnp.full((1,), n // tm, jnp.int32),
                     jnp.full((n // tm,), tm, jnp.int32), F32, gain=ffn_norm[layer], tm=tm)
            if last:
                h = _rmsnorm(h, final_norm, F32)
        else:
            hn = _rmsnorm(h, ffn_norm[layer], BF16)
            h = _moe(hn, h, moe_router[i], moe_w_gate[i], moe_w_up[i], moe_w_down[i], final_norm, last)
    return h.reshape(b, t, d)
```

```python
import functools
import math

import numpy as np
import jax
import jax.numpy as jnp
from jax import lax
from jax.experimental import pallas as pl
from jax.experimental.pallas import tpu as pltpu

F32 = jnp.float32
BF16 = jnp.bfloat16

D_MODEL = 2048
HEAD_DIM = 64
FOX_HEADS = 8
FOX_W = FOX_HEADS * HEAD_DIM
SWA_HEADS = 8
SWA_KV_HEADS = 2
SWA_GROUP = SWA_HEADS // SWA_KV_HEADS
SWA_W = SWA_HEADS * HEAD_DIM
SWA_KV_W = SWA_KV_HEADS * HEAD_DIM
SWA_WINDOW = 128
NSA_HEADS = 16
NSA_KV_HEADS = 2
NSA_GROUP = NSA_HEADS // NSA_KV_HEADS
NSA_W = NSA_HEADS * HEAD_DIM
NSA_KV_W = NSA_KV_HEADS * HEAD_DIM
NSA_CMP_LEN = 32
NSA_CMP_STRIDE = 16
NSA_CMP_HIDDEN = 256
NSA_SEL_LEN = 64
NSA_SEL_TOPK = 16
NSA_WINDOW = 512
NSA_N_BRANCH = 3
FORCE_SCORE = 1e9
REL_BUCKETS = 32
REL_MAX_DIST = 128
D_FF = 5632
N_EXPERTS = 8
TOP_K = 2
RMS_EPS = 1e-6
NEG_INF = -1e30
M_INIT = -1e29
REMOVED = -3e38
LOG2E = 1.4426950408889634
ONES_ROWS = 16
BAND_TILES = {"swa": 8, "win": 8, "sel": 4}
CMP_TILES = 8
MOE_CHUNKS = 8

LANES = 128
QB = 128
VMEM_LIMIT = 56 * 1024 * 1024
FFN_VMEM_LIMIT = 60 * 1024 * 1024

C_FQ, C_FK, C_FV = 0, 512, 1024
C_SQ = 1536
C_NQ = 2048
C_SK, C_NKS, C_NKW = 3072, 3328, 3584
C_V = 3840
C_NKC, C_NVC = 4224, 4352
D_PROJ = 4608
PROJ_TN = 1536
G_NG = FOX_HEADS
D_GATE = 128


def _cparams(sem, vmem=VMEM_LIMIT):
    return pltpu.CompilerParams(dimension_semantics=sem, vmem_limit_bytes=vmem)


def _dot(a, b):
    return jnp.dot(a, b, preferred_element_type=F32)


def _dot_nt(a, b):
    return lax.dot_general(a, b, (((1,), (1,)), ((), ())), preferred_element_type=F32)


def _split3(x):
    hi = x.astype(BF16)
    r1 = x - hi.astype(F32)
    mid = r1.astype(BF16)
    lo = (r1 - mid.astype(F32)).astype(BF16)
    return hi, mid, lo


def _pipeline3(n, stage1, stage2, stage3):
    first, second, outs = {}, {}, []
    for step in range(n + 2):
        if step < n:
            first[step] = stage1(step)
        if 0 <= step - 1 < n:
            second[step - 1] = stage2(step - 1, first.pop(step - 1))
        if 0 <= step - 2 < n:
            outs.append(stage3(step - 2, *second.pop(step - 2)))
    return outs


def _rmsnorm_kernel(x_ref, g_ref, o_ref):
    x = x_ref[...]
    y = x * lax.rsqrt(jnp.mean(x * x, axis=-1, keepdims=True) + RMS_EPS)
    o_ref[...] = (y * g_ref[...]).astype(o_ref.dtype)


def _rmsnorm(h, g, out_dtype, tm=512):
    n, d = h.shape
    return pl.pallas_call(
        _rmsnorm_kernel,
        grid=(n // tm,),
        in_specs=[pl.BlockSpec((tm, d), lambda i: (i, 0)),
                  pl.BlockSpec((1, d), lambda i: (0, 0))],
        out_specs=pl.BlockSpec((tm, d), lambda i: (i, 0)),
        out_shape=jax.ShapeDtypeStruct((n, d), out_dtype),
        compiler_params=_cparams(("parallel",)),
        name="rmsnorm",
    )(h, g.reshape(1, d).astype(F32))


def _mm_kernel(*refs, n_in, has_res):
    o_ref = refs[-1]
    acc = None
    for a in range(n_in):
        d = _dot(refs[a][...], refs[n_in + a][...].astype(BF16))
        acc = d if acc is None else acc + d
    if has_res:
        acc = acc + refs[2 * n_in][...]
    o_ref[...] = acc.astype(o_ref.dtype)


def _matmul(xs, ws, out_dtype, residual=None, tm=1024, tn=256, name="matmul"):
    n = xs[0].shape[0]
    m = ws[0].shape[1]
    tm = min(tm, n)
    tn = min(tn, m)
    in_specs = [pl.BlockSpec((tm, x.shape[1]), lambda i, j: (i, 0)) for x in xs]
    in_specs += [pl.BlockSpec((w.shape[0], tn), lambda i, j: (0, j)) for w in ws]
    args = list(xs) + list(ws)
    if residual is not None:
        in_specs.append(pl.BlockSpec((tm, tn), lambda i, j: (i, j)))
        args.append(residual)
    return pl.pallas_call(
        functools.partial(_mm_kernel, n_in=len(xs), has_res=residual is not None),
        grid=(n // tm, m // tn),
        in_specs=in_specs,
        out_specs=pl.BlockSpec((tm, tn), lambda i, j: (i, j)),
        out_shape=jax.ShapeDtypeStruct((n, m), out_dtype),
        compiler_params=_cparams(("parallel", "arbitrary")),
        name=name,
    )(*args)


def _in_proj_kernel(h_ref, g_ref, w_ref, wgate_ref, proj_ref, gates_ref, xn_ref):
    @pl.when(pl.program_id(1) == 0)
    def _():
        h = h_ref[...]
        y = h * lax.rsqrt(jnp.mean(h * h, axis=-1, keepdims=True) + RMS_EPS)
        xn_ref[...] = (y * g_ref[...]).astype(BF16)
        gates_ref[...] = _dot(xn_ref[...], wgate_ref[...])

    proj_ref[...] = _dot(xn_ref[...], w_ref[...]).astype(proj_ref.dtype)


def _in_proj(h, gain, w_proj, w_gate, tm=1024, tn=PROJ_TN):
    n, d = h.shape
    return pl.pallas_call(
        _in_proj_kernel,
        grid=(n // tm, D_PROJ // tn),
        in_specs=[pl.BlockSpec((tm, d), lambda i, j: (i, 0)),
                  pl.BlockSpec((1, d), lambda i, j: (0, 0)),
                  pl.BlockSpec((d, tn), lambda i, j: (0, j)),
                  pl.BlockSpec((d, D_GATE), lambda i, j: (0, 0))],
        out_specs=[pl.BlockSpec((tm, tn), lambda i, j: (i, j)),
                   pl.BlockSpec((tm, D_GATE), lambda i, j: (i, 0))],
        out_shape=[jax.ShapeDtypeStruct((n, D_PROJ), BF16), jax.ShapeDtypeStruct((n, D_GATE), F32)],
        scratch_shapes=[pltpu.VMEM((tm, d), BF16)],
        compiler_params=_cparams(("parallel", "arbitrary")),
        name="in_proj",
    )(h, gain.reshape(1, d).astype(F32), w_proj, w_gate)


def _cumsum_kernel(g_ref, b_ref, pq_ref, pk_ref, oq_ref, ok_ref, qx_ref, kx_ref, carry_ref, *, tc):
    @pl.when(pl.program_id(1) == 0)
    def _():
        carry_ref[...] = jnp.zeros_like(carry_ref)

    z = g_ref[...] + b_ref[...]
    log_f = jnp.minimum(z, 0.0) - jnp.log1p(jnp.exp(-jnp.abs(z)))
    row = lax.broadcasted_iota(jnp.int32, (tc, tc), 0)
    col = lax.broadcasted_iota(jnp.int32, (tc, tc), 1)
    tri = jnp.where(col <= row, 1.0, 0.0).astype(BF16)
    hi, mid, lo = _split3(log_f)
    c = _dot(tri, hi) + _dot(tri, mid) + _dot(tri, lo) + carry_ref[...]
    carry_ref[...] = c[tc - 1:tc, :]
    parts = jnp.concatenate(_split3(c * LOG2E), axis=1)
    qx_ref[...] = (_dot(parts, pq_ref[...]) + oq_ref[...]).astype(qx_ref.dtype)
    kx_ref[...] = (_dot(parts, pk_ref[...]) + ok_ref[...]).astype(kx_ref.dtype)


def _fox_decay_operands(gates, forget_bias, b, t, tc=512):
    nt = t // tc
    wide = FOX_HEADS * LANES
    bias = jnp.zeros((1, D_GATE), F32).at[0, :FOX_HEADS].set(forget_bias.astype(F32))
    pq = np.zeros((3 * D_GATE, wide), np.float32)
    pk = np.zeros((3 * D_GATE, wide), np.float32)
    oq = np.zeros((1, wide), np.float32)
    ok = np.zeros((1, wide), np.float32)
    for h in range(FOX_HEADS):
        for part in range(3):
            pq[part * D_GATE + h, h * LANES + part] = 1.0
            pk[part * D_GATE + h, h * LANES + 3 + part] = -1.0
            oq[0, h * LANES + 3 + part] = 1.0
            ok[0, h * LANES + part] = 1.0
    const = lambda shape: pl.BlockSpec(shape, lambda bi, ti: (0, 0))
    out_spec = pl.BlockSpec((tc, wide), lambda bi, ti: (bi * nt + ti, 0))
    return pl.pallas_call(
        functools.partial(_cumsum_kernel, tc=tc),
        grid=(b, nt),
        in_specs=[pl.BlockSpec((tc, D_GATE), lambda bi, ti: (bi * nt + ti, 0)),
                  const((1, D_GATE)), const((3 * D_GATE, wide)), const((3 * D_GATE, wide)),
                  const((1, wide)), const((1, wide))],
        out_specs=[out_spec, out_spec],
        out_shape=[jax.ShapeDtypeStruct((b * t, wide), BF16), jax.ShapeDtypeStruct((b * t, wide), BF16)],
        scratch_shapes=[pltpu.VMEM((1, D_GATE), F32)],
        compiler_params=_cparams(("parallel", "arbitrary")),
        name="fox_cumsum",
    )(gates, bias, jnp.asarray(pq, BF16), jnp.asarray(pk, BF16), jnp.asarray(oq), jnp.asarray(ok))


def _fox_kernel(qi_ref, kj_ref, q_ref, k_ref, vt_ref, qx_ref, kx_ref, o_ref, m_sc, acc_sc, *, tq):
    i = qi_ref[pl.program_id(1)]
    j = kj_ref[pl.program_id(1)]
    lane = lax.broadcasted_iota(jnp.int32, (1, LANES), 1)
    upper = lane >= HEAD_DIM

    @pl.when(j == 0)
    def _():
        m_sc[...] = jnp.full_like(m_sc, M_INIT)
        acc_sc[...] = jnp.zeros_like(acc_sc)

    def step(diagonal):
        if diagonal:
            key = lax.broadcasted_iota(jnp.int32, (tq, tq), 0)
            qry = lax.broadcasted_iota(jnp.int32, (tq, tq), 1)
            causal = key <= qry
        ones = jnp.ones((ONES_ROWS, tq), BF16)

        def qk(h):
            p, a = divmod(h, 2)
            q2 = q_ref[:, p * LANES:(p + 1) * LANES]
            k2 = k_ref[:, p * LANES:(p + 1) * LANES]
            zero = jnp.zeros_like(q2)
            qa = jnp.where(upper, q2, zero) if a else jnp.where(upper, zero, q2)
            q_aug = jnp.concatenate([qa, qx_ref[:, h * LANES:(h + 1) * LANES]], axis=1)
            k_aug = jnp.concatenate([k2, kx_ref[:, h * LANES:(h + 1) * LANES]], axis=1)
            s = _dot_nt(k_aug, q_aug)
            if diagonal:
                s = jnp.where(causal, s, NEG_INF)
            return s, jnp.max(s, axis=0, keepdims=True)

        def softmax(h, scored):
            s, s_max = scored
            m_prev = m_sc[h]
            m_new = jnp.maximum(m_prev, s_max)
            m_sc[h] = m_new
            return jnp.exp2(m_prev - m_new), jnp.exp2(s - m_new).astype(BF16)

        def pv(h, alpha, pr):
            vt = jnp.concatenate([vt_ref[0, h * HEAD_DIM:(h + 1) * HEAD_DIM, :], ones], axis=0)
            acc_sc[h] = alpha * acc_sc[h] + _dot(vt, pr)

        _pipeline3(FOX_HEADS, qk, softmax, pv)

    @pl.when(j < i)
    def _():
        step(False)

    @pl.when(j == i)
    def _():
        step(True)
        for p in range(FOX_HEADS // 2):
            outs = []
            for h in (2 * p, 2 * p + 1):
                acc = acc_sc[h]
                outs.append(acc[:HEAD_DIM] * (1.0 / acc[HEAD_DIM:HEAD_DIM + 1]))
            o_ref[:, p * LANES:(p + 1) * LANES] = jnp.concatenate(outs, axis=0).T.astype(o_ref.dtype)


def _fox_attention(proj, v_t, qx, kx, b, t, tq=512):
    nt = t // tq
    wide = FOX_HEADS * LANES
    pairs = [(i, j) for i in range(nt) for j in range(i + 1)]
    qi = jnp.asarray([p[0] for p in pairs], jnp.int32)
    kj = jnp.asarray([p[1] for p in pairs], jnp.int32)
    return pl.pallas_call(
        functools.partial(_fox_kernel, tq=tq),
        grid_spec=pltpu.PrefetchScalarGridSpec(
            num_scalar_prefetch=2,
            grid=(b, len(pairs)),
            in_specs=[
                pl.BlockSpec((tq, FOX_W), lambda bi, s, qi, kj: (bi * nt + qi[s], C_FQ // FOX_W)),
                pl.BlockSpec((tq, FOX_W), lambda bi, s, qi, kj: (bi * nt + kj[s], C_FK // FOX_W)),
                pl.BlockSpec((1, FOX_W, tq), lambda bi, s, qi, kj: (bi, 0, kj[s])),
                pl.BlockSpec((tq, wide), lambda bi, s, qi, kj: (bi * nt + qi[s], 0)),
                pl.BlockSpec((tq, wide), lambda bi, s, qi, kj: (bi * nt + kj[s], 0)),
            ],
            out_specs=pl.BlockSpec((tq, FOX_W), lambda bi, s, qi, kj: (bi * nt + qi[s], 0)),
            scratch_shapes=[pltpu.VMEM((FOX_HEADS, 1, tq), F32),
                            pltpu.VMEM((FOX_HEADS, HEAD_DIM + ONES_ROWS, tq), F32)],
        ),
        out_shape=jax.ShapeDtypeStruct((b * t, FOX_W), BF16),
        compiler_params=_cparams(("parallel", "arbitrary")),
        name="fox_attention",
    )(qi, kj, proj, proj, v_t, qx, kx)


def _stack_heads(q_ref, hg, r0=0):
    lane = lax.broadcasted_iota(jnp.int32, (1, LANES), 1)
    upper = lane >= HEAD_DIM
    qs = []
    for p in range(hg // 2):
        q2 = q_ref[r0:r0 + QB, p * LANES:(p + 1) * LANES]
        qs.append(jnp.where(upper, jnp.zeros_like(q2), q2))
        qs.append(jnp.where(upper, q2, jnp.zeros_like(q2)))
    return jnp.concatenate(qs, axis=0)


def _store_heads(o_ref, o_t, hg, r0=0):
    for p in range(hg // 2):
        pair = jnp.concatenate([o_t[:, (2 * p) * QB:(2 * p + 1) * QB],
                                o_t[:, (2 * p + 1) * QB:(2 * p + 2) * QB]], axis=0)
        o_ref[r0:r0 + QB, p * LANES:(p + 1) * LANES] = pair.T.astype(o_ref.dtype)


def _band_kernel(*refs, mode, hg, chunk, tiles):
    for sub in range(tiles):
        _band_tile(*refs, mode=mode, hg=hg, chunk=chunk, i=pl.program_id(2) * tiles + sub, r0=sub * QB)


def _band_tile(*refs, mode, hg, chunk, i, r0):
    if mode == "swa":
        q_ref, k_ref, vt_ref, tab_ref, sink_ref, o_ref = refs
    elif mode == "sel":
        (q_ref, k_ref, vt_ref, tab_ref, sel_ref, e_ref, et_ref, o_ref,
         sa_ref, sb_ref, ma_ref, mb_ref, m_ref, acc_ref) = refs
    else:
        q_ref, k_ref, vt_ref, tab_ref, o_ref = refs
    g = pl.program_id(1)
    cols = hg * QB
    qstack = _stack_heads(q_ref, hg, r0)
    eye = jnp.where(lax.broadcasted_iota(jnp.int32, (QB, QB), 0) == lax.broadcasted_iota(jnp.int32, (QB, QB), 1),
                    1.0, 0.0).astype(BF16)

    def k_block(blk):
        return k_ref[pl.ds(pl.multiple_of(blk * QB, QB), QB), :]

    def vt_rows(start, size):
        return jnp.concatenate([vt_ref[0, :, pl.ds(start, size)], jnp.ones((ONES_ROWS, size), BF16)], axis=0)

    def vt_block(blk):
        return vt_rows(pl.multiple_of(blk * QB, QB), QB)

    def with_table(table):
        return jnp.concatenate([qstack, table.astype(BF16)], axis=1)

    def grp(x, j):
        return x[j * GW:(j + 1) * GW]

    def lanes(x, j):
        return x[:, j * GW:(j + 1) * GW]

    GW = cols if mode == "sel" else 2 * QB
    n_grp = cols // GW
    ip = jnp.maximum(i - 1, 0)
    t_prev = tab_ref[0, 0]
    if mode == "sel":
        sel_q = sel_ref[0, 0, r0:r0 + QB, :]
        allowed = _dot(sel_q, e_ref[:, pl.ds(pl.multiple_of(ip * QB, QB), QB)])
        t_prev = jnp.where(jnp.concatenate([allowed] * hg, axis=0) > 0.5, t_prev, NEG_INF)
    t_prev = jnp.where(i > 0, t_prev, NEG_INF)
    q_prev = with_table(t_prev)
    q_cur = with_table(tab_ref[0, 1])
    k_prev = jnp.concatenate([k_block(ip), eye], axis=1)
    k_cur = jnp.concatenate([k_block(i), eye], axis=1)
    vt_near = jnp.concatenate([vt_block(ip), vt_block(i)], axis=1)

    if mode == "swa":
        head = lax.broadcasted_iota(jnp.int32, (1, cols), 1) // QB
        sink = jnp.zeros((1, cols), F32)
        for h in range(hg):
            sink = jnp.where(head == h, sink_ref[g * hg + h], sink)

    elif mode == "win":
        n_far = NSA_WINDOW // QB - 1
        backs = range(n_far + 1, 1, -1)
        onehot = jnp.concatenate([eye, jnp.zeros(((n_far - 1) * QB, QB), BF16)], axis=0)
        k_far = jnp.concatenate(
            [jnp.concatenate([k_block(jnp.maximum(i - bk, 0)) for bk in backs], axis=0), onehot], axis=1)
        vt_far = jnp.concatenate([vt_block(jnp.maximum(i - bk, 0)) for bk in backs], axis=1)
        q_tri = with_table(tab_ref[0, 2])
        in_seq = lax.broadcasted_iota(jnp.int32, (n_far * QB, 1), 0) >= (n_far + 1 - i) * QB

    else:
        blk = lax.broadcasted_iota(jnp.int32, (QB, LANES), 1)
        pick = jnp.logical_and(sel_q.astype(F32) > 0.5, blk < 2 * (i - 1))
        sel_bias = jnp.where(pick, 0.0, NEG_INF)
        q_far = with_table(jnp.concatenate([sel_bias] * hg, axis=0))
        per = chunk // QB
        n_chunks = (jnp.maximum(i - 1, 0) + per - 1) // per
        def far_scores(c):
            off = pl.multiple_of(c * chunk, chunk)
            k_rows = jnp.concatenate([k_ref[pl.ds(off, chunk), :], et_ref[pl.ds(off, chunk), :]], axis=1)
            return _dot_nt(k_rows, q_far)

        def put_scores(c, s_ref, smax_ref):
            s = far_scores(c)
            s_ref[...] = s
            smax_ref[...] = jnp.max(s, axis=0, keepdims=True)

        def sweep(c, cur, nxt):
            put_scores(jnp.minimum(c + 1, n_chunks - 1), *nxt)
            s_ref, smax_ref = cur
            m_old = m_ref[...]
            m_new = jnp.maximum(m_old, smax_ref[...])
            pf = jnp.exp2(s_ref[...] - m_new).astype(BF16)
            acc_ref[...] = (jnp.exp2(m_old - m_new) * acc_ref[...]
                            + _dot(vt_rows(pl.multiple_of(c * chunk, chunk), chunk), pf))
            m_ref[...] = m_new

        buf_a, buf_b = (sa_ref, ma_ref), (sb_ref, mb_ref)

        def body(quad, carry):
            for u in range(4):
                sweep(4 * quad + u, *((buf_a, buf_b) if u % 2 == 0 else (buf_b, buf_a)))
            return carry

        m_ref[...] = jnp.full((1, cols), M_INIT, F32)
        acc_ref[...] = jnp.zeros((HEAD_DIM + ONES_ROWS, cols), F32)
        put_scores(0, *buf_a)
        lax.fori_loop(0, n_chunks // 4, body, 0)
        done = (n_chunks // 4) * 4

        @pl.when(n_chunks % 4 >= 2)
        def _():
            sweep(done, buf_a, buf_b)
            sweep(done + 1, buf_b, buf_a)

        @pl.when(n_chunks % 2 == 1)
        def _():
            sweep(n_chunks - 1, buf_a, buf_b)

        m_far = m_ref[...]
        acc_far = acc_ref[...]

    def near_scores(j):
        parts = [_dot_nt(k_prev, grp(q_prev, j)), _dot_nt(k_cur, grp(q_cur, j))]
        if mode == "win":
            parts.append(jnp.where(in_seq, _dot_nt(k_far, grp(q_tri, j)), NEG_INF))
        return parts

    def near_softmax(j, parts):
        m = jnp.max(parts[0], axis=0, keepdims=True)
        for s in parts[1:]:
            m = jnp.maximum(m, jnp.max(s, axis=0, keepdims=True))
        if mode == "swa":
            m = jnp.maximum(m, lanes(sink, j))
        if mode == "sel":
            m = jnp.maximum(m, lanes(m_far, j))
        return m, [jnp.exp2(s - m).astype(BF16) for s in parts]

    def near_output(j, m, probs):
        acc = _dot(vt_near, jnp.concatenate(probs[:2], axis=0))
        if mode == "win":
            acc = acc + _dot(vt_far, probs[2])
        if mode == "sel":
            acc = jnp.exp2(lanes(m_far, j) - m) * lanes(acc_far, j) + acc
        den = acc[HEAD_DIM:HEAD_DIM + 1]
        if mode == "swa":
            den = den + jnp.exp2(lanes(sink, j) - m)
        o_t = acc[:HEAD_DIM] * (1.0 / den)
        for pp in range(GW // (2 * QB)):
            pair = jnp.concatenate([o_t[:, 2 * pp * QB:(2 * pp + 1) * QB],
                                    o_t[:, (2 * pp + 1) * QB:(2 * pp + 2) * QB]], axis=0)
            c0 = (j * (GW // (2 * QB)) + pp) * LANES
            o_ref[r0:r0 + QB, c0:c0 + LANES] = pair.T.astype(o_ref.dtype)

    _pipeline3(n_grp, near_scores, near_softmax, near_output)


def _band_attention(proj, v_t, tabs, b, t, *, mode, hg, c_q, c_k, v_blk, sinks=None, sel=None, emats=None):
    tiles = BAND_TILES[mode]
    nb = t // (QB * tiles)
    n_groups = 2
    qw = hg * HEAD_DIM
    chunk = min(512, t)
    in_specs = [
        pl.BlockSpec((QB * tiles, qw), lambda bi, g, i: (bi * nb + i, c_q // qw + g)),
        pl.BlockSpec((t, LANES), lambda bi, g, i: (bi, c_k // LANES + g)),
        pl.BlockSpec((1, HEAD_DIM, t), lambda bi, g, i: (bi, v_blk + g, 0)),
        pl.BlockSpec((1,) + tabs.shape[1:], lambda bi, g, i: (g, 0, 0, 0)),
    ]
    args = [proj, proj, v_t, tabs]
    scratch = []
    if mode == "swa":
        in_specs.append(pl.BlockSpec(memory_space=pltpu.SMEM))
        args.append(sinks.astype(F32) * LOG2E)
    if mode == "sel":
        emat, emat_t = emats
        in_specs.append(pl.BlockSpec((1, 1, QB * tiles, LANES), lambda bi, g, i: (bi, g, i, 0)))
        in_specs.append(pl.BlockSpec((LANES, t), lambda bi, g, i: (0, 0)))
        in_specs.append(pl.BlockSpec((t, LANES), lambda bi, g, i: (0, 0)))
        args += [sel, emat, emat_t]
        scratch = [pltpu.VMEM((chunk, hg * QB), F32), pltpu.VMEM((chunk, hg * QB), F32),
                   pltpu.VMEM((1, hg * QB), F32), pltpu.VMEM((1, hg * QB), F32),
                   pltpu.VMEM((1, hg * QB), F32), pltpu.VMEM((HEAD_DIM + ONES_ROWS, hg * QB), F32)]
    return pl.pallas_call(
        functools.partial(_band_kernel, mode=mode, hg=hg, chunk=chunk, tiles=tiles),
        grid=(b, n_groups, nb),
        in_specs=in_specs,
        out_specs=pl.BlockSpec((QB * tiles, qw), lambda bi, g, i: (bi * nb + i, g)),
        out_shape=jax.ShapeDtypeStruct((b * t, n_groups * qw), BF16),
        scratch_shapes=scratch,
        compiler_params=_cparams(("parallel", "parallel", "arbitrary")),
        name="band_" + mode,
    )(*args)


def _compress_kernel(x_ref, pos_ref, w1_ref, b1_ref, w2_ref, b2_ref, o_ref, *, nck):
    half = NSA_CMP_STRIDE * HEAD_DIM
    c = x_ref[0, 0].astype(F32)
    first = _dot((c + pos_ref[0, :, :half]).astype(BF16), w1_ref[0, :half, :])
    second = _dot((c + pos_ref[0, :, half:]).astype(BF16), w1_ref[0, half:, :])
    hid = jax.nn.gelu(first + pltpu.roll(second, nck - 1, 0) + b1_ref[0])
    o_ref[0, 0] = (_dot(hid.astype(BF16), w2_ref[0]) + b2_ref[0]).astype(o_ref.dtype)


def _compress(chunks, pos, w1, b1, w2d, b2d):
    _, bg, nck, cin = chunks.shape
    return pl.pallas_call(
        functools.partial(_compress_kernel, nck=nck),
        grid=(2, bg),
        in_specs=[pl.BlockSpec((1, 1, nck, cin), lambda s, i: (s, i, 0, 0)),
                  pl.BlockSpec((1, 1, 2 * cin), lambda s, i: (s, 0, 0)),
                  pl.BlockSpec((1, 2 * cin, NSA_CMP_HIDDEN), lambda s, i: (s, 0, 0)),
                  pl.BlockSpec((1, 1, NSA_CMP_HIDDEN), lambda s, i: (s, 0, 0)),
                  pl.BlockSpec((1, NSA_CMP_HIDDEN, LANES), lambda s, i: (s, 0, 0)),
                  pl.BlockSpec((1, 1, LANES), lambda s, i: (s, 0, 0))],
        out_specs=pl.BlockSpec((1, 1, nck, LANES), lambda s, i: (s, i, 0, 0)),
        out_shape=jax.ShapeDtypeStruct((2, bg, nck, LANES), BF16),
        compiler_params=_cparams(("parallel", "parallel")),
        name="nsa_compress",
    )(chunks, pos, w1, b1, w2d, b2d)


def _cmp_kernel(q_ref, kc_ref, vct_ref, ztab_ref, mmat_ref, o_ref, sel_ref, *, hg, ncp, ns, n_sel, tiles):
    for sub in range(tiles):
        _cmp_tile(q_ref, kc_ref, vct_ref, ztab_ref, mmat_ref, o_ref, sel_ref, pl.program_id(2) * tiles + sub,
                  sub * QB, hg=hg, ncp=ncp, ns=ns, n_sel=n_sel)


def _cmp_tile(q_ref, kc_ref, vct_ref, ztab_ref, mmat_ref, o_ref, sel_ref, i, r0, *, hg, ncp, ns, n_sel):
    cols = hg * QB
    qstack = _stack_heads(q_ref, hg, r0)
    s = _dot_nt(kc_ref[0, 0], qstack)
    start = pl.multiple_of(ncp - 8 * i, 8)
    delta = jnp.concatenate([ztab_ref[h, pl.ds(start, ncp), :] for h in range(hg)], axis=1)
    s = s + delta
    m = jnp.max(s, axis=0, keepdims=True)
    e = jnp.exp2(s - m)
    p = e * jnp.where(m > M_INIT, 1.0 / jnp.sum(e, axis=0, keepdims=True), 0.0)
    _store_heads(o_ref, _dot(vct_ref[0, 0], p.astype(BF16)), hg, r0)

    imp = p[:, 0:QB]
    for h in range(1, hg):
        imp = imp + p[:, h * QB:(h + 1) * QB]
    hi, mid, lo = _split3(imp)
    mm = mmat_ref[...]
    imp_sel = _dot(mm, hi) + _dot(mm, mid) + _dot(mm, lo)
    sb = lax.broadcasted_iota(jnp.int32, (ns, QB), 0)
    tb = (i * QB + lax.broadcasted_iota(jnp.int32, (ns, QB), 1)) // NSA_SEL_LEN
    forced = jnp.logical_or(jnp.logical_or(sb == 0, sb == tb), sb == tb - 1)
    score = jnp.where(forced, FORCE_SCORE, jnp.where(sb <= tb, imp_sel, NEG_INF))
    sub = 8
    rows = [score[v * sub:(v + 1) * sub, :] for v in range(ns // sub)]
    ranks = [jnp.zeros((sub, QB), F32) for _ in rows]
    sub_idx = lax.broadcasted_iota(jnp.int32, (sub, QB), 0)
    for r in range(ns):
        other = score[r:r + 1, :]
        for v, mine in enumerate(rows):
            if v < r // sub:
                inc = jnp.where(other > mine, 1.0, 0.0)
            elif v > r // sub:
                inc = jnp.where(other >= mine, 1.0, 0.0)
            else:
                tie = jnp.where(sub_idx > r % sub, 1.0, 0.0)
                inc = jnp.where(other > mine, 1.0, jnp.where(other == mine, tie, 0.0))
            ranks[v] = ranks[v] + inc
    chosen = [jnp.where(rk < n_sel, 1.0, 0.0) for rk in ranks]
    chosen.append(jnp.zeros((LANES - ns, QB), F32))
    sel_ref[0, 0, r0:r0 + QB, :] = jnp.concatenate(chosen, axis=0).T.astype(sel_ref.dtype)


def _cmp_attention(proj, kc, vc_t, ztab, mmat_t, b, t):
    tiles = CMP_TILES
    nb = t // (QB * tiles)
    hg = NSA_GROUP
    qw = hg * HEAD_DIM
    ncp = t // NSA_CMP_STRIDE
    ns = t // NSA_SEL_LEN
    n_sel = min(NSA_SEL_TOPK, ns)
    return pl.pallas_call(
        functools.partial(_cmp_kernel, hg=hg, ncp=ncp, ns=ns, n_sel=n_sel, tiles=tiles),
        grid=(b, NSA_KV_HEADS, nb),
        in_specs=[
            pl.BlockSpec((QB * tiles, qw), lambda bi, g, i: (bi * nb + i, C_NQ // qw + g)),
            pl.BlockSpec((1, 1, ncp, LANES), lambda bi, g, i: (bi, g, 0, 0)),
            pl.BlockSpec((1, 1, HEAD_DIM, ncp), lambda bi, g, i: (bi, g, 0, 0)),
            pl.BlockSpec((hg, 2 * ncp, QB), lambda bi, g, i: (g, 0, 0)),
            pl.BlockSpec((ns, ncp), lambda bi, g, i: (0, 0)),
        ],
        out_specs=[pl.BlockSpec((QB * tiles, qw), lambda bi, g, i: (bi * nb + i, g)),
                   pl.BlockSpec((1, 1, QB * tiles, LANES), lambda bi, g, i: (bi, g, i, 0))],
        out_shape=[jax.ShapeDtypeStruct((b * t, NSA_W), BF16),
                   jax.ShapeDtypeStruct((b, NSA_KV_HEADS, t, LANES), BF16)],
        compiler_params=_cparams(("parallel", "parallel", "arbitrary")),
        name="nsa_cmp_select",
    )(proj, kc, vc_t, ztab, mmat_t)


def _combine_kernel(oc_ref, os_ref, ow_ref, g_ref, e_ref, o_ref):
    sg = jax.nn.sigmoid(g_ref[...]).astype(BF16)
    acc = None
    for br, ref in enumerate((oc_ref, os_ref, ow_ref)):
        term = _dot(sg, e_ref[br]) * ref[...].astype(F32)
        acc = term if acc is None else acc + term
    o_ref[...] = acc.astype(o_ref.dtype)


def _nsa_combine(o_cmp, o_slc, o_win, gates, tm=512):
    n = o_cmp.shape[0]
    spread = np.zeros((NSA_N_BRANCH, D_GATE, NSA_W), np.float32)
    for h in range(NSA_HEADS):
        for br in range(NSA_N_BRANCH):
            spread[br, G_NG + NSA_N_BRANCH * h + br, h * HEAD_DIM:(h + 1) * HEAD_DIM] = 1.0
    spec = pl.BlockSpec((tm, NSA_W), lambda i: (i, 0))
    return pl.pallas_call(
        _combine_kernel,
        grid=(n // tm,),
        in_specs=[spec, spec, spec, pl.BlockSpec((tm, D_GATE), lambda i: (i, 0)),
                  pl.BlockSpec((NSA_N_BRANCH, D_GATE, NSA_W), lambda i: (0, 0, 0))],
        out_specs=spec,
        out_shape=jax.ShapeDtypeStruct((n, NSA_W), BF16),
        compiler_params=_cparams(("parallel",)),
        name="nsa_combine",
    )(o_cmp, o_slc, o_win, gates, jnp.asarray(spread, BF16))


def _ffn_kernel(te_ref, nu_ref, rows_ref, *refs, fused_norm, chained, sub):
    if fused_norm:
        x_ref, g_ref, wg_ref, wu_ref, wd_ref, o_ref, acc_ref, xn_ref = refs
    elif chained:
        x_ref, wg_ref, wu_ref, wd_ref, _, o_ref, acc_ref = refs
    else:
        x_ref, wg_ref, wu_ref, wd_ref, o_ref, acc_ref = refs
    del te_ref
    i = pl.program_id(0)
    k = pl.program_id(1)
    last = pl.num_programs(1) - 1
    used = i < nu_ref[0]
    tm = acc_ref.shape[0]
    src_ref = xn_ref if fused_norm else x_ref

    @pl.when(jnp.logical_and(used, k == 0))
    def _():
        if fused_norm:
            h = x_ref[...]
            y = h * lax.rsqrt(jnp.mean(h * h, axis=-1, keepdims=True) + RMS_EPS)
            xn_ref[...] = (y * g_ref[...]).astype(BF16)
            acc_ref[...] = h
        else:
            acc_ref[...] = jnp.zeros_like(acc_ref)

    def mlp(r0, nrows):
        x = src_ref[r0:r0 + nrows, :]
        gate = _dot(x, wg_ref[0].astype(BF16))
        up = _dot(x, wu_ref[0].astype(BF16))
        hid = (jax.nn.silu(gate) * up).astype(BF16)
        acc_ref[r0:r0 + nrows, :] += _dot(hid, wd_ref[0].astype(BF16))

    full = rows_ref[i] > tm - sub

    @pl.when(jnp.logical_and(used, full))
    def _():
        mlp(0, tm)

    for piece in range(tm // sub - 1):
        @pl.when(jnp.logical_and(jnp.logical_and(used, jnp.logical_not(full)), rows_ref[i] > piece * sub))
        def _():
            mlp(piece * sub, sub)

    @pl.when(jnp.logical_and(used, k == last))
    def _():
        o_ref[...] = acc_ref[...].astype(o_ref.dtype)

    @pl.when(jnp.logical_and(jnp.logical_not(used), k == last))
    def _():
        o_ref[...] = jnp.zeros_like(o_ref)


def _ffn(x, w_gate, w_up, w_down, tile_expert, n_used, tile_rows, out_dtype, gain=None, tm=512, tf=512,
         out_rows=None, out_tile0=0, out_buf=None, sub=256):
    r, d = x.shape
    nk = D_FF // tf
    n_tiles = r // tm
    fused_norm = gain is not None
    out_rows = r if out_rows is None else out_rows

    def tile(i, nu):
        return jnp.minimum(i, jnp.maximum(nu[0] - 1, 0))

    def kk(i, k, nu):
        return jnp.where(i < nu[0], k, nk - 1)

    in_specs = [pl.BlockSpec((tm, d), lambda i, k, te, nu, rw: (tile(i, nu), 0))]
    args = [x]
    scratch = [pltpu.VMEM((tm, d), F32)]
    if fused_norm:
        in_specs.append(pl.BlockSpec((1, d), lambda i, k, te, nu, rw: (0, 0)))
        args.append(gain.reshape(1, d).astype(F32))
        scratch.append(pltpu.VMEM((tm, d), BF16))
    in_specs += [
        pl.BlockSpec((1, d, tf), lambda i, k, te, nu, rw: (te[tile(i, nu)], 0, kk(i, k, nu))),
        pl.BlockSpec((1, d, tf), lambda i, k, te, nu, rw: (te[tile(i, nu)], 0, kk(i, k, nu))),
        pl.BlockSpec((1, tf, d), lambda i, k, te, nu, rw: (te[tile(i, nu)], kk(i, k, nu), 0)),
    ]
    args += [w_gate, w_up, w_down]
    aliases = {}
    if out_buf is not None:
        in_specs.append(pl.BlockSpec(memory_space=pl.ANY))
        args.append(out_buf)
        aliases = {3 + len(args) - 1: 0}
    return pl.pallas_call(
        functools.partial(_ffn_kernel, fused_norm=fused_norm, chained=out_buf is not None, sub=sub),
        grid_spec=pltpu.PrefetchScalarGridSpec(
            num_scalar_prefetch=3,
            grid=(n_tiles, nk),
            in_specs=in_specs,
            out_specs=pl.BlockSpec((tm, d), lambda i, k, te, nu, rw: (i + out_tile0, 0)),
            scratch_shapes=scratch,
        ),
        out_shape=jax.ShapeDtypeStruct((out_rows, d), out_dtype),
        input_output_aliases=aliases,
        compiler_params=_cparams(("arbitrary", "arbitrary"), FFN_VMEM_LIMIT),
        name="swiglu_ffn",
    )(tile_expert, n_used, tile_rows, *args)


def _router_kernel(l_ref, o_ref):
    lane = lax.broadcasted_iota(jnp.int32, l_ref.shape, 1)
    lf = lane.astype(F32)
    lg = jnp.where(lane < N_EXPERTS, l_ref[...], REMOVED)
    v1 = jnp.max(lg, axis=1, keepdims=True)
    i1 = jnp.min(jnp.where(lg == v1, lf, float(LANES)), axis=1, keepdims=True)
    lg2 = jnp.where(lf == i1, REMOVED, lg)
    v2 = jnp.max(lg2, axis=1, keepdims=True)
    i2 = jnp.min(jnp.where(lg2 == v2, lf, float(LANES)), axis=1, keepdims=True)
    e2 = jnp.exp(v2 - v1)
    den = 1.0 + e2
    p1 = 1.0 / den
    p2 = e2 / den
    out = jnp.where(lane == N_EXPERTS, i1, 0.0)
    out = jnp.where(lane == N_EXPERTS + 1, i2, out)
    out = jnp.where(lane == N_EXPERTS + 2, p1, out)
    out = jnp.where(lane == N_EXPERTS + 3, p2, out)
    o_ref[...] = out


def _router_top2(logits, tm=512):
    n = logits.shape[0]
    spec = pl.BlockSpec((tm, LANES), lambda i: (i, 0))
    return pl.pallas_call(
        _router_kernel,
        grid=(n // tm,),
        in_specs=[spec],
        out_specs=spec,
        out_shape=jax.ShapeDtypeStruct((n, LANES), F32),
        compiler_params=_cparams(("parallel",)),
        name="moe_router_top2",
    )(logits)


def _moe_combine_kernel(h_ref, a_ref, b_ref, top_ref, g_ref, o_ref, *, final_norm):
    p0 = top_ref[:, N_EXPERTS + TOP_K:N_EXPERTS + TOP_K + 1]
    p1 = top_ref[:, N_EXPERTS + TOP_K + 1:N_EXPERTS + TOP_K + 2]
    y = h_ref[...] + p0 * a_ref[...].astype(F32) + p1 * b_ref[...].astype(F32)
    if final_norm:
        y = y * lax.rsqrt(jnp.mean(y * y, axis=-1, keepdims=True) + RMS_EPS) * g_ref[...]
    o_ref[...] = y


def _moe_combine(h, ya, yb, top, gain, final_norm, tm=512):
    n, d = h.shape
    row = pl.BlockSpec((tm, d), lambda i: (i, 0))
    return pl.pallas_call(
        functools.partial(_moe_combine_kernel, final_norm=final_norm),
        grid=(n // tm,),
        in_specs=[row, row, row, pl.BlockSpec((tm, LANES), lambda i: (i, 0)),
                  pl.BlockSpec((1, d), lambda i: (0, 0))],
        out_specs=row,
        out_shape=jax.ShapeDtypeStruct((n, d), F32),
        compiler_params=_cparams(("parallel",)),
        name="moe_combine",
    )(h, ya, yb, top, gain.reshape(1, d).astype(F32))


def _t5_bucket_np(dist):
    n = np.maximum(dist, 0)
    max_exact = REL_BUCKETS // 2
    nf = np.maximum(n, 1).astype(np.float32)
    large = max_exact + (np.log(nf / np.float32(max_exact)) / np.float32(math.log(REL_MAX_DIST / max_exact))
                         * np.float32(REL_BUCKETS - max_exact)).astype(np.int32)
    large = np.minimum(large, REL_BUCKETS - 1)
    return np.where(n < max_exact, n, large).astype(np.int32)


def _lookup(table, idx):
    onehot = (idx.reshape(-1, 1) == np.arange(table.shape[0])[None, :]).astype(np.float32)
    out = jnp.dot(jnp.asarray(onehot), table, precision=lax.Precision.HIGHEST)
    return out.reshape(idx.shape + (table.shape[1],))


def _band_tables(rel_tab, t):
    q = np.arange(QB)[:, None]
    k = np.arange(QB)[None, :]
    rel = rel_tab.astype(F32) * LOG2E
    toep = _lookup(rel, _t5_bucket_np((q - k) % QB))
    toep = toep.transpose(2, 0, 1)
    before = jnp.asarray(k > q)
    far = rel[REL_BUCKETS - 1, SWA_HEADS:]

    def tiles(tab, fill_prev, extra=()):
        heads = tab.shape[0]
        parts = [jnp.where(before, tab, fill_prev), jnp.where(before, NEG_INF, tab)]
        parts += [jnp.broadcast_to(e, tab.shape) for e in extra]
        return jnp.stack([p.reshape(2, heads // 2 * QB, QB) for p in parts], axis=1)

    tabs_swa = tiles(toep[:SWA_HEADS], NEG_INF)
    tabs_nsa = tiles(toep[SWA_HEADS:] - far[:, None, None], 0.0, extra=[jnp.where(before, 0.0, NEG_INF)])
    ncp = t // NSA_CMP_STRIDE
    m = np.arange(-9, 7)[:, None]
    qr = np.arange(QB)[None, :]
    d = qr - NSA_CMP_STRIDE * m - (NSA_CMP_LEN - 1)
    vals = _lookup(rel[:, SWA_HEADS:], _t5_bucket_np(np.clip(d, 0, None)))
    vals = vals.transpose(2, 0, 1) - far[:, None, None]
    band = jnp.where(jnp.asarray(d >= 0), jnp.where(jnp.asarray(d < REL_MAX_DIST), vals, 0.0), NEG_INF)
    ztab = jnp.concatenate([jnp.zeros((NSA_HEADS, ncp - 9, QB), F32), band,
                            jnp.full((NSA_HEADS, ncp - 7, QB), NEG_INF, F32)], axis=1)
    return tabs_swa, tabs_nsa, ztab


def _selection_matrices(t):
    ncp = t // NSA_CMP_STRIDE
    ns = t // NSA_SEL_LEN
    per = NSA_SEL_LEN // NSA_CMP_STRIDE
    ratio = NSA_CMP_LEN // NSA_CMP_STRIDE
    mmat_t = np.zeros((ns, ncp), np.float32)
    for n in range(ncp - 1):
        for j in range(ratio):
            mmat_t[(n + j) // per, n] += 1.0
    emat_t = (np.arange(t)[:, None] // NSA_SEL_LEN == np.arange(LANES)[None, :]).astype(np.float32)
    return jnp.asarray(mmat_t, BF16), (jnp.asarray(emat_t.T, BF16), jnp.asarray(emat_t, BF16))


def _dup(w):
    d = w.shape[0]
    w = w.reshape(d, -1, 1, HEAD_DIM)
    return jnp.broadcast_to(w, (d, w.shape[1], 2, HEAD_DIM)).reshape(d, -1)


def _prep_in_weights(w_in_l):
    sizes = (FOX_W, FOX_W, FOX_W, FOX_HEADS, SWA_W, SWA_KV_W, SWA_KV_W,
             NSA_W, NSA_KV_W, NSA_KV_W, NSA_KV_W, NSA_KV_W, NSA_KV_W, NSA_KV_W, NSA_HEADS * NSA_N_BRANCH)
    splits = [int(s) for s in np.cumsum(sizes)[:-1]]
    (fq, fk, fv, ff, sq, sk, sv, nq, nkc, nvc, nks, nvs, nkw, nvw, ng) = jnp.split(w_in_l, splits, axis=-1)
    scale = HEAD_DIM ** -0.5 * LOG2E
    cols = [fq * scale, fk, fv, sq * scale, nq * scale, _dup(sk), _dup(nks), _dup(nkw), sv, nvs, nvw, nkc, nvc]
    cols.append(jnp.zeros((w_in_l.shape[0], D_PROJ - sum(c.shape[1] for c in cols)), w_in_l.dtype))
    w_proj = jnp.concatenate(cols, axis=-1).astype(BF16)
    pad = jnp.zeros((w_in_l.shape[0], D_GATE - FOX_HEADS - NSA_HEADS * NSA_N_BRANCH), w_in_l.dtype)
    w_gate = jnp.concatenate([ff, ng, pad], axis=-1).astype(BF16)
    return w_proj, w_gate


def _compress_inputs(proj, b, t):
    g = NSA_KV_HEADS
    nck = t // NSA_CMP_STRIDE
    x = proj[:, C_NKC:C_NKC + 2 * NSA_KV_W].reshape(b, nck, NSA_CMP_STRIDE, 2, g, HEAD_DIM)
    return x.transpose(3, 0, 4, 1, 2, 5).reshape(2, b * g, nck, NSA_CMP_STRIDE * HEAD_DIM)


def _mixer(h, gain, b, t, w_in_l, forget_bias, sinks, cmp_pos, cmp_w1, cmp_b1, cmp_w2, cmp_b2, tables):
    tabs_swa, tabs_nsa, ztab, mmat_t, emats = tables
    w_proj, w_gate = _prep_in_weights(w_in_l)
    proj, gates = _in_proj(h, gain, w_proj, w_gate)
    fv_t = proj[:, C_FV:C_FV + FOX_W].reshape(b, t, FOX_W).transpose(0, 2, 1)
    v_t = proj[:, C_V:C_V + 3 * NSA_KV_W].reshape(b, t, 3 * NSA_KV_W).transpose(0, 2, 1)

    qx, kx = _fox_decay_operands(gates, forget_bias, b, t)
    o_fox = _fox_attention(proj, fv_t, qx, kx, b, t)

    o_swa = _band_attention(proj, v_t, tabs_swa, b, t, mode="swa", hg=SWA_GROUP,
                            c_q=C_SQ, c_k=C_SK, v_blk=0, sinks=sinks)

    flat = _compress_inputs(proj, b, t)
    pos = cmp_pos.reshape(2, 1, NSA_CMP_LEN * HEAD_DIM).astype(F32)
    w2d = jnp.concatenate([cmp_w2, cmp_w2], axis=-1).astype(BF16)
    b2d = jnp.concatenate([cmp_b2, cmp_b2], axis=-1).reshape(2, 1, LANES).astype(F32)
    kvc = _compress(flat, pos, cmp_w1.astype(BF16), cmp_b1.reshape(2, 1, NSA_CMP_HIDDEN).astype(F32), w2d, b2d)
    ncp = t // NSA_CMP_STRIDE
    kvc = kvc.reshape(2, b, NSA_KV_HEADS, ncp, LANES)
    vc_t = kvc[1, :, :, :, :HEAD_DIM].transpose(0, 1, 3, 2)
    o_cmp, sel = _cmp_attention(proj, kvc[0], vc_t, ztab, mmat_t, b, t)
    o_slc = _band_attention(proj, v_t, tabs_nsa, b, t, mode="sel", hg=NSA_GROUP,
                            c_q=C_NQ, c_k=C_NKS, v_blk=2, sel=sel, emats=emats)
    o_win = _band_attention(proj, v_t, tabs_nsa, b, t, mode="win", hg=NSA_GROUP,
                            c_q=C_NQ, c_k=C_NKW, v_blk=4)
    o_nsa = _nsa_combine(o_cmp, o_slc, o_win, gates)
    return o_fox, o_swa, o_nsa


def _moe(hn, h, router, w_gate, w_up, w_down, gain, final_norm, tm=1024, tf=512):
    n, d = hn.shape
    w_r = jnp.zeros((d, LANES), BF16).at[:, :N_EXPERTS].set(router.astype(BF16))
    logits = _matmul([hn], [w_r], F32, tn=LANES, name="router_logits")
    top = _router_top2(logits)
    e_idx = top[:, N_EXPERTS:N_EXPERTS + TOP_K].astype(jnp.int32)
    e_flat = e_idx.reshape(-1)
    onehot = (e_flat[:, None] == jnp.arange(N_EXPERTS)[None, :]).astype(jnp.int32)
    csum = jnp.cumsum(onehot, axis=0)
    counts = csum[-1]
    rank = jnp.take_along_axis(csum, e_flat[:, None], axis=1)[:, 0] - 1
    padded = ((counts + tm - 1) // tm) * tm
    ends = jnp.cumsum(padded)
    starts = ends - padded
    dest = starts[e_flat] + rank
    r_pad = n * TOP_K + N_EXPERTS * tm
    src_tok = jnp.zeros((r_pad,), jnp.int32).at[dest].set(jnp.arange(n * TOP_K, dtype=jnp.int32) // TOP_K)
    tile_start = jnp.arange(r_pad // tm, dtype=jnp.int32) * tm
    tile_expert = jnp.minimum(jnp.sum(tile_start[:, None] >= ends[None, :], axis=1), N_EXPERTS - 1).astype(jnp.int32)
    n_used = (ends[-1:] // tm).astype(jnp.int32)
    tile_rows = jnp.clip((starts + counts)[tile_expert] - tile_start, 0, tm).astype(jnp.int32)
    n_tiles = r_pad // tm
    per = n_tiles // MOE_CHUNKS
    y = jnp.zeros((r_pad, d), BF16)
    for c in range(MOE_CHUNKS):
        xs = hn.at[src_tok[c * per * tm:(c + 1) * per * tm]].get(mode="promise_in_bounds")
        used_c = jnp.clip(n_used - c * per, 0, per)
        y = _ffn(xs, w_gate, w_up, w_down, tile_expert[c * per:(c + 1) * per], used_c,
                 tile_rows[c * per:(c + 1) * per], BF16, tm=tm, tf=tf,
                 out_rows=r_pad, out_tile0=c * per, out_buf=y)
    dest = dest.reshape(n, TOP_K)
    ya = y.at[dest[:, 0]].get(mode="promise_in_bounds")
    yb = y.at[dest[:, 1]].get(mode="promise_in_bounds")
    return _moe_combine(h, ya, yb, top, gain, final_norm)


def kernel(x, attn_norm, w_in, fox_forget_bias, swa_sinks, nsa_cmp_pos, nsa_cmp_w1, nsa_cmp_b1, nsa_cmp_w2,
           nsa_cmp_b2, w_out, rel_bias_table, ffn_norm, dense_w_gate, dense_w_up, dense_w_down, moe_router,
           moe_w_gate, moe_w_up, moe_w_down, final_norm):
    b, t, d = x.shape
    n = b * t
    depth = w_in.shape[0]
    tables = _band_tables(rel_bias_table, t) + _selection_matrices(t)
    h = x.reshape(n, d)
    for layer in range(depth):
        o_fox, o_swa, o_nsa = _mixer(h, attn_norm[layer], b, t, w_in[layer], fox_forget_bias[layer],
                                     swa_sinks[layer], nsa_cmp_pos[layer], nsa_cmp_w1[layer], nsa_cmp_b1[layer],
                                     nsa_cmp_w2[layer], nsa_cmp_b2[layer], tables)
        wo = w_out[layer].astype(BF16)
        h = _matmul([o_fox, o_swa, o_nsa], [wo[:FOX_W], wo[FOX_W:FOX_W + SWA_W], wo[FOX_W + SWA_W:]],
                    F32, residual=h, tn=1024, name="out_proj")
        i = layer // 2
        last = layer == depth - 1
        if layer % 2 == 0:
            tm = 512
            zeros = jnp.zeros((n // tm,), jnp.int32)
            h = _ffn(h, dense_w_gate[i][None].astype(BF16), dense_w_up[i][None].astype(BF16),
                     dense_w_down[i][None].astype(BF16), zeros, jnp.full((1,), n // tm, jnp.int32),
                     jnp.full((n // tm,), tm, jnp.int32), F32, gain=ffn_norm[layer], tm=tm)
            if last:
                h = _rmsnorm(h, final_norm, F32)
        else:
            hn = _rmsnorm(h, ffn_norm[layer], BF16)
            h = _moe(hn, h, moe_router[i], moe_w_gate[i], moe_w_up[i], moe_w_down[i], final_norm, last)
    return h.reshape(b, t, d)
```
